```python
import math
import jax
import jax.numpy as jnp
from jax import lax
import numpy as np

D_MODEL = 2048
BATCH = 4
SEQ = 2048
DEPTH = 4
DEC_BATCH = 128
DEC_SEQ = 1
PAST_LEN = 16384
PAGE_SIZE = 128

N_MIXERS = 4
GW = D_MODEL // N_MIXERS
RWKV_HD = 64
RWKV_H = GW // RWKV_HD
RWKV_DECAY_LORA = 64
RWKV_AAA_LORA = 64
RWKV_GATE_LORA = 128
RWKV_LN_EPS = 64e-5
ML_H = 4
ML_HD = GW // ML_H
HG_H = 4
HG_DK = 128
HG_DV = GW // HG_H
GDN_H = 4
GDN_HD = GW // GDN_H
CONV_W = 4
CHUNK = 64
N_EXPERT_GROUPS = 4
EXPERTS_PER_GROUP = 8
N_EXPERTS = N_EXPERT_GROUPS * EXPERTS_PER_GROUP
EXPERT_TOP_K = 2
D_FF_EXPERT = 256
NORM_EPS = 1e-6

RWKV_SPLIT = (GW, GW, GW, RWKV_DECAY_LORA, RWKV_AAA_LORA, RWKV_GATE_LORA)
ML_SPLIT = (GW, GW, GW, GW, ML_H, ML_H)
HG_SPLIT = (HG_H * HG_DK, HG_H * HG_DK, GW, GW)
GDN_SPLIT = (3 * GW, GW, GDN_H, GDN_H)
RWKV_PROJ = sum(RWKV_SPLIT)
ML_PROJ = sum(ML_SPLIT)
HG_PROJ = sum(HG_SPLIT)
GDN_PROJ = sum(GDN_SPLIT)
MIX_SPLIT = (RWKV_PROJ, ML_PROJ, HG_PROJ, GDN_PROJ)
PROJ = sum(MIX_SPLIT)

kernel_name = 'hybrid_rwkv7_mlstm_hgrn2_gdn_hmoe_step'

F32 = jnp.float32


def _split(x, sizes):
    return jnp.split(x, np.cumsum(sizes)[:-1].tolist(), axis=-1)


def _rmsnorm(x, g):
    xf = x.astype(F32)
    return xf * lax.rsqrt(jnp.mean(xf * xf, -1, keepdims=True) + NORM_EPS) * g.astype(F32)


def _head_rmsnorm(y, g):
    B, T, H, Dh = y.shape
    y = y * lax.rsqrt(jnp.mean(y * y, -1, keepdims=True) + NORM_EPS)
    return y.reshape(B, T, H * Dh) * g.astype(F32)


def _l2norm(x):
    return x * lax.rsqrt(jnp.maximum(jnp.sum(x * x, -1, keepdims=True), 1e-12))


def _to_chunks(x, L):
    B, T = x.shape[:2]
    x = x.reshape((B, T // L, L) + x.shape[2:])
    return jnp.moveaxis(x, (1, 3), (0, 2))


def _from_chunks(x):
    x = jnp.moveaxis(x, (0, 2), (1, 3))
    return x.reshape((x.shape[0], x.shape[1] * x.shape[2]) + x.shape[3:])


def _causal_conv(u, buf, w):
    ext = jnp.concatenate([buf, u], axis=1)
    y = lax.conv_general_dilated(ext, w[:, None, :], window_strides=(1,), padding='VALID',
                                 dimension_numbers=('NWC', 'WIO', 'NWC'),
                                 feature_group_count=u.shape[-1])
    return y, ext[:, u.shape[1]:]


def _rwkv_step(S, inp):
    r, w, k, v, a, b = inp
    sa = jnp.einsum('bhvk,bhk->bhv', S, a)
    S = S * w[:, :, None, :] + sa[..., None] * b[:, :, None, :] + v[..., None] * k[:, :, None, :]
    return S, jnp.einsum('bhvk,bhk->bhv', S, r)


def _rwkv7(p, shift_buf, S0, mu, w0, w_up, a0, a_up, g_up, k_k, k_a, r_k, ln_w, ln_b):
    B, T, _ = p.shape
    prev = jnp.concatenate([shift_buf.astype(F32), p], axis=1)[:, :-1]
    xs = p + (prev - p) * mu
    r, k, v, dw, da, dg = _split(xs, RWKV_SPLIT)
    logw = -jax.nn.softplus(-(w0 + jnp.tanh(dw) @ w_up)) - 0.5
    a = jax.nn.sigmoid(a0 + da @ a_up)
    g = jax.nn.sigmoid(dg) @ g_up
    heads = lambda t: t.reshape(B, T, RWKV_H, RWKV_HD)
    kk = _l2norm(heads(k * k_k))
    k = heads(k * (1.0 + (a - 1.0) * k_a))
    r, v, a = heads(r), heads(v), heads(a)
    steps = (r, jnp.exp(-jnp.exp(heads(logw))), k, v, -kk, kk * a)
    S, y = lax.scan(_rwkv_step, S0.astype(F32), [jnp.moveaxis(t, 1, 0) for t in steps])
    y = jnp.moveaxis(y, 0, 1)
    mean = jnp.mean(y, -1, keepdims=True)
    var = jnp.mean(jnp.square(y - mean), -1, keepdims=True)
    y = ((y - mean) * lax.rsqrt(var + RWKV_LN_EPS)).reshape(B, T, GW) * ln_w + ln_b
    bonus = jnp.sum(r * k * r_k.reshape(RWKV_H, RWKV_HD), -1, keepdims=True) * v
    y = (y + bonus.reshape(B, T, GW)) * g
    return y, p[:, -1:], S


def _mlstm_chunk(carry, inp):
    C, n, m = carry
    q, k, v, ig, lf = inp
    L = q.shape[2]
    causal = jnp.tril(jnp.ones((L, L), dtype=bool))
    b = jnp.cumsum(lf, axis=-1)
    D = jnp.where(causal, b[..., :, None] - b[..., None, :] + ig[..., None, :], -jnp.inf)
    g = b + m[..., None]
    mt = jnp.maximum(g, jnp.max(D, axis=-1))
    s = jnp.einsum('bhtk,bhsk->bhts', q, k) * jnp.exp(D - mt[..., None])
    wg = jnp.exp(g - mt)
    num = jnp.einsum('bhts,bhsv->bhtv', s, v) + wg[..., None] * jnp.einsum('bhvk,bhtk->bhtv', C, q)
    den = jnp.sum(s, -1) + wg * jnp.einsum('bhk,bhtk->bht', n, q)
    h = num / jnp.maximum(jnp.abs(den), jnp.exp(-mt))[..., None]
    mL = mt[..., -1]
    wk = jnp.exp(D[..., -1, :] - mL[..., None])
    decay = jnp.exp(g[..., -1] - mL)
    C = decay[..., None, None] * C + jnp.einsum('bhs,bhsv,bhsk->bhvk', wk, v, k)
    n = decay[..., None] * n + jnp.einsum('bhs,bhsk->bhk', wk, k)
    return (C, n, mL), h


def _mlstm(p, C0, n0, m0, i_bias, f_bias, norm_w):
    B, T, _ = p.shape
    q, k, v, o, ip, fp = _split(p, ML_SPLIT)
    heads = lambda t: t.reshape(B, T, ML_H, ML_HD)
    L = math.gcd(T, CHUNK)
    xs = (_to_chunks(heads(q), L), _to_chunks(heads(k) * ML_HD ** -0.5, L), _to_chunks(heads(v), L),
          _to_chunks(ip + i_bias, L), _to_chunks(jax.nn.log_sigmoid(fp + f_bias), L))
    (C, n, m), h = lax.scan(_mlstm_chunk, (C0.astype(F32), n0.astype(F32), m0.astype(F32)), xs)
    y = _head_rmsnorm(_from_chunks(h), norm_w) * jax.nn.sigmoid(o)
    return y, C, n, m


def _hgrn_chunk(S, inp):
    q, k, v, lg = inp
    L = q.shape[2]
    causal = jnp.tril(jnp.ones((L, L), dtype=bool))[..., None]
    cg = jnp.cumsum(lg, axis=2)
    dec = jnp.exp(jnp.where(causal, cg[:, :, :, None, :] - cg[:, :, None, :, :], -jnp.inf))
    A = jnp.einsum('bhtk,bhsk,bhtsk->bhts', q, k, dec)
    o = jnp.einsum('bhts,bhsv->bhtv', A, v) + jnp.einsum('bhtk,bhkv->bhtv', q * jnp.exp(cg), S)
    S = jnp.exp(cg[:, :, -1])[..., None] * S + jnp.einsum('bhsk,bhsv->bhkv', k * jnp.exp(cg[:, :, -1:] - cg), v)
    return S, o


def _hgrn2(p, S0, lb, norm_w):
    B, T, _ = p.shape
    q, f, i, g = _split(p, HG_SPLIT)
    fg = lb + (1.0 - lb) * jax.nn.sigmoid(f)
    hk = lambda t: t.reshape(B, T, HG_H, HG_DK)
    L = math.gcd(T, CHUNK)
    xs = (_to_chunks(hk(jax.nn.silu(q)), L), _to_chunks(hk(1.0 - fg), L),
          _to_chunks(i.reshape(B, T, HG_H, HG_DV), L), _to_chunks(hk(jnp.log(fg)), L))
    S, o = lax.scan(_hgrn_chunk, S0.astype(F32), xs)
    y = _head_rmsnorm(_from_chunks(o), norm_w) * jax.nn.silu(g)
    return y, S


def _gdn_chunk(S, inp):
    q, k, u, w, qk, cg = inp
    u = u - jnp.einsum('bhtk,bhkv->bhtv', w, S)
    o = jnp.einsum('bhtk,bhkv->bhtv', q * jnp.exp(cg)[..., None], S) + jnp.einsum('bhts,bhsv->bhtv', qk, u)
    S = jnp.exp(cg[..., -1])[..., None, None] * S + jnp.einsum(
        'bhsk,bhsv->bhkv', k * jnp.exp(cg[..., -1:] - cg)[..., None], u)
    return S, o


def _gdn(p, conv_buf, S0, conv_w, a_log, dt_bias, norm_w):
    B, T, _ = p.shape
    qkv, g, bp, ap = _split(p, GDN_SPLIT)
    qkv, new_buf = _causal_conv(qkv, conv_buf.astype(F32), conv_w.astype(F32))
    q, k, v = jnp.split(jax.nn.silu(qkv), 3, axis=-1)
    heads = lambda t: t.reshape(B, T, GDN_H, GDN_HD)
    L = math.gcd(T, CHUNK)
    q = _to_chunks(_l2norm(heads(q)) * GDN_HD ** -0.5, L)
    k = _to_chunks(_l2norm(heads(k)), L)
    v = _to_chunks(heads(v), L)
    beta = _to_chunks(jax.nn.sigmoid(bp), L)
    cg = jnp.cumsum(_to_chunks(-jnp.exp(a_log) * jax.nn.softplus(ap + dt_bias), L), axis=-1)
    diff = cg[..., :, None] - cg[..., None, :]
    causal = jnp.tril(jnp.ones((L, L), dtype=bool))
    strict = jnp.tril(jnp.ones((L, L), dtype=bool), -1)
    kb = k * beta[..., None]
    IA = jnp.eye(L, dtype=F32) + jnp.einsum('...tk,...sk->...ts', kb, k) * jnp.exp(jnp.where(strict, diff, -jnp.inf))
    solve = lambda rhs: lax.linalg.triangular_solve(IA, rhs, left_side=True, lower=True, unit_diagonal=True)
    u = solve(v * beta[..., None])
    w = solve(kb * jnp.exp(cg)[..., None])
    qk = jnp.einsum('...tk,...sk->...ts', q, k) * jnp.exp(jnp.where(causal, diff, -jnp.inf))
    S, o = lax.scan(_gdn_chunk, S0.astype(F32), (q, k, u, w, qk, cg))
    y = _head_rmsnorm(_from_chunks(o), norm_w) * jax.nn.silu(g)
    return y, new_buf, S


def _moe(h, w_group, b_group, w_router, b_router, w_gate, w_up, w_down):
    B, T, D = h.shape
    hf = h.reshape(B * T, D)
    gl = (hf @ w_group).astype(F32)
    gp = jax.nn.softmax(gl, axis=-1)
    gsel = jnp.argmax(gl + b_group, axis=-1)
    pg = jnp.take_along_axis(gp, gsel[:, None], axis=-1)
    el = (hf @ w_router).astype(F32).reshape(-1, N_EXPERT_GROUPS, EXPERTS_PER_GROUP)
    el_g = jnp.take_along_axis(el, gsel[:, None, None], axis=1)[:, 0]
    _, idx = lax.top_k(el_g + b_router.reshape(N_EXPERT_GROUPS, EXPERTS_PER_GROUP)[gsel], EXPERT_TOP_K)
    pe = jax.nn.softmax(jnp.take_along_axis(el_g, idx, axis=-1), axis=-1)
    eid = gsel[:, None] * EXPERTS_PER_GROUP + idx
    combine = jnp.einsum('nk,nke->ne', pg * pe, jax.nn.one_hot(eid, N_EXPERTS, dtype=F32))
    y = jnp.zeros(hf.shape, F32)
    for e in range(N_EXPERTS):
        hid = jax.nn.silu(hf @ w_gate[e]) * (hf @ w_up[e])
        y = y + combine[:, e:e + 1] * (hid @ w_down[e])
    return y.reshape(B, T, D)


def _trunk(x, c, states, P, hg_lbs):
    sh_s, wkv_s, mc_s, mn_s, mm_s, hg_s, cv_s, gd_s = states
    new = [[] for _ in range(8)]
    cs = jax.nn.silu(c.astype(F32))
    for l in range(DEPTH):
        mod = cs @ P['ada_w'][l] + P['ada_b'][l]
        sh1, sc1, gt1, sh2, sc2, gt2 = jnp.split(mod[:, None, :], 6, axis=-1)
        h = (_rmsnorm(x, P['norm1'][l]) * (1.0 + sc1) + sh1).astype(x.dtype)
        p = (h @ P['w_in'][l]).astype(F32)
        pr, pm, ph, pg = _split(p, MIX_SPLIT)
        yr, n_sh, n_wkv = _rwkv7(pr, sh_s[l], wkv_s[l], P['rwkv_mu'][l], P['rwkv_w0'][l], P['rwkv_w_up'][l],
                                 P['rwkv_a0'][l], P['rwkv_a_up'][l], P['rwkv_g_up'][l], P['rwkv_k_k'][l],
                                 P['rwkv_k_a'][l], P['rwkv_r_k'][l], P['rwkv_ln_w'][l], P['rwkv_ln_b'][l])
        ym, n_c, n_n, n_m = _mlstm(pm, mc_s[l], mn_s[l], mm_s[l], P['ml_i_bias'][l], P['ml_f_bias'][l],
                                   P['ml_norm'][l])
        yh, n_hg = _hgrn2(ph, hg_s[l], hg_lbs[l], P['hg_norm'][l])
        yg, n_cv, n_gd = _gdn(pg, cv_s[l], gd_s[l], P['gdn_conv_w'][l], P['gdn_a_log'][l],
                              P['gdn_dt_bias'][l], P['gdn_norm'][l])
        mix = jnp.concatenate([yr, ym, yh, yg], axis=-1).astype(x.dtype) @ P['w_out'][l]
        x = (x + gt1 * mix).astype(x.dtype)
        h2 = (_rmsnorm(x, P['norm2'][l]) * (1.0 + sc2) + sh2).astype(x.dtype)
        ff = _moe(h2, P['moe_w_group'][l], P['moe_b_group'][l], P['moe_w_router'][l], P['moe_b_router'][l],
                  P['moe_w_gate'][l], P['moe_w_up'][l], P['moe_w_down'][l])
        x = (x + gt2 * ff).astype(x.dtype)
        for lst, s in zip(new, (n_sh, n_wkv, n_c, n_n, n_m, n_hg, n_cv, n_gd)):
            lst.append(s)
    y = _rmsnorm(x, P['norm_f']).astype(x.dtype)
    return y, [jnp.stack(lst).astype(x.dtype) for lst in new]


def setup_inputs(seed: int = 0) -> dict:
    key = jax.random.key(seed)
    ks = iter(jax.random.split(key, 64))
    nrm = lambda shape, s=1.0: s * jax.random.normal(next(ks), shape, F32)
    uni = lambda shape, lo, hi: jax.random.uniform(next(ks), shape, F32, lo, hi)
    Ld = DEPTH
    dt = jnp.exp(uni((Ld, GDN_H), math.log(1e-3), math.log(1e-1)))
    return {
        'x_prompt': nrm((BATCH, SEQ, D_MODEL)),
        'x_sample': nrm((DEC_BATCH, DEC_SEQ, D_MODEL)),
        'state_rwkv_shift': nrm((DEPTH, DEC_BATCH, 1, RWKV_PROJ)),
        'state_rwkv_wkv': nrm((DEPTH, DEC_BATCH, RWKV_H, RWKV_HD, RWKV_HD), 0.3),
        'state_mlstm_c': nrm((DEPTH, DEC_BATCH, ML_H, ML_HD, ML_HD), 0.3),
        'state_mlstm_n': nrm((DEPTH, DEC_BATCH, ML_H, ML_HD), 0.3),
        'state_mlstm_m': nrm((DEPTH, DEC_BATCH, ML_H)),
        'state_hgrn': nrm((DEPTH, DEC_BATCH, HG_H, HG_DK, HG_DV), 0.3),
        'state_gdn_conv': nrm((DEPTH, DEC_BATCH, CONV_W - 1, 3 * GW)),
        'state_gdn': nrm((DEPTH, DEC_BATCH, GDN_H, GDN_HD, GDN_HD), 0.1),
        'c_prompt': nrm((BATCH, D_MODEL)),
        'c_sample': nrm((DEC_BATCH, D_MODEL)),
        'ada_w': nrm((Ld, D_MODEL, 6 * D_MODEL), 0.5 * D_MODEL ** -0.5),
        'ada_b': nrm((Ld, 6 * D_MODEL), 0.02),
        'norm1': 1.0 + nrm((Ld, D_MODEL), 0.1),
        'norm2': 1.0 + nrm((Ld, D_MODEL), 0.1),
        'norm_f': 1.0 + nrm((D_MODEL,), 0.1),
        'w_in': nrm((Ld, D_MODEL, PROJ), D_MODEL ** -0.5),
        'w_out': nrm((Ld, D_MODEL, D_MODEL), D_MODEL ** -0.5),
        'rwkv_mu': uni((Ld, RWKV_PROJ), 0.0, 1.0),
        'rwkv_w0': uni((Ld, GW), -6.0, -1.0),
        'rwkv_w_up': nrm((Ld, RWKV_DECAY_LORA, GW), 0.5 * RWKV_DECAY_LORA ** -0.5),
        'rwkv_a0': nrm((Ld, GW), 0.1),
        'rwkv_a_up': nrm((Ld, RWKV_AAA_LORA, GW), 0.5 * RWKV_AAA_LORA ** -0.5),
        'rwkv_g_up': nrm((Ld, RWKV_GATE_LORA, GW), RWKV_GATE_LORA ** -0.5),
        'rwkv_k_k': 0.85 + nrm((Ld, GW), 0.05),
        'rwkv_k_a': 1.0 + nrm((Ld, GW), 0.05),
        'rwkv_r_k': nrm((Ld, GW), 0.1),
        'rwkv_ln_w': 1.0 + nrm((Ld, GW), 0.1),
        'rwkv_ln_b': nrm((Ld, GW), 0.02),
        'ml_i_bias': nrm((Ld, ML_H), 0.1),
        'ml_f_bias': uni((Ld, ML_H), 3.0, 6.0),
        'ml_norm': 1.0 + nrm((Ld, GW), 0.1),
        'hg_lb': nrm((Ld, HG_H * HG_DK)),
        'hg_norm': 1.0 + nrm((Ld, GW), 0.1),
        'gdn_conv_w': nrm((Ld, CONV_W, 3 * GW), CONV_W ** -0.5),
        'gdn_a_log': jnp.log(uni((Ld, GDN_H), 1.0, 16.0)),
        'gdn_dt_bias': dt + jnp.log(-jnp.expm1(-dt)),
        'gdn_norm': 1.0 + nrm((Ld, GW), 0.1),
        'moe_w_group': nrm((Ld, D_MODEL, N_EXPERT_GROUPS), D_MODEL ** -0.5),
        'moe_b_group': nrm((Ld, N_EXPERT_GROUPS), 0.01),
        'moe_w_router': nrm((Ld, D_MODEL, N_EXPERTS), D_MODEL ** -0.5),
        'moe_b_router': nrm((Ld, N_EXPERTS), 0.01),
        'moe_w_gate': nrm((Ld, N_EXPERTS, D_MODEL, D_FF_EXPERT), D_MODEL ** -0.5),
        'moe_w_up': nrm((Ld, N_EXPERTS, D_MODEL, D_FF_EXPERT), D_MODEL ** -0.5),
        'moe_w_down': nrm((Ld, N_EXPERTS, D_FF_EXPERT, D_MODEL), D_FF_EXPERT ** -0.5),
    }


def reference(x_prompt, x_sample, state_rwkv_shift, state_rwkv_wkv, state_mlstm_c, state_mlstm_n,
              state_mlstm_m, state_hgrn, state_gdn_conv, state_gdn, c_prompt, c_sample,
              ada_w, ada_b, norm1, norm2, norm_f, w_in, w_out,
              rwkv_mu, rwkv_w0, rwkv_w_up, rwkv_a0, rwkv_a_up, rwkv_g_up, rwkv_k_k, rwkv_k_a, rwkv_r_k,
              rwkv_ln_w, rwkv_ln_b, ml_i_bias, ml_f_bias, ml_norm, hg_lb, hg_norm,
              gdn_conv_w, gdn_a_log, gdn_dt_bias, gdn_norm,
              moe_w_group, moe_b_group, moe_w_router, moe_b_router, moe_w_gate, moe_w_up, moe_w_down):
    P = dict(ada_w=ada_w, ada_b=ada_b, norm1=norm1, norm2=norm2, norm_f=norm_f, w_in=w_in, w_out=w_out,
             rwkv_mu=rwkv_mu, rwkv_w0=rwkv_w0, rwkv_w_up=rwkv_w_up, rwkv_a0=rwkv_a0, rwkv_a_up=rwkv_a_up,
             rwkv_g_up=rwkv_g_up, rwkv_k_k=rwkv_k_k, rwkv_k_a=rwkv_k_a, rwkv_r_k=rwkv_r_k,
             rwkv_ln_w=rwkv_ln_w, rwkv_ln_b=rwkv_ln_b, ml_i_bias=ml_i_bias, ml_f_bias=ml_f_bias,
             ml_norm=ml_norm, hg_norm=hg_norm, gdn_conv_w=gdn_conv_w, gdn_a_log=gdn_a_log,
             gdn_dt_bias=gdn_dt_bias, gdn_norm=gdn_norm, moe_w_group=moe_w_group, moe_b_group=moe_b_group,
             moe_w_router=moe_w_router, moe_b_router=moe_b_router, moe_w_gate=moe_w_gate,
             moe_w_up=moe_w_up, moe_w_down=moe_w_down)
    lbs = jax.nn.softmax(hg_lb.astype(F32), axis=0)
    hg_lbs = jnp.cumsum(lbs, axis=0) - lbs[0]
    bp, dt = x_prompt.shape[0], x_prompt.dtype
    z = lambda *s: jnp.zeros((DEPTH, bp) + s, dt)
    init = (z(1, RWKV_PROJ), z(RWKV_H, RWKV_HD, RWKV_HD), z(ML_H, ML_HD, ML_HD), z(ML_H, ML_HD), z(ML_H),
            z(HG_H, HG_DK, HG_DV), z(CONV_W - 1, 3 * GW), z(GDN_H, GDN_HD, GDN_HD))
    y_prompt, sp = _trunk(x_prompt, c_prompt, init, P, hg_lbs)
    past = (state_rwkv_shift, state_rwkv_wkv, state_mlstm_c, state_mlstm_n, state_mlstm_m,
            state_hgrn, state_gdn_conv, state_gdn)
    y_sample, ss = _trunk(x_sample, c_sample, past, P, hg_lbs)
    (p_shift, p_wkv, p_mc, p_mn, p_mm, p_hg, p_cv, p_gd) = sp
    (s_shift, s_wkv, s_mc, s_mn, s_mm, s_hg, s_cv, s_gd) = ss
    return (y_prompt, y_sample, p_shift, p_wkv, p_mc, p_mn, p_mm, p_hg, p_cv, p_gd,
            s_shift, s_wkv, s_mc, s_mn, s_mm, s_hg, s_cv, s_gd)
```

```python
import functools
import math

import jax
import jax.numpy as jnp
from jax import lax
from jax.experimental import pallas as pl
from jax.experimental.pallas import tpu as pltpu

F32 = jnp.float32
BF16 = jnp.bfloat16
HI = lax.Precision.HIGHEST

D_MODEL = 2048
DEPTH = 4
GW = 512
RWKV_HD = 64
RWKV_H = 8
RWKV_PROJ = 1792
RWKV_LN_EPS = 64e-5
ML_H = 4
HG_H = 4
GDN_H = 4
HD = 128
CONV_W = 4
N_GROUPS = 4
EPG = 8
N_EXPERTS = 32
D_FF = 256
NORM_EPS = 1e-6

PROJ_PAD = 8192
GATES_OFF = 1792
GATES_W = 256
COL_ML_I, COL_ML_F, COL_GDN_B, COL_GDN_A = 0, 4, 8, 12

CHUNK = 64
SUB = 16
VMEM_LIMIT = 56 * 1024 * 1024


def _cparams(sem):
    return pltpu.CompilerParams(dimension_semantics=sem, vmem_limit_bytes=VMEM_LIMIT)


def _mm(a, b):
    return jnp.dot(a.astype(BF16), b.astype(BF16), preferred_element_type=F32)


def _mm_nt(a, b):
    return lax.dot_general(a.astype(BF16), b.astype(BF16), (((1,), (1,)), ((), ())),
                           preferred_element_type=F32)


def _mm_tn(a, b):
    return lax.dot_general(a.astype(BF16), b.astype(BF16), (((0,), (0,)), ((), ())),
                           preferred_element_type=F32)


def _mm_hi(a, b):
    return jnp.dot(a, b, precision=HI, preferred_element_type=F32)


def _mm_nt_hi(a, b):
    return lax.dot_general(a, b, (((1,), (1,)), ((), ())), precision=HI, preferred_element_type=F32)


def _sigmoid(x):
    return 1.0 / (1.0 + jnp.exp(-x))


def _silu(x):
    return x * _sigmoid(x)


def _softplus(x):
    return jnp.maximum(x, 0.0) + jnp.log(1.0 + jnp.exp(-jnp.abs(x)))


def _iota(shape, dim):
    return lax.broadcasted_iota(jnp.int32, shape, dim)


def _eye(n, m):
    return (_iota((n, m), 0) == _iota((n, m), 1)).astype(F32)


def _lower(n, inclusive):
    r, c = _iota((n, n), 0), _iota((n, n), 1)
    return (c <= r) if inclusive else (c < r)


def _rms_rows(x):
    return x * lax.rsqrt(jnp.mean(x * x, axis=-1, keepdims=True) + NORM_EPS)


def _tri_inv_minus_eye(n_mat, size):
    m = -n_mat
    pw = n_mat
    steps = int(math.log2(size))
    for i in range(1, steps):
        pw = _mm(pw, pw)
        m = m + pw + _mm(m, pw)
    return m


def _ada_kernel(c_ref, w_ref, b_ref, o_ref):
    c = c_ref[...]
    o_ref[...] = _mm(_silu(c), w_ref[...]) + b_ref[...]


def _ada(c_all, ada_w, ada_b):
    depth, d, n6 = ada_w.shape
    rows = c_all.shape[0]
    tn = 1024
    return pl.pallas_call(
        _ada_kernel,
        grid=(depth, n6 // tn),
        in_specs=[pl.BlockSpec((rows, d), lambda l, j: (0, 0)),
                  pl.BlockSpec((None, d, tn), lambda l, j: (l, 0, j)),
                  pl.BlockSpec((None, 1, tn), lambda l, j: (l, 0, j))],
        out_specs=pl.BlockSpec((None, rows, tn), lambda l, j: (l, 0, j)),
        out_shape=jax.ShapeDtypeStruct((depth, rows, n6), F32),
        compiler_params=_cparams(("parallel", "parallel")),
        name="ada",
    )(c_all, ada_w, ada_b.reshape(depth, 1, n6))


def _modulated_norm(x, nw, sc, sh):
    return _rms_rows(x) * nw * (1.0 + sc) + sh


def _inproj_kernel(*refs, combine):
    if combine:
        x_ref, ff_ref, gt_ref, sh_ref, sc_ref, nw_ref, w_ref, xo_ref, p_ref, h_ref = refs
    else:
        x_ref, sh_ref, sc_ref, nw_ref, w_ref, p_ref, h_ref = refs

    @pl.when(pl.program_id(1) == 0)
    def _():
        x = x_ref[...]
        if combine:
            x = x + gt_ref[...] * (ff_ref[0] + ff_ref[1])
            xo_ref[...] = x
        h_ref[...] = _modulated_norm(x, nw_ref[...], sc_ref[...], sh_ref[...]).astype(BF16)

    p_ref[...] = jnp.dot(h_ref[...], w_ref[...], preferred_element_type=F32)


def _mod_spec(mod3, which, tm, rows_per_group):
    r = mod3.shape[1]
    return pl.BlockSpec((None, r, D_MODEL),
                        lambda i, *_: ((i * tm) // rows_per_group, 0, which))


def _in_proj(x, mod3, rows_per_group, nw, w, ff=None, gate_mod3=None):
    n = x.shape[0]
    tm = min(512, rows_per_group)
    tn = 1024
    combine = ff is not None
    row = lambda i, j: (i, 0)
    in_specs = [pl.BlockSpec((tm, D_MODEL), row)]
    args = [x]
    if combine:
        in_specs += [pl.BlockSpec((2, tm, D_MODEL), lambda i, j: (0, i, 0)),
                     _mod_spec(gate_mod3, 5, tm, rows_per_group)]
        args += [ff, gate_mod3]
    in_specs += [_mod_spec(mod3, 0, tm, rows_per_group), _mod_spec(mod3, 1, tm, rows_per_group),
                 pl.BlockSpec((1, D_MODEL), lambda i, j: (0, 0)),
                 pl.BlockSpec((D_MODEL, tn), lambda i, j: (0, j))]
    args += [mod3, mod3, nw.reshape(1, D_MODEL), w]
    out_specs = [pl.BlockSpec((tm, tn), lambda i, j: (i, j))]
    out_shape = [jax.ShapeDtypeStruct((n, PROJ_PAD), F32)]
    if combine:
        out_specs = [pl.BlockSpec((tm, D_MODEL), row)] + out_specs
        out_shape = [jax.ShapeDtypeStruct((n, D_MODEL), F32)] + out_shape
    res = pl.pallas_call(
        functools.partial(_inproj_kernel, combine=combine),
        grid=(n // tm, PROJ_PAD // tn),
        in_specs=in_specs, out_specs=out_specs, out_shape=out_shape,
        scratch_shapes=[pltpu.VMEM((tm, D_MODEL), BF16)],
        compiler_params=_cparams(("parallel", "arbitrary")),
        name="in_proj",
    )(*args)
    return (res[0], res[1]) if combine else (x, res[0])


def _lane_max(x):
    return jnp.max(x, axis=-1, keepdims=True)


def _lane_sum(x):
    return jnp.sum(x, axis=-1, keepdims=True)


def _first_lane_of(mask, lane):
    return jnp.min(jnp.where(mask, lane, 4096), axis=-1, keepdims=True)


def _route(logits, biased):
    lane = _iota(logits.shape, 1)
    neg = jnp.float32(-jnp.inf)
    is_g = lane < N_GROUPS
    gmax = _lane_max(jnp.where(is_g, logits, neg))
    gexp = jnp.where(is_g, jnp.exp(logits - gmax), 0.0)
    gb = jnp.where(is_g, biased, neg)
    gsel = _first_lane_of(gb == _lane_max(gb), lane)
    pg = _lane_sum(jnp.where(lane == gsel, gexp, 0.0)) / _lane_sum(gexp)
    lo = N_GROUPS + gsel * EPG
    eb = jnp.where((lane >= lo) & (lane < lo + EPG), biased, neg)
    i1 = _first_lane_of(eb == _lane_max(eb), lane)
    eb2 = jnp.where(lane == i1, neg, eb)
    i2 = _first_lane_of(eb2 == _lane_max(eb2), lane)
    l1 = _lane_sum(jnp.where(lane == i1, logits, 0.0))
    l2 = _lane_sum(jnp.where(lane == i2, logits, 0.0))
    mx = jnp.maximum(l1, l2)
    e1, e2 = jnp.exp(l1 - mx), jnp.exp(l2 - mx)
    cw1, cw2 = pg * e1 / (e1 + e2), pg * e2 / (e1 + e2)
    id1, id2 = (i1 - N_GROUPS).astype(F32), (i2 - N_GROUPS).astype(F32)
    return jnp.where(lane == 0, id1, jnp.where(lane == 1, id2,
                     jnp.where(lane == 2, cw1, jnp.where(lane == 3, cw2, 0.0))))


def _outproj_kernel(yr_ref, ym_ref, yh_ref, yg_ref, x_ref, gt_ref, sh_ref, sc_ref, nw_ref,
                    wo_ref, wr_ref, br_ref, xo_ref, h2_ref, rt_ref):
    mix = jnp.dot(yr_ref[...], wo_ref[0:GW, :], preferred_element_type=F32)
    mix += jnp.dot(ym_ref[...], wo_ref[GW:2 * GW, :], preferred_element_type=F32)
    mix += jnp.dot(yh_ref[...], wo_ref[2 * GW:3 * GW, :], preferred_element_type=F32)
    mix += jnp.dot(yg_ref[...], wo_ref[3 * GW:4 * GW, :], preferred_element_type=F32)
    x = x_ref[...] + gt_ref[...] * mix
    xo_ref[...] = x
    h2 = _modulated_norm(x, nw_ref[...], sc_ref[...], sh_ref[...])
    h2_ref[...] = h2
    logits = _mm_hi(h2, wr_ref[...])
    rt_ref[...] = _route(logits, logits + br_ref[...])


def _out_proj(ys, x, mod3, rows_per_group, nw, wo, w_route, b_route):
    n = x.shape[0]
    tm = min(256, rows_per_group)
    row = lambda i: (i, 0)
    const = lambda i: (0, 0)
    in_specs = [pl.BlockSpec((tm, GW), row)] * 4 + [
        pl.BlockSpec((tm, D_MODEL), row),
        _mod_spec(mod3, 2, tm, rows_per_group), _mod_spec(mod3, 3, tm, rows_per_group),
        _mod_spec(mod3, 4, tm, rows_per_group),
        pl.BlockSpec((1, D_MODEL), const), pl.BlockSpec((D_MODEL, D_MODEL), const),
        pl.BlockSpec((D_MODEL, 128), const), pl.BlockSpec((1, 128), const)]
    return pl.pallas_call(
        _outproj_kernel,
        grid=(n // tm,),
        in_specs=in_specs,
        out_specs=[pl.BlockSpec((tm, D_MODEL), row), pl.BlockSpec((tm, D_MODEL), row),
                   pl.BlockSpec((tm, 128), row)],
        out_shape=[jax.ShapeDtypeStruct((n, D_MODEL), F32), jax.ShapeDtypeStruct((n, D_MODEL), F32),
                   jax.ShapeDtypeStruct((n, 128), F32)],
        compiler_params=_cparams(("parallel",)),
        name="out_proj",
    )(*ys, x, mod3, mod3, mod3, nw.reshape(1, D_MODEL), wo, w_route, b_route)


def _moe_kernel(te_ref, nv_ref, tok_ref, dst_ref, h2_hbm, cw_ref, wg_ref, wu_ref, wd_ref, out_hbm,
                hbuf, obuf, wgb, wub, wdb, gsem, ssem, *, tm):
    t = pl.program_id(0)
    nv = nv_ref[t]

    @pl.when(nv > 0)
    def _():
        base = t * tm

        @pl.when(nv < tm)
        def _():
            hbuf[...] = jnp.zeros_like(hbuf)

        def gather(r, carry):
            tok = tok_ref[base + r]
            pltpu.make_async_copy(h2_hbm.at[pl.ds(tok, 1)], hbuf.at[pl.ds(r, 1)], gsem).start()
            return carry

        nv8 = pl.multiple_of(((nv + 7) // 8) * 8, 8)
        lax.fori_loop(0, nv8, gather, 0)

        prev = te_ref[jnp.maximum(t - 1, 0)]

        @pl.when((t == 0) | (te_ref[t] != prev))
        def _():
            wgb[...] = wg_ref[...].astype(BF16)
            wub[...] = wu_ref[...].astype(BF16)
            wdb[...] = wd_ref[...].astype(BF16)

        pltpu.make_async_copy(h2_hbm.at[pl.ds(0, nv8)], hbuf.at[pl.ds(0, nv8)], gsem).wait()
        h = hbuf[...].astype(BF16)
        g = jnp.dot(h, wgb[...], preferred_element_type=F32)
        u = jnp.dot(h, wub[...], preferred_element_type=F32)
        hid = (_silu(g) * u * cw_ref[...]).astype(BF16)
        obuf[...] = jnp.dot(hid, wdb[...], preferred_element_type=F32)

        def scatter(r, carry):
            dst = dst_ref[base + r]
            pltpu.make_async_copy(obuf.at[pl.ds(r, 1)], out_hbm.at[pl.ds(dst, 1)], ssem).start()
            return carry

        lax.fori_loop(0, nv, scatter, 0)

        def scatter_done(r, carry):
            pltpu.make_async_copy(obuf.at[pl.ds(0, 1)], out_hbm.at[pl.ds(0, 1)], ssem).wait()
            return carry

        lax.fori_loop(0, nv, scatter_done, 0)


def _moe(h2, route, w_gate, w_up, w_down, layer, tm):
    n = h2.shape[0]
    slots = 2 * n
    n_tiles = -(-(slots + N_EXPERTS * (tm - 1)) // tm)
    eid = jnp.concatenate([route[:, 0], route[:, 1]]).astype(jnp.int32)
    cw = jnp.concatenate([route[:, 2], route[:, 3]])
    order = jnp.argsort(eid, stable=True).astype(jnp.int32)
    e_sorted = eid[order]
    counts = jnp.zeros((N_EXPERTS,), jnp.int32).at[eid].add(1)
    tiles_per = (counts + tm - 1) // tm
    tile_start = jnp.cumsum(tiles_per) - tiles_per
    slot_start = jnp.cumsum(counts) - counts
    pos = tile_start[e_sorted] * tm + (jnp.arange(slots, dtype=jnp.int32) - slot_start[e_sorted])
    row_slot = jnp.zeros((n_tiles * tm,), jnp.int32).at[pos].set(order)
    row_cw = jnp.zeros((n_tiles * tm,), F32).at[pos].set(cw[order])
    row_tok = row_slot % n
    tile_ids = jnp.arange(n_tiles, dtype=jnp.int32)
    used = jnp.sum(tiles_per)
    tile_e = jnp.searchsorted(jnp.cumsum(tiles_per), tile_ids, side="right").astype(jnp.int32)
    tile_e = jnp.minimum(tile_e, N_EXPERTS - 1)
    rem = counts[tile_e] - (tile_ids - tile_start[tile_e]) * tm
    tile_nv = jnp.where(tile_ids < used, jnp.clip(rem, 0, tm), 0).astype(jnp.int32)

    wspec = lambda shape: pl.BlockSpec((None, None) + shape, lambda t, te, *_: (layer, te[t], 0, 0))
    grid_spec = pltpu.PrefetchScalarGridSpec(
        num_scalar_prefetch=4,
        grid=(n_tiles,),
        in_specs=[pl.BlockSpec(memory_space=pl.ANY),
                  pl.BlockSpec((tm, 1), lambda t, *_: (t, 0)),
                  wspec((D_MODEL, D_FF)), wspec((D_MODEL, D_FF)), wspec((D_FF, D_MODEL))],
        out_specs=pl.BlockSpec(memory_space=pl.ANY),
        scratch_shapes=[pltpu.VMEM((tm, D_MODEL), F32), pltpu.VMEM((tm, D_MODEL), F32),
                        pltpu.VMEM((D_MODEL, D_FF), BF16), pltpu.VMEM((D_MODEL, D_FF), BF16),
                        pltpu.VMEM((D_FF, D_MODEL), BF16),
                        pltpu.SemaphoreType.DMA, pltpu.SemaphoreType.DMA])
    out = pl.pallas_call(
        functools.partial(_moe_kernel, tm=tm),
        grid_spec=grid_spec,
        out_shape=jax.ShapeDtypeStruct((slots, D_MODEL), F32),
        compiler_params=_cparams(("arbitrary",)),
        name="moe",
    )(tile_e, tile_nv, row_tok, row_slot, h2, row_cw.reshape(-1, 1), w_gate, w_up, w_down)
    return out.reshape(2, n, D_MODEL)


def _final_kernel(x_ref, ff_ref, gt_ref, nw_ref, o_ref):
    x = x_ref[...] + gt_ref[...] * (ff_ref[0] + ff_ref[1])
    o_ref[...] = _rms_rows(x) * nw_ref[...]


def _final(x, ff, mod3, rows_per_group, nw):
    n = x.shape[0]
    tm = min(512, rows_per_group)
    return pl.pallas_call(
        _final_kernel,
        grid=(n // tm,),
        in_specs=[pl.BlockSpec((tm, D_MODEL), lambda i: (i, 0)),
                  pl.BlockSpec((2, tm, D_MODEL), lambda i: (0, i, 0)),
                  _mod_spec(mod3, 5, tm, rows_per_group),
                  pl.BlockSpec((1, D_MODEL), lambda i: (0, 0))],
        out_specs=pl.BlockSpec((tm, D_MODEL), lambda i: (i, 0)),
        out_shape=jax.ShapeDtypeStruct((n, D_MODEL), F32),
        compiler_params=_cparams(("parallel",)),
        name="final_norm",
    )(x, ff, mod3, nw.reshape(1, D_MODEL))


def _seg_sum(x, width):
    seg = (_iota((128, 128), 0) // width == _iota((128, 128), 1) // width).astype(F32)
    parts = [_mm_hi(x[:, j:j + 128], seg) for j in range(0, x.shape[1], 128)]
    return parts[0] if len(parts) == 1 else jnp.concatenate(parts, axis=1)


def _head_norm(o, nw, gate):
    return o * lax.rsqrt(_seg_sum(o * o, HD) * (1.0 / HD) + NORM_EPS) * nw * gate


def _carry_rows(ext_ref, cur, first, L):
    @pl.when(first)
    def _():
        ext_ref[0:8, :] = jnp.zeros((8, ext_ref.shape[1]), F32)

    ext_ref[8:8 + L, :] = cur


def _rwkv_kernel(p_ref, mu_ref, vec_ref, wup_ref, aup_ref, gup_ref, y_ref, sh_ref, s_ref, ext_ref, *, L):
    c = pl.program_id(1)
    p = p_ref[...]
    _carry_rows(ext_ref, p, c == 0, L)
    prev = ext_ref[7:7 + L, :]
    xs = p + (prev - p) * mu_ref[...]
    ext_ref[0:8, :] = ext_ref[L:L + 8, :]
    sh_ref[...] = p[L - 1:L, :]

    @pl.when(c == 0)
    def _():
        s_ref[...] = jnp.zeros_like(s_ref)

    w0, a0, k_k, k_a, r_k, ln_w, ln_b = [vec_ref[i:i + 1, :] for i in range(7)]
    r, k, v = xs[:, 0:GW], xs[:, GW:2 * GW], xs[:, 2 * GW:3 * GW]
    dw, da, dg = xs[:, 1536:1600], xs[:, 1600:1664], xs[:, 1664:1792]
    logw = -_softplus(-(w0 + _mm(jnp.tanh(dw), wup_ref[...]))) - 0.5
    lw = -jnp.exp(logw)
    a = _sigmoid(a0 + _mm(da, aup_ref[...]))
    g = _mm(_sigmoid(dg), gup_ref[...])
    kk = k * k_k
    kk = kk * lax.rsqrt(jnp.maximum(_seg_sum(kk * kk, RWKV_HD), 1e-12))
    k2 = k * (1.0 + (a - 1.0) * k_a)
    cl = _mm_hi(_lower(L, True).astype(F32), lw)
    cl_last = cl[L - 1:L, :]
    at = -kk * jnp.exp(cl - lw)
    bt = kk * a * jnp.exp(-cl)
    kt = k2 * jnp.exp(-cl)
    rt = r * jnp.exp(cl)
    b_end = kk * a * jnp.exp(cl_last - cl)
    k_end = k2 * jnp.exp(cl_last - cl)
    strict, incl = _lower(L, False), _lower(L, True)
    ys = []
    for h in range(RWKV_H):
        sl = slice(h * RWKV_HD, (h + 1) * RWKV_HD)
        s0 = s_ref[h]
        ath, bth, kth, rth, vh = at[:, sl], bt[:, sl], kt[:, sl], rt[:, sl], v[:, sl]
        n_ab = jnp.where(strict, _mm_nt(ath, bth), 0.0)
        n_ak = jnp.where(strict, _mm_nt(ath, kth), 0.0)
        m_rb = jnp.where(incl, _mm_nt(rth, bth), 0.0)
        m_rk = jnp.where(incl, _mm_nt(rth, kth), 0.0)
        m = _tri_inv_minus_eye(-n_ab, L)
        rhs = _mm_nt(ath, s0) + _mm(n_ak, vh)
        u = rhs + _mm(m, rhs)
        ys.append(_mm_nt(rth, s0) + _mm(m_rb, u) + _mm(m_rk, vh))
        s_ref[h] = s0 * jnp.exp(cl_last[:, sl]) + _mm_tn(u, b_end[:, sl]) + _mm_tn(vh, k_end[:, sl])
    y = jnp.concatenate(ys, axis=1)
    mean = _seg_sum(y, RWKV_HD) * (1.0 / RWKV_HD)
    dev = y - mean
    var = _seg_sum(dev * dev, RWKV_HD) * (1.0 / RWKV_HD)
    y = dev * lax.rsqrt(var + RWKV_LN_EPS) * ln_w + ln_b
    bonus = _seg_sum(r * k2 * r_k, RWKV_HD) * v
    y_ref[...] = ((y + bonus) * g).astype(y_ref.dtype)


def _gate_forms(g_ref, gp_ref, L):
    raw = g_ref[:, 0:128] + gp_ref[0:1, :]
    lane = _iota(raw.shape, 1)
    lf = jnp.minimum(raw, 0.0) - jnp.log(1.0 + jnp.exp(-jnp.abs(raw)))
    beta = _sigmoid(raw)
    decay = -jnp.exp(gp_ref[1:2, :]) * _softplus(raw)
    cols = jnp.where(lane < 4, raw, jnp.where(lane < 8, lf, jnp.where(lane < 12, beta, decay)))
    rows = _mm_nt_hi(_eye(16, 128), cols)
    ccols = _mm_hi(_lower(L, True).astype(F32), cols)
    upper = (_iota((L, L), 0) <= _iota((L, L), 1)).astype(F32)
    crows = _mm_hi(rows, upper)
    return cols, rows, ccols, crows


def _mlstm_kernel(p_ref, g_ref, gp_ref, nw_ref, y_ref, c_ref, n_ref, m_ref, *, L):
    c = pl.program_id(1)

    @pl.when(c == 0)
    def _():
        c_ref[...] = jnp.zeros_like(c_ref)
        n_ref[...] = jnp.zeros_like(n_ref)
        m_ref[...] = jnp.zeros_like(m_ref)

    cols, rows, ccols, crows = _gate_forms(g_ref, gp_ref, L)
    causal = _lower(L, True)
    neg = jnp.float32(-jnp.inf)
    ys = []
    for h in range(ML_H):
        sl = slice(h * HD, (h + 1) * HD)
        q = p_ref[:, h * HD:(h + 1) * HD]
        k = p_ref[:, GW + h * HD:GW + (h + 1) * HD] * (HD ** -0.5)
        v = p_ref[:, 2 * GW + h * HD:2 * GW + (h + 1) * HD]
        ig_col, ig_row = cols[:, h:h + 1], rows[h:h + 1, :]
        b_col, b_row = ccols[:, 4 + h:5 + h], crows[4 + h:5 + h, :]
        m_prev = m_ref[:, h:h + 1]
        cmat, nvec = c_ref[h], n_ref[h:h + 1, :]
        dmat = jnp.where(causal, b_col - b_row + ig_row, neg)
        gcol = b_col + m_prev
        mt = jnp.maximum(gcol, _lane_max(dmat))
        smat = _mm_nt(q, k) * jnp.exp(dmat - mt)
        wg = jnp.exp(gcol - mt)
        num = _mm(smat, v) + wg * _mm_nt(q, cmat)
        den = _lane_sum(smat) + wg * _lane_sum(q * nvec)
        hout = num / jnp.maximum(jnp.abs(den), jnp.exp(-mt))
        m_last = mt[L - 1:L, :]
        wk = jnp.exp(b_col[L - 1:L, :] - b_col + ig_col - m_last)
        decay = jnp.exp(gcol[L - 1:L, :] - m_last)
        c_ref[h] = decay * cmat + _mm_tn(v * wk, k)
        n_ref[h:h + 1, :] = decay * nvec + jnp.sum(k * wk, axis=0, keepdims=True)
        m_ref[:, h:h + 1] = m_last
        ys.append(hout)
    o = p_ref[:, 3 * GW:4 * GW]
    y_ref[...] = _head_norm(jnp.concatenate(ys, axis=1), nw_ref[...], _sigmoid(o)).astype(y_ref.dtype)


def _hgrn_kernel(p_ref, lb_ref, nw_ref, y_ref, s_ref, *, L):
    c = pl.program_id(1)

    @pl.when(c == 0)
    def _():
        s_ref[...] = jnp.zeros_like(s_ref)

    lb = lb_ref[...]
    qa = _silu(p_ref[:, 0:GW])
    fg = lb + (1.0 - lb) * _sigmoid(p_ref[:, GW:2 * GW])
    ka = 1.0 - fg
    va = p_ref[:, 2 * GW:3 * GW]
    cga = _mm_hi(_lower(L, True).astype(F32), jnp.log(fg))
    ones = jnp.ones((HD, HD), BF16)
    t3, s3 = _iota((SUB, SUB, HD), 0), _iota((SUB, SUB, HD), 1)
    neg = jnp.float32(-jnp.inf)
    ys = []
    for h in range(HG_H):
        sl = slice(h * HD, (h + 1) * HD)
        q, k, v, cg = qa[:, sl], ka[:, sl], va[:, sl], cga[:, sl]
        st = s_ref[h]
        inter = _mm_nt(q * jnp.exp(cg), st)
        rows_out = []
        for i in range(L // SUB):
            lo = i * SUB
            qi, ki, vi, cgi = q[lo:lo + SUB], k[lo:lo + SUB], v[lo:lo + SUB], cg[lo:lo + SUB]
            e3 = jnp.exp(jnp.where(s3 <= t3, cgi[:, None, :] - cgi[None, :, :], neg))
            x3 = qi[:, None, :] * ki[None, :, :] * e3
            a3 = jnp.dot(x3.reshape(SUB * SUB, HD).astype(BF16), ones,
                         preferred_element_type=F32).reshape(SUB, SUB, HD)
            oi = inter[lo:lo + SUB] + jnp.sum(a3 * vi[None, :, :], axis=1)
            if i > 0:
                ref = cg[lo - 1:lo, :]
                a_off = _mm_nt(qi * jnp.exp(cgi - ref), k[0:lo] * jnp.exp(ref - cg[0:lo]))
                oi = oi + _mm(a_off, v[0:lo])
            rows_out.append(oi)
        ys.append(jnp.concatenate(rows_out, axis=0))
        cg_last = cg[L - 1:L, :]
        s_ref[h] = st * jnp.exp(cg_last) + _mm_tn(v, k * jnp.exp(cg_last - cg))
    g = p_ref[:, 3 * GW:4 * GW]
    y_ref[...] = _head_norm(jnp.concatenate(ys, axis=1), nw_ref[...], _silu(g)).astype(y_ref.dtype)


def _gdn_kernel(p_ref, g_ref, gp_ref, cw_ref, nw_ref, y_ref, cv_ref, s_ref, ext_ref, *, L):
    c = pl.program_id(1)
    w3 = 3 * GW
    _carry_rows(ext_ref, p_ref[:, 0:w3], c == 0, L)
    conv = (cw_ref[3:4, :] * ext_ref[8:8 + L, :] + cw_ref[2:3, :] * ext_ref[7:7 + L, :]
            + cw_ref[1:2, :] * ext_ref[6:6 + L, :] + cw_ref[0:1, :] * ext_ref[5:5 + L, :])
    cv_ref[...] = ext_ref[L + 5:L + 8, :]
    ext_ref[0:8, :] = ext_ref[L:L + 8, :]

    @pl.when(c == 0)
    def _():
        s_ref[...] = jnp.zeros_like(s_ref)

    qkv = _silu(conv)
    cols, rows, ccols, crows = _gate_forms(g_ref, gp_ref, L)
    strict, causal = _lower(L, False), _lower(L, True)
    neg = jnp.float32(-jnp.inf)
    ys = []
    for h in range(GDN_H):
        q = qkv[:, h * HD:(h + 1) * HD]
        k = qkv[:, GW + h * HD:GW + (h + 1) * HD]
        v = qkv[:, 2 * GW + h * HD:2 * GW + (h + 1) * HD]
        q = q * lax.rsqrt(jnp.maximum(_lane_sum(q * q), 1e-12)) * (HD ** -0.5)
        k = k * lax.rsqrt(jnp.maximum(_lane_sum(k * k), 1e-12))
        beta = cols[:, COL_GDN_B + h:COL_GDN_B + h + 1]
        cg_col = ccols[:, COL_GDN_A + h:COL_GDN_A + h + 1]
        cg_row = crows[COL_GDN_A + h:COL_GDN_A + h + 1, :]
        diff = cg_col - cg_row
        s0 = s_ref[h]
        kb = k * beta
        n_mat = _mm_nt(kb, k) * jnp.exp(jnp.where(strict, diff, neg))
        m = _tri_inv_minus_eye(n_mat, L)
        vb = v * beta
        u = vb + _mm(m, vb)
        kw = kb * jnp.exp(cg_col)
        w = kw + _mm(m, kw)
        qk = _mm_nt(q, k) * jnp.exp(jnp.where(causal, diff, neg))
        u2 = u - _mm(w, s0)
        ys.append(_mm(q * jnp.exp(cg_col), s0) + _mm(qk, u2))
        cg_last = cg_col[L - 1:L, :]
        s_ref[h] = jnp.exp(cg_last) * s0 + _mm_tn(k * jnp.exp(cg_last - cg_col), u2)
    g = p_ref[:, w3:w3 + GW]
    y_ref[...] = _head_norm(jnp.concatenate(ys, axis=1), nw_ref[...], _silu(g)).astype(y_ref.dtype)


def _prompt_mixers(p3, lp):
    b, t, _ = p3.shape
    L = math.gcd(t, CHUNK)
    grid = (b, t // L)
    cp = _cparams(("parallel", "arbitrary"))
    col = lambda width, idx: pl.BlockSpec((None, L, width), lambda i, c: (i, c, idx))
    const2 = lambda shape: pl.BlockSpec(shape, lambda i, c: (0, 0))
    y_spec = pl.BlockSpec((None, L, GW), lambda i, c: (i, c, 0))
    y_shape = jax.ShapeDtypeStruct((b, t, GW), BF16)
    state = lambda *s: (pl.BlockSpec((None,) + s, lambda i, c: (i,) + (0,) * len(s)),
                        jax.ShapeDtypeStruct((b,) + s, F32))
    gates = col(GATES_W, GATES_OFF // GATES_W)

    specs, shapes = zip(state(1, RWKV_PROJ), state(RWKV_H, RWKV_HD, RWKV_HD))
    yr, n_sh, n_wkv = pl.pallas_call(
        functools.partial(_rwkv_kernel, L=L), grid=grid,
        in_specs=[col(RWKV_PROJ, 0), const2((1, RWKV_PROJ)), const2((8, GW)), const2((64, GW)),
                  const2((64, GW)), const2((128, GW))],
        out_specs=[y_spec, *specs], out_shape=[y_shape, *shapes],
        scratch_shapes=[pltpu.VMEM((L + 8, RWKV_PROJ), F32)], compiler_params=cp, name="rwkv7",
    )(p3, lp["rwkv_mu"], lp["rwkv_vec"], lp["rwkv_w_up"], lp["rwkv_a_up"], lp["rwkv_g_up"])

    specs, shapes = zip(state(ML_H, HD, HD), state(ML_H, HD), state(1, 128))
    ym, n_c, n_n, n_m = pl.pallas_call(
        functools.partial(_mlstm_kernel, L=L), grid=grid,
        in_specs=[col(4 * GW, 1), gates, const2((8, 128)), const2((1, GW))],
        out_specs=[y_spec, *specs], out_shape=[y_shape, *shapes], compiler_params=cp, name="mlstm",
    )(p3, p3, lp["gate_par"], lp["ml_norm"])

    specs, shapes = zip(state(HG_H, HD, HD))
    yh, n_hg = pl.pallas_call(
        functools.partial(_hgrn_kernel, L=L), grid=grid,
        in_specs=[col(4 * GW, 2), const2((1, GW)), const2((1, GW))],
        out_specs=[y_spec, *specs], out_shape=[y_shape, *shapes], compiler_params=cp, name="hgrn2",
    )(p3, lp["hg_lb"], lp["hg_norm"])

    specs, shapes = zip(state(CONV_W - 1, 3 * GW), state(GDN_H, HD, HD))
    yg, n_cv, n_gd = pl.pallas_call(
        functools.partial(_gdn_kernel, L=L), grid=grid,
        in_specs=[col(4 * GW, 3), gates, const2((8, 128)), const2((CONV_W, 3 * GW)), const2((1, GW))],
        out_specs=[y_spec, *specs], out_shape=[y_shape, *shapes],
        scratch_shapes=[pltpu.VMEM((L + 8, 3 * GW), F32)], compiler_params=cp, name="gdn",
    )(p3, p3, lp["gate_par"], lp["gdn_conv_w"], lp["gdn_norm"])

    ys = [y.reshape(b * t, GW) for y in (yr, ym, yh, yg)]
    states = (n_sh, n_wkv, n_c, n_n, n_m[:, 0, :ML_H], jnp.swapaxes(n_hg, -1, -2), n_cv, n_gd)
    return ys, states


DEC_BS = 8


def _col_to_row(col):
    n = col.shape[0]
    return jnp.sum(jnp.where(_iota((n, n), 0) == _iota((n, n), 1), col, 0.0), axis=0, keepdims=True)


def _sub_sum(x):
    return jnp.sum(x, axis=0, keepdims=True)


def _decode_kernel(p_ref, sh_ref, wkv_ref, mc_ref, mn_ref, mm_ref, hg_ref, cv_ref, gd_ref,
                   mu_ref, vec_ref, wup_ref, aup_ref, gup_ref, gp_ref, mlw_ref, lb_ref, hgw_ref,
                   cw_ref, gdw_ref,
                   yr_ref, ym_ref, yh_ref, yg_ref,
                   nsh_ref, nwkv_ref, nmc_ref, nmn_ref, nmm_ref, nhg_ref, ncv_ref, ngd_ref):
    bs = DEC_BS
    w3 = 3 * GW
    pr = p_ref[:, 0:RWKV_PROJ]
    xs = pr + (sh_ref[...] - pr) * mu_ref[...]
    nsh_ref[...] = pr
    w0, a0, k_k, k_a, r_k, ln_w, ln_b = [vec_ref[i:i + 1, :] for i in range(7)]
    r, k, v = xs[:, 0:GW], xs[:, GW:2 * GW], xs[:, 2 * GW:3 * GW]
    dw, da, dg = xs[:, 1536:1600], xs[:, 1600:1664], xs[:, 1664:1792]
    logw = -_softplus(-(w0 + _mm(jnp.tanh(dw), wup_ref[...]))) - 0.5
    wdec = jnp.exp(-jnp.exp(logw))
    a = _sigmoid(a0 + _mm(da, aup_ref[...]))
    g_r = _mm(_sigmoid(dg), gup_ref[...])
    kk = k * k_k
    kk = kk * lax.rsqrt(jnp.maximum(_seg_sum(kk * kk, RWKV_HD), 1e-12))
    k2 = k * (1.0 + (a - 1.0) * k_a)
    ra, rb = -kk, kk * a

    gates = p_ref[:, GATES_OFF:GATES_OFF + 128] + gp_ref[0:1, :]
    ml_off = GATES_OFF + GATES_W
    mq = p_ref[:, ml_off:ml_off + GW]
    mk = p_ref[:, ml_off + GW:ml_off + 2 * GW] * (HD ** -0.5)
    mv = p_ref[:, ml_off + 2 * GW:ml_off + 3 * GW]
    mo = p_ref[:, ml_off + 3 * GW:ml_off + 4 * GW]
    m_ig = gates
    m_lf = jnp.minimum(gates, 0.0) - jnp.log(1.0 + jnp.exp(-jnp.abs(gates)))

    hg_off = ml_off + 4 * GW
    lb = lb_ref[...]
    hq = _silu(p_ref[:, hg_off:hg_off + GW])
    hfg = lb + (1.0 - lb) * _sigmoid(p_ref[:, hg_off + GW:hg_off + 2 * GW])
    hv = p_ref[:, hg_off + 2 * GW:hg_off + 3 * GW]
    hgate = p_ref[:, hg_off + 3 * GW:hg_off + 4 * GW]

    gd_off = hg_off + 4 * GW
    cur = p_ref[:, gd_off:gd_off + w3]
    conv = (cw_ref[3:4, :] * cur + cw_ref[2:3, :] * cv_ref[:, 2 * w3:3 * w3]
            + cw_ref[1:2, :] * cv_ref[:, w3:2 * w3] + cw_ref[0:1, :] * cv_ref[:, 0:w3])
    ncv_ref[:, 0:2 * w3] = cv_ref[:, w3:3 * w3]
    ncv_ref[:, 2 * w3:3 * w3] = cur
    qkv = _silu(conv)
    gq, gk, gv = qkv[:, 0:GW], qkv[:, GW:2 * GW], qkv[:, 2 * GW:3 * GW]
    gq = gq * lax.rsqrt(jnp.maximum(_seg_sum(gq * gq, HD), 1e-12)) * (HD ** -0.5)
    gk = gk * lax.rsqrt(jnp.maximum(_seg_sum(gk * gk, HD), 1e-12))
    ggate = p_ref[:, gd_off + w3:gd_off + w3 + GW]
    g_beta = _sigmoid(gates)
    g_dec = jnp.exp(-jnp.exp(gp_ref[1:2, :]) * _softplus(gates))

    blocks = ([mv[:, h * HD:(h + 1) * HD] for h in range(ML_H)]
              + [hq[:, h * HD:(h + 1) * HD] for h in range(HG_H)]
              + [hfg[:, h * HD:(h + 1) * HD] for h in range(HG_H)]
              + [gq[:, h * HD:(h + 1) * HD] for h in range(GDN_H)]
              + [gk[:, h * HD:(h + 1) * HD] for h in range(GDN_H)]
              + [v[:, j * 128:(j + 1) * 128] for j in range(4)])
    xt = _mm_nt_hi(_eye(128, 128), jnp.concatenate(blocks, axis=0))
    colf = lambda blk, s: xt[:, blk * bs + s:blk * bs + s + 1]

    yr_rows, ym_rows, yh_rows, yg_rows = [], [], [], []
    for s in range(bs):
        row = lambda arr, lo, width: arr[s:s + 1, lo:lo + width]
        parts = []
        for h in range(RWKV_H):
            lo = h * RWKV_HD
            st = wkv_ref[s, h]
            v_col = colf(20 + h // 2, s)[(h % 2) * 64:(h % 2) * 64 + 64, :]
            sa = _lane_sum(st * row(ra, lo, 64))
            st = st * row(wdec, lo, 64) + sa * row(rb, lo, 64) + v_col * row(k2, lo, 64)
            nwkv_ref[s, h] = st
            parts.append(_col_to_row(_lane_sum(st * row(r, lo, 64))))
        yr_rows.append(jnp.concatenate(parts, axis=1))
        parts = []
        for h in range(ML_H):
            lo = h * HD
            q_r, k_r = row(mq, lo, HD), row(mk, lo, HD)
            ig, lf = m_ig[s:s + 1, COL_ML_I + h:COL_ML_I + h + 1], m_lf[s:s + 1, COL_ML_F + h:COL_ML_F + h + 1]
            cmat, nvec, m_prev = mc_ref[s, h], mn_ref[s, h:h + 1, :], mm_ref[s:s + 1, h:h + 1]
            gsc = lf + m_prev
            mt = jnp.maximum(gsc, ig)
            wi, wg = jnp.exp(ig - mt), jnp.exp(gsc - mt)
            sc = _lane_sum(q_r * k_r) * wi
            v_col = colf(h, s)
            num = sc * v_col + wg * _lane_sum(cmat * q_r)
            den = sc + wg * _lane_sum(nvec * q_r)
            parts.append(_col_to_row(num / jnp.maximum(jnp.abs(den), jnp.exp(-mt))))
            nmc_ref[s, h] = wg * cmat + (wi * v_col) * k_r
            nmn_ref[s, h:h + 1, :] = wg * nvec + wi * k_r
            nmm_ref[s:s + 1, h:h + 1] = mt
        ym_rows.append(jnp.concatenate(parts, axis=1))
        parts = []
        for h in range(HG_H):
            lo = h * HD
            st = hg_ref[s, h]
            q_col, fg_col = colf(4 + h, s), colf(8 + h, s)
            q_r, fg_r, v_r = row(hq, lo, HD), row(hfg, lo, HD), row(hv, lo, HD)
            parts.append(_sub_sum(st * (q_col * fg_col)) + _lane_sum(q_r * (1.0 - fg_r)) * v_r)
            nhg_ref[s, h] = fg_col * st + (1.0 - fg_col) * v_r
        yh_rows.append(jnp.concatenate(parts, axis=1))
        parts = []
        for h in range(GDN_H):
            lo = h * HD
            st = gd_ref[s, h]
            q_col, k_col = colf(12 + h, s), colf(16 + h, s)
            beta = g_beta[s:s + 1, COL_GDN_B + h:COL_GDN_B + h + 1]
            dec = g_dec[s:s + 1, COL_GDN_A + h:COL_GDN_A + h + 1]
            u2 = beta * row(gv, lo, HD) - _sub_sum(st * (k_col * (beta * dec)))
            qk = _lane_sum(row(gq, lo, HD) * row(gk, lo, HD))
            parts.append(dec * _sub_sum(st * q_col) + qk * u2)
            ngd_ref[s, h] = dec * st + k_col * u2
        yg_rows.append(jnp.concatenate(parts, axis=1))

    yr = jnp.concatenate(yr_rows, axis=0)
    mean = _seg_sum(yr, RWKV_HD) * (1.0 / RWKV_HD)
    dev = yr - mean
    var = _seg_sum(dev * dev, RWKV_HD) * (1.0 / RWKV_HD)
    yr = dev * lax.rsqrt(var + RWKV_LN_EPS) * ln_w + ln_b
    bonus = _seg_sum(r * k2 * r_k, RWKV_HD) * v
    yr_ref[...] = ((yr + bonus) * g_r).astype(yr_ref.dtype)
    ym_ref[...] = _head_norm(jnp.concatenate(ym_rows, axis=0), mlw_ref[...], _sigmoid(mo)).astype(ym_ref.dtype)
    yh_ref[...] = _head_norm(jnp.concatenate(yh_rows, axis=0), hgw_ref[...], _silu(hgate)).astype(yh_ref.dtype)
    yg_ref[...] = _head_norm(jnp.concatenate(yg_rows, axis=0), gdw_ref[...], _silu(ggate)).astype(yg_ref.dtype)


def _sample_mixers(p, states, layer, lp):
    n = p.shape[0]
    bs = DEC_BS
    sh, wkv, mc, mn, mm, hg, cv, gd = states
    sh2 = sh.reshape(DEPTH, n, RWKV_PROJ)
    cv = cv.reshape(DEPTH, n, (CONV_W - 1) * 3 * GW)

    def st_in(arr):
        tail = arr.shape[2:]
        return pl.BlockSpec((None, bs) + tail, lambda i: (layer, i) + (0,) * len(tail))

    def st_out(arr):
        tail = arr.shape[2:]
        return (pl.BlockSpec((bs,) + tail, lambda i: (i,) + (0,) * len(tail)),
                jax.ShapeDtypeStruct((n,) + tail, F32))

    const2 = lambda shape: pl.BlockSpec(shape, lambda i: (0, 0))
    st_arrays = (sh2, wkv, mc, mn, mm, hg, cv, gd)
    o_specs, o_shapes = zip(*[st_out(a) for a in st_arrays])
    y_spec = pl.BlockSpec((bs, GW), lambda i: (i, 0))
    y_shape = jax.ShapeDtypeStruct((n, GW), BF16)
    res = pl.pallas_call(
        _decode_kernel,
        grid=(n // bs,),
        in_specs=[pl.BlockSpec((bs, PROJ_PAD), lambda i: (i, 0))] + [st_in(a) for a in st_arrays]
        + [const2((1, RWKV_PROJ)), const2((8, GW)), const2((64, GW)), const2((64, GW)), const2((128, GW)),
           const2((8, 128)), const2((1, GW)), const2((1, GW)), const2((1, GW)),
           const2((CONV_W, 3 * GW)), const2((1, GW))],
        out_specs=[y_spec] * 4 + list(o_specs),
        out_shape=[y_shape] * 4 + list(o_shapes),
        compiler_params=_cparams(("parallel",)),
        name="decode_mixers",
    )(p, *st_arrays, lp["rwkv_mu"], lp["rwkv_vec"], lp["rwkv_w_up"], lp["rwkv_a_up"], lp["rwkv_g_up"],
      lp["gate_par"], lp["ml_norm"], lp["hg_lb"], lp["hg_norm"], lp["gdn_conv_w"], lp["gdn_norm"])
    ys = list(res[:4])
    n_sh, n_wkv, n_mc, n_mn, n_mm, n_hg, n_cv, n_gd = res[4:]
    return ys, (n_sh.reshape(n, 1, RWKV_PROJ), n_wkv, n_mc, n_mn, n_mm, n_hg,
                n_cv.reshape(n, CONV_W - 1, 3 * GW), n_gd)


def _layer_params(l, a):
    gate_par = jnp.zeros((8, 128), F32)
    gate_par = gate_par.at[0, COL_ML_I:COL_ML_I + 4].set(a["ml_i_bias"][l])
    gate_par = gate_par.at[0, COL_ML_F:COL_ML_F + 4].set(a["ml_f_bias"][l])
    gate_par = gate_par.at[0, COL_GDN_A:COL_GDN_A + 4].set(a["gdn_dt_bias"][l])
    gate_par = gate_par.at[1, COL_GDN_A:COL_GDN_A + 4].set(a["gdn_a_log"][l])
    vec = jnp.stack([a[k][l] for k in ("rwkv_w0", "rwkv_a0", "rwkv_k_k", "rwkv_k_a", "rwkv_r_k",
                                        "rwkv_ln_w", "rwkv_ln_b")] + [jnp.zeros((GW,), F32)])
    return {
        "rwkv_mu": a["rwkv_mu"][l].reshape(1, RWKV_PROJ), "rwkv_vec": vec,
        "rwkv_w_up": a["rwkv_w_up"][l], "rwkv_a_up": a["rwkv_a_up"][l], "rwkv_g_up": a["rwkv_g_up"][l],
        "gate_par": gate_par, "ml_norm": a["ml_norm"][l].reshape(1, GW),
        "hg_lb": a["hg_lbs"][l].reshape(1, GW), "hg_norm": a["hg_norm"][l].reshape(1, GW),
        "gdn_conv_w": a["gdn_conv_w"][l], "gdn_norm": a["gdn_norm"][l].reshape(1, GW),
    }


def _pad_w_in(w_in):
    z = jnp.zeros(w_in.shape[:2] + (GATES_W - 16,), w_in.dtype)
    parts = [w_in[..., 0:1792], w_in[..., 3840:3848], w_in[..., 7944:7952], z,
             w_in[..., 1792:3840], w_in[..., 3848:5896], w_in[..., 5896:7944]]
    return jnp.concatenate(parts, axis=-1).astype(BF16)


def kernel(x_prompt, x_sample, state_rwkv_shift, state_rwkv_wkv, state_mlstm_c, state_mlstm_n,
           state_mlstm_m, state_hgrn, state_gdn_conv, state_gdn, c_prompt, c_sample,
           ada_w, ada_b, norm1, norm2, norm_f, w_in, w_out,
           rwkv_mu, rwkv_w0, rwkv_w_up, rwkv_a0, rwkv_a_up, rwkv_g_up, rwkv_k_k, rwkv_k_a, rwkv_r_k,
           rwkv_ln_w, rwkv_ln_b, ml_i_bias, ml_f_bias, ml_norm, hg_lb, hg_norm,
           gdn_conv_w, gdn_a_log, gdn_dt_bias, gdn_norm,
           moe_w_group, moe_b_group, moe_w_router, moe_b_router, moe_w_gate, moe_w_up, moe_w_down):
    bp, t, _ = x_prompt.shape
    ns = x_sample.shape[0]
    lbs = jax.nn.softmax(hg_lb.astype(F32), axis=0)
    hg_lbs = jnp.cumsum(lbs, axis=0) - lbs[0]
    a = dict(rwkv_mu=rwkv_mu, rwkv_w0=rwkv_w0, rwkv_w_up=rwkv_w_up, rwkv_a0=rwkv_a0, rwkv_a_up=rwkv_a_up,
             rwkv_g_up=rwkv_g_up, rwkv_k_k=rwkv_k_k, rwkv_k_a=rwkv_k_a, rwkv_r_k=rwkv_r_k,
             rwkv_ln_w=rwkv_ln_w, rwkv_ln_b=rwkv_ln_b, ml_i_bias=ml_i_bias, ml_f_bias=ml_f_bias,
             ml_norm=ml_norm, hg_lbs=hg_lbs, hg_norm=hg_norm, gdn_conv_w=gdn_conv_w, gdn_a_log=gdn_a_log,
             gdn_dt_bias=gdn_dt_bias, gdn_norm=gdn_norm)
    w_in_p = _pad_w_in(w_in)
    w_out_b = w_out.astype(BF16)
    w_route = jnp.concatenate([moe_w_group, moe_w_router,
                               jnp.zeros((DEPTH, D_MODEL, 128 - N_GROUPS - N_EXPERTS), F32)], axis=-1)
    b_route = jnp.concatenate([moe_b_group, moe_b_router,
                               jnp.zeros((DEPTH, 128 - N_GROUPS - N_EXPERTS), F32)], axis=-1)

    mod = _ada(jnp.concatenate([c_prompt, c_sample], axis=0), ada_w, ada_b)
    sample_states = (state_rwkv_shift, state_rwkv_wkv, state_mlstm_c, state_mlstm_n, state_mlstm_m,
                     state_hgrn, state_gdn_conv, state_gdn)

    xp = x_prompt.reshape(bp * t, D_MODEL)
    xs = x_sample.reshape(ns, D_MODEL)
    ffp = ffs = None
    modp = mods = None
    new_p, new_s = [], []
    for l in range(DEPTH):
        lp = _layer_params(l, a)
        prev_modp, prev_mods = modp, mods
        modp = mod[l, :bp].reshape(bp, 1, 6 * D_MODEL)
        mods = mod[l, bp:].reshape(1, ns, 6 * D_MODEL)
        wr, br = w_route[l], b_route[l].reshape(1, 128)

        xp, pp = _in_proj(xp, modp, t, norm1[l], w_in_p[l], ffp, prev_modp)
        xs, ps = _in_proj(xs, mods, ns, norm1[l], w_in_p[l], ffs, prev_mods)
        ysp, stp = _prompt_mixers(pp.reshape(bp, t, PROJ_PAD), lp)
        yss, sts = _sample_mixers(ps, sample_states, l, lp)
        new_p.append(stp)
        new_s.append(sts)
        xp, h2p, rtp = _out_proj(ysp, xp, modp, t, norm2[l], w_out_b[l], wr, br)
        xs, h2s, rts = _out_proj(yss, xs, mods, ns, norm2[l], w_out_b[l], wr, br)
        ffp = _moe(h2p, rtp, moe_w_gate, moe_w_up, moe_w_down, l, 256)
        ffs = _moe(h2s, rts, moe_w_gate, moe_w_up, moe_w_down, l, 16)
    yp = _final(xp, ffp, modp, t, norm_f).reshape(bp, t, D_MODEL)
    ys = _final(xs, ffs, mods, ns, norm_f).reshape(ns, 1, D_MODEL)
    stack = lambda lst: tuple(jnp.stack([st[i] for st in lst]) for i in range(8))
    return (yp, ys) + stack(new_p) + stack(new_s)
```

```python
import functools
import math

import jax
import jax.numpy as jnp
from jax import lax
from jax.experimental import pallas as pl
from jax.experimental.pallas import tpu as pltpu

F32 = jnp.float32
BF16 = jnp.bfloat16
HI = lax.Precision.HIGHEST

D_MODEL = 2048
DEPTH = 4
GW = 512
RWKV_HD = 64
RWKV_H = 8
RWKV_PROJ = 1792
RWKV_LN_EPS = 64e-5
ML_H = 4
HG_H = 4
GDN_H = 4
HD = 128
CONV_W = 4
N_GROUPS = 4
EPG = 8
N_EXPERTS = 32
D_FF = 256
NORM_EPS = 1e-6

PROJ_PAD = 8192
GATES_OFF = 1792
GATES_W = 256
COL_ML_I, COL_ML_F, COL_GDN_B, COL_GDN_A = 0, 4, 8, 12

CHUNK = 64
SUB = 16
VMEM_LIMIT = 56 * 1024 * 1024


def _cparams(sem):
    return pltpu.CompilerParams(dimension_semantics=sem, vmem_limit_bytes=VMEM_LIMIT)


def _mm(a, b):
    return jnp.dot(a.astype(BF16), b.astype(BF16), preferred_element_type=F32)


def _mm_nt(a, b):
    return lax.dot_general(a.astype(BF16), b.astype(BF16), (((1,), (1,)), ((), ())),
                           preferred_element_type=F32)


def _mm_tn(a, b):
    return lax.dot_general(a.astype(BF16), b.astype(BF16), (((0,), (0,)), ((), ())),
                           preferred_element_type=F32)


def _mm_hi(a, b):
    return jnp.dot(a, b, precision=HI, preferred_element_type=F32)


def _mm_nt_hi(a, b):
    return lax.dot_general(a, b, (((1,), (1,)), ((), ())), precision=HI, preferred_element_type=F32)


def _sigmoid(x):
    return 1.0 / (1.0 + jnp.exp(-x))


def _silu(x):
    return x * _sigmoid(x)


def _softplus(x):
    return jnp.maximum(x, 0.0) + jnp.log(1.0 + jnp.exp(-jnp.abs(x)))


def _iota(shape, dim):
    return lax.broadcasted_iota(jnp.int32, shape, dim)


def _eye(n, m):
    return (_iota((n, m), 0) == _iota((n, m), 1)).astype(F32)


def _lower(n, inclusive):
    r, c = _iota((n, n), 0), _iota((n, n), 1)
    return (c <= r) if inclusive else (c < r)


def _rms_rows(x):
    return x * lax.rsqrt(jnp.mean(x * x, axis=-1, keepdims=True) + NORM_EPS)


def _tri_inv_minus_eye(n_mat, size):
    m = -n_mat
    steps = int(math.log2(size))
    pw = _mm(n_mat, n_mat)
    yield
    for i in range(1, steps):
        t = _mm(m, pw)
        nxt = _mm(pw, pw) if i < steps - 1 else None
        yield
        m = m + pw + t
        pw = nxt
    return m


def _round_robin(gens):
    gens = list(gens)
    results = [None] * len(gens)
    active = list(range(len(gens)))
    while active:
        for i in list(active):
            try:
                next(gens[i])
            except StopIteration as stop:
                results[i] = stop.value
                active.remove(i)
    return results


def _ada_kernel(c_ref, w_ref, b_ref, o_ref):
    c = c_ref[...]
    o_ref[...] = _mm(_silu(c), w_ref[...]) + b_ref[...]


def _ada(c_all, ada_w, ada_b):
    depth, d, n6 = ada_w.shape
    rows = c_all.shape[0]
    tn = 1024
    return pl.pallas_call(
        _ada_kernel,
        grid=(depth, n6 // tn),
        in_specs=[pl.BlockSpec((rows, d), lambda l, j: (0, 0)),
                  pl.BlockSpec((None, d, tn), lambda l, j: (l, 0, j)),
                  pl.BlockSpec((None, 1, tn), lambda l, j: (l, 0, j))],
        out_specs=pl.BlockSpec((None, rows, tn), lambda l, j: (l, 0, j)),
        out_shape=jax.ShapeDtypeStruct((depth, rows, n6), F32),
        compiler_params=_cparams(("parallel", "parallel")),
        name="ada",
    )(c_all, ada_w, ada_b.reshape(depth, 1, n6))


def _modulated_norm(x, nw, sc, sh):
    return _rms_rows(x) * nw * (1.0 + sc) + sh


def _inproj_kernel(*refs, combine):
    if combine:
        x_ref, ff_ref, gt_ref, sh_ref, sc_ref, nw_ref, w_ref, xo_ref, p_ref, h_ref = refs
    else:
        x_ref, sh_ref, sc_ref, nw_ref, w_ref, p_ref, h_ref = refs

    @pl.when(pl.program_id(1) == 0)
    def _():
        x = x_ref[...]
        if combine:
            rows = x.shape[0]
            x = x + gt_ref[...] * (_load_slabs(ff_ref.at[0], rows) + _load_slabs(ff_ref.at[1], rows))
            xo_ref[...] = x
        h_ref[...] = _modulated_norm(x, nw_ref[...], sc_ref[...], sh_ref[...]).astype(BF16)

    p_ref[...] = jnp.dot(h_ref[...], w_ref[...], preferred_element_type=F32)


def _mod_spec(mod3, which, tm, rows_per_group):
    r = mod3.shape[1]
    return pl.BlockSpec((None, r, D_MODEL),
                        lambda i, *_: ((i * tm) // rows_per_group, 0, which))


def _in_proj(x, mod3, rows_per_group, nw, w, ff=None, ff_row0=0, gate_mod3=None):
    n = x.shape[0]
    tm = min(512, rows_per_group)
    tn = 1024
    combine = ff is not None
    row = lambda i, j: (i, 0)
    in_specs = [pl.BlockSpec((tm, D_MODEL), row)]
    args = [x]
    if combine:
        blk0 = ff_row0 // tm
        in_specs += [pl.BlockSpec((2, tm * SLAB, 128), lambda i, j: (0, blk0 + i, 0)),
                     _mod_spec(gate_mod3, 5, tm, rows_per_group)]
        args += [ff, gate_mod3]
    in_specs += [_mod_spec(mod3, 0, tm, rows_per_group), _mod_spec(mod3, 1, tm, rows_per_group),
                 pl.BlockSpec((1, D_MODEL), lambda i, j: (0, 0)),
                 pl.BlockSpec((D_MODEL, tn), lambda i, j: (0, j))]
    args += [mod3, mod3, nw.reshape(1, D_MODEL), w]
    out_specs = [pl.BlockSpec((tm, tn), lambda i, j: (i, j))]
    out_shape = [jax.ShapeDtypeStruct((n, PROJ_PAD), F32)]
    if combine:
        out_specs = [pl.BlockSpec((tm, D_MODEL), row)] + out_specs
        out_shape = [jax.ShapeDtypeStruct((n, D_MODEL), F32)] + out_shape
    res = pl.pallas_call(
        functools.partial(_inproj_kernel, combine=combine),
        grid=(n // tm, PROJ_PAD // tn),
        in_specs=in_specs, out_specs=out_specs, out_shape=out_shape,
        scratch_shapes=[pltpu.VMEM((tm, D_MODEL), BF16)],
        compiler_params=_cparams(("parallel", "arbitrary")),
        name="in_proj",
    )(*args)
    return (res[0], res[1]) if combine else (x, res[0])


def _lane_max(x):
    return jnp.max(x, axis=-1, keepdims=True)


def _lane_sum(x):
    return jnp.sum(x, axis=-1, keepdims=True)


def _first_lane_of(mask, lane):
    return jnp.min(jnp.where(mask, lane, 4096), axis=-1, keepdims=True)


def _route(logits, biased):
    lane = _iota(logits.shape, 1)
    neg = jnp.float32(-jnp.inf)
    is_g = lane < N_GROUPS
    gmax = _lane_max(jnp.where(is_g, logits, neg))
    gexp = jnp.where(is_g, jnp.exp(logits - gmax), 0.0)
    gb = jnp.where(is_g, biased, neg)
    gsel = _first_lane_of(gb == _lane_max(gb), lane)
    pg = _lane_sum(jnp.where(lane == gsel, gexp, 0.0)) / _lane_sum(gexp)
    lo = N_GROUPS + gsel * EPG
    eb = jnp.where((lane >= lo) & (lane < lo + EPG), biased, neg)
    i1 = _first_lane_of(eb == _lane_max(eb), lane)
    eb2 = jnp.where(lane == i1, neg, eb)
    i2 = _first_lane_of(eb2 == _lane_max(eb2), lane)
    l1 = _lane_sum(jnp.where(lane == i1, logits, 0.0))
    l2 = _lane_sum(jnp.where(lane == i2, logits, 0.0))
    mx = jnp.maximum(l1, l2)
    e1, e2 = jnp.exp(l1 - mx), jnp.exp(l2 - mx)
    cw1, cw2 = pg * e1 / (e1 + e2), pg * e2 / (e1 + e2)
    id1, id2 = (i1 - N_GROUPS).astype(F32), (i2 - N_GROUPS).astype(F32)
    return jnp.where(lane == 0, id1, jnp.where(lane == 1, id2,
                     jnp.where(lane == 2, cw1, jnp.where(lane == 3, cw2, 0.0))))


SLAB = D_MODEL // 128


def _load_slabs(ref, rows):
    return jnp.concatenate([ref[pl.ds(s, rows, stride=SLAB), :] for s in range(SLAB)], axis=1)


def _store_slabs(ref, x, rows):
    for s in range(SLAB):
        ref[pl.ds(s, rows, stride=SLAB), :] = x[:, s * 128:(s + 1) * 128]


def _outproj_kernel(*refs, n_prompt_tiles, tm, rows_s):
    group_p, group_s = refs[0:8], refs[8:16]
    nw_ref, wo_ref, wr_ref, br_ref, xop_ref, xos_ref, h2_ref, rt_ref = refs[16:]
    i = pl.program_id(0)

    def body(group, xo_ref, rows):
        yr_ref, ym_ref, yh_ref, yg_ref, x_ref, gt_ref, sh_ref, sc_ref = group
        mix = jnp.dot(yr_ref[...], wo_ref[0:GW, :], preferred_element_type=F32)
        mix += jnp.dot(ym_ref[...], wo_ref[GW:2 * GW, :], preferred_element_type=F32)
        mix += jnp.dot(yh_ref[...], wo_ref[2 * GW:3 * GW, :], preferred_element_type=F32)
        mix += jnp.dot(yg_ref[...], wo_ref[3 * GW:4 * GW, :], preferred_element_type=F32)
        x = x_ref[...] + gt_ref[...] * mix
        xo_ref[...] = x
        h2 = _modulated_norm(x, nw_ref[...], sc_ref[...], sh_ref[...])
        h_hi = h2.astype(BF16)
        h_lo = (h2 - h_hi.astype(F32)).astype(BF16)
        hw = jnp.dot(h_hi, wr_ref[...], preferred_element_type=F32)
        logits = hw[:, 0:128] + hw[:, 128:256] + jnp.dot(h_lo, wr_ref[:, 0:128], preferred_element_type=F32)
        route = _route(logits, logits + br_ref[...])
        _store_slabs(h2_ref, h2, rows)
        rt_ref[0:rows, :] = route
        if rows < tm:
            h2_ref[rows * SLAB:tm * SLAB, :] = jnp.zeros(((tm - rows) * SLAB, 128), F32)
            rt_ref[rows:tm, :] = jnp.zeros((tm - rows, 128), F32)

    @pl.when(i < n_prompt_tiles)
    def _():
        body(group_p, xop_ref, tm)

    @pl.when(i == n_prompt_tiles)
    def _():
        body(group_s, xos_ref, rows_s)


def _out_proj(ys_p, x_p, mod_p, t, ys_s, x_s, mod_s, nw, wo, w_route, b_route):
    n_p, n_s = x_p.shape[0], x_s.shape[0]
    tm = min(256, t)
    assert n_s <= tm and n_p % tm == 0
    n_pt = n_p // tm
    prow = lambda i: (jnp.minimum(i, n_pt - 1), 0)
    const = lambda i: (0, 0)
    pmod = lambda which: pl.BlockSpec((None, 1, D_MODEL),
                                      lambda i: ((jnp.minimum(i, n_pt - 1) * tm) // t, 0, which))
    smod = lambda which: pl.BlockSpec((None, n_s, D_MODEL), lambda i: (0, 0, which))
    in_specs = ([pl.BlockSpec((tm, GW), prow)] * 4 + [pl.BlockSpec((tm, D_MODEL), prow), pmod(2), pmod(3), pmod(4)]
                + [pl.BlockSpec((n_s, GW), const)] * 4 + [pl.BlockSpec((n_s, D_MODEL), const),
                                                          smod(2), smod(3), smod(4)]
                + [pl.BlockSpec((1, D_MODEL), const), pl.BlockSpec((D_MODEL, D_MODEL), const),
                   pl.BlockSpec((D_MODEL, 256), const), pl.BlockSpec((1, 128), const)])
    n_all = n_p + tm
    return pl.pallas_call(
        functools.partial(_outproj_kernel, n_prompt_tiles=n_pt, tm=tm, rows_s=n_s),
        grid=(n_pt + 1,),
        in_specs=in_specs,
        out_specs=[pl.BlockSpec((tm, D_MODEL), prow), pl.BlockSpec((n_s, D_MODEL), const),
                   pl.BlockSpec((tm * SLAB, 128), lambda i: (i, 0)), pl.BlockSpec((tm, 128), lambda i: (i, 0))],
        out_shape=[jax.ShapeDtypeStruct((n_p, D_MODEL), F32), jax.ShapeDtypeStruct((n_s, D_MODEL), F32),
                   jax.ShapeDtypeStruct((n_all * SLAB, 128), F32), jax.ShapeDtypeStruct((n_all, 128), F32)],
        compiler_params=_cparams(("arbitrary",)),
        name="out_proj",
    )(*ys_p, x_p, mod_p, mod_p, mod_p, *ys_s, x_s, mod_s, mod_s, mod_s,
      nw.reshape(1, D_MODEL), wo, w_route, b_route)


def _moe_kernel(te_ref, tok_ref, dst_ref, h2_hbm, cw_ref, wg_ref, wu_ref, wd_ref, out_hbm,
                hbuf0, hbuf1, obuf0, obuf1, wgb, wub, wdb, gsem, ssem, *, tm, n_tiles):
    t = pl.program_id(0)
    hbufs, obufs = (hbuf0, hbuf1), (obuf0, obuf1)

    def start_gather(tile, slot):
        for r in range(tm):
            src = pl.multiple_of(tok_ref[tile * tm + r], SLAB)
            pltpu.make_async_copy(h2_hbm.at[pl.ds(src, SLAB)], hbufs[slot].at[pl.ds(r * SLAB, SLAB)],
                                  gsem.at[slot]).start()

    def start_scatter(tile, slot):
        for r in range(tm):
            dst = pl.multiple_of(dst_ref[tile * tm + r], SLAB)
            pltpu.make_async_copy(obufs[slot].at[pl.ds(r * SLAB, SLAB)], out_hbm.at[pl.ds(dst, SLAB)],
                                  ssem.at[slot]).start()

    def wait_gather(slot):
        pltpu.make_async_copy(h2_hbm.at[pl.ds(0, tm * SLAB)], hbufs[slot], gsem.at[slot]).wait()

    def wait_scatter(slot):
        pltpu.make_async_copy(obufs[slot], out_hbm.at[pl.ds(0, tm * SLAB)], ssem.at[slot]).wait()

    @pl.when(t == 0)
    def _():
        start_gather(0, 0)

    @pl.when((t == 0) | (te_ref[t] != te_ref[jnp.maximum(t - 1, 0)]))
    def _():
        wgb[...] = wg_ref[...].astype(BF16)
        wub[...] = wu_ref[...].astype(BF16)
        wdb[...] = wd_ref[...].astype(BF16)

    def step(slot):
        wait_gather(slot)
        start_gather(t + 1, 1 - slot)
        h = _load_slabs(hbufs[slot], tm).astype(BF16)
        g = jnp.dot(h, wgb[...], preferred_element_type=F32)
        u = jnp.dot(h, wub[...], preferred_element_type=F32)
        hid = (_silu(g) * u * cw_ref[...]).astype(BF16)
        _store_slabs(obufs[slot], jnp.dot(hid, wdb[...], preferred_element_type=F32), tm)
        start_scatter(t, slot)

    for slot in (0, 1):
        @pl.when(t % 2 == slot)
        def _(slot=slot):
            step(slot)

        @pl.when((t >= 1) & (t % 2 == slot))
        def _(slot=slot):
            wait_scatter(1 - slot)

    @pl.when(t == n_tiles - 1)
    def _():
        wait_scatter((n_tiles - 1) % 2)
        wait_gather(n_tiles % 2)


MOE_TM = 256


def _moe(h2, route, n_tok, w_gate, w_up, w_down, layer):
    tm = MOE_TM
    slots = 2 * n_tok
    n_tiles = -(-(slots + N_EXPERTS * (tm - 1)) // tm) + 2
    n_pad = n_tok + tm
    n_rows = (n_tiles + 1) * tm
    eid = jnp.concatenate([route[:n_tok, 0], route[:n_tok, 1]]).astype(jnp.int32)
    cw = jnp.concatenate([route[:n_tok, 2], route[:n_tok, 3]])
    onehot = (eid[:, None] == jnp.arange(N_EXPERTS, dtype=jnp.int32)[None, :]).astype(jnp.int32)
    running = jnp.cumsum(onehot, axis=0)
    counts = running[-1]
    tiles_per = (counts + tm - 1) // tm
    tile_end = jnp.cumsum(tiles_per)
    tile_start = tile_end - tiles_per
    rank = jnp.sum(running * onehot, axis=1) - 1
    pos = jnp.sum(onehot * tile_start[None, :], axis=1) * tm + rank
    s_idx = jnp.arange(slots, dtype=jnp.int32)
    tok = s_idx % n_tok
    dst = (s_idx // n_tok) * n_pad + tok
    j = jnp.arange(n_rows, dtype=jnp.int32)
    dump = ((j // tm) % 2) * n_pad + n_tok + j % tm
    default = jnp.stack([jnp.zeros_like(j), dump, jnp.zeros_like(j)], axis=1).astype(F32)
    packed = jnp.stack([tok.astype(F32), dst.astype(F32), cw], axis=1)
    rows = default.at[pos].set(packed)
    row_tok = rows[:, 0].astype(jnp.int32) * SLAB
    row_dst = rows[:, 1].astype(jnp.int32) * SLAB
    row_cw = rows[:, 2].reshape(n_rows, 1)
    tile_ids = jnp.arange(n_tiles, dtype=jnp.int32)
    tile_e = jnp.sum((tile_end[None, :] <= tile_ids[:, None]).astype(jnp.int32), axis=1)
    tile_e = jnp.minimum(tile_e, N_EXPERTS - 1)

    wspec = lambda shape: pl.BlockSpec((None, None) + shape, lambda t, te, *_: (layer, te[t], 0, 0))
    grid_spec = pltpu.PrefetchScalarGridSpec(
        num_scalar_prefetch=3,
        grid=(n_tiles,),
        in_specs=[pl.BlockSpec(memory_space=pl.ANY),
                  pl.BlockSpec((tm, 1), lambda t, *_: (t, 0)),
                  wspec((D_MODEL, D_FF)), wspec((D_MODEL, D_FF)), wspec((D_FF, D_MODEL))],
        out_specs=pl.BlockSpec(memory_space=pl.ANY),
        scratch_shapes=[pltpu.VMEM((tm * SLAB, 128), F32)] * 4
        + [pltpu.VMEM((D_MODEL, D_FF), BF16), pltpu.VMEM((D_MODEL, D_FF), BF16),
           pltpu.VMEM((D_FF, D_MODEL), BF16),
           pltpu.SemaphoreType.DMA((2,)), pltpu.SemaphoreType.DMA((2,))])
    out = pl.pallas_call(
        functools.partial(_moe_kernel, tm=tm, n_tiles=n_tiles),
        grid_spec=grid_spec,
        out_shape=jax.ShapeDtypeStruct((2 * n_pad * SLAB, 128), F32),
        compiler_params=_cparams(("arbitrary",)),
        name="moe",
    )(tile_e, row_tok, row_dst, h2, row_cw, w_gate, w_up, w_down)
    return out.reshape(2, n_pad * SLAB, 128)


def _final_kernel(x_ref, ff_ref, gt_ref, nw_ref, o_ref):
    rows = x_ref.shape[0]
    x = x_ref[...] + gt_ref[...] * (_load_slabs(ff_ref.at[0], rows) + _load_slabs(ff_ref.at[1], rows))
    o_ref[...] = _rms_rows(x) * nw_ref[...]


def _final(x, ff, ff_row0, mod3, rows_per_group, nw):
    n = x.shape[0]
    tm = min(512, rows_per_group)
    blk0 = ff_row0 // tm
    return pl.pallas_call(
        _final_kernel,
        grid=(n // tm,),
        in_specs=[pl.BlockSpec((tm, D_MODEL), lambda i: (i, 0)),
                  pl.BlockSpec((2, tm * SLAB, 128), lambda i: (0, blk0 + i, 0)),
                  _mod_spec(mod3, 5, tm, rows_per_group),
                  pl.BlockSpec((1, D_MODEL), lambda i: (0, 0))],
        out_specs=pl.BlockSpec((tm, D_MODEL), lambda i: (i, 0)),
        out_shape=jax.ShapeDtypeStruct((n, D_MODEL), F32),
        compiler_params=_cparams(("parallel",)),
        name="final_norm",
    )(x, ff, mod3, nw.reshape(1, D_MODEL))


def _seg_sum(x, width):
    seg = (_iota((128, 128), 0) // width == _iota((128, 128), 1) // width).astype(F32)
    parts = [_mm_hi(x[:, j:j + 128], seg) for j in range(0, x.shape[1], 128)]
    return parts[0] if len(parts) == 1 else jnp.concatenate(parts, axis=1)


def _head_norm(o, nw, gate):
    return o * lax.rsqrt(_seg_sum(o * o, HD) * (1.0 / HD) + NORM_EPS) * nw * gate


def _carry_rows(ext_ref, cur, first, L):
    @pl.when(first)
    def _():
        ext_ref[0:8, :] = jnp.zeros((8, ext_ref.shape[1]), F32)

    ext_ref[8:8 + L, :] = cur


def _rwkv_kernel(p_ref, mu_ref, vec_ref, wup_ref, aup_ref, gup_ref, y_ref, sh_ref, s_ref, ext_ref, *, L):
    c = pl.program_id(1)
    p = p_ref[...]
    _carry_rows(ext_ref, p, c == 0, L)
    prev = ext_ref[7:7 + L, :]
    xs = p + (prev - p) * mu_ref[...]
    ext_ref[0:8, :] = ext_ref[L:L + 8, :]
    sh_ref[...] = p[L - 1:L, :]

    @pl.when(c == 0)
    def _():
        s_ref[...] = jnp.zeros_like(s_ref)

    w0, a0, k_k, k_a, r_k, ln_w, ln_b = [vec_ref[i:i + 1, :] for i in range(7)]
    r, k, v = xs[:, 0:GW], xs[:, GW:2 * GW], xs[:, 2 * GW:3 * GW]
    dw, da, dg = xs[:, 1536:1600], xs[:, 1600:1664], xs[:, 1664:1792]
    logw = -_softplus(-(w0 + _mm(jnp.tanh(dw), wup_ref[...]))) - 0.5
    lw = -jnp.exp(logw)
    a = _sigmoid(a0 + _mm(da, aup_ref[...]))
    g = _mm(_sigmoid(dg), gup_ref[...])
    kk = k * k_k
    kk = kk * lax.rsqrt(jnp.maximum(_seg_sum(kk * kk, RWKV_HD), 1e-12))
    k2 = k * (1.0 + (a - 1.0) * k_a)
    cl = _mm_hi(_lower(L, True).astype(F32), lw)
    cl_last = cl[L - 1:L, :]
    at = -kk * jnp.exp(cl - lw)
    bt = kk * a * jnp.exp(-cl)
    kt = k2 * jnp.exp(-cl)
    rt = r * jnp.exp(cl)
    b_end = kk * a * jnp.exp(cl_last - cl)
    k_end = k2 * jnp.exp(cl_last - cl)
    strict, incl = _lower(L, False), _lower(L, True)
    def head(h):
        sl = slice(h * RWKV_HD, (h + 1) * RWKV_HD)
        s0 = s_ref[h]
        ath, bth, kth, rth, vh = at[:, sl], bt[:, sl], kt[:, sl], rt[:, sl], v[:, sl]
        g_ab, g_ak, g_rb, g_rk = _mm_nt(ath, bth), _mm_nt(ath, kth), _mm_nt(rth, bth), _mm_nt(rth, kth)
        a_s0, r_s0 = _mm_nt(ath, s0), _mm_nt(rth, s0)
        s_v = _mm_tn(vh, k_end[:, sl])
        yield
        n_ab = jnp.where(strict, g_ab, 0.0)
        ak_v = _mm(jnp.where(strict, g_ak, 0.0), vh)
        rk_v = _mm(jnp.where(incl, g_rk, 0.0), vh)
        m = yield from _tri_inv_minus_eye(-n_ab, L)
        rhs = a_s0 + ak_v
        m_rhs = _mm(m, rhs)
        yield
        u = rhs + m_rhs
        rb_u = _mm(jnp.where(incl, g_rb, 0.0), u)
        s_u = _mm_tn(u, b_end[:, sl])
        yield
        s_ref[h] = s0 * jnp.exp(cl_last[:, sl]) + s_u + s_v
        return r_s0 + rb_u + rk_v

    y = jnp.concatenate(_round_robin(head(h) for h in range(RWKV_H)), axis=1)
    mean = _seg_sum(y, RWKV_HD) * (1.0 / RWKV_HD)
    dev = y - mean
    var = _seg_sum(dev * dev, RWKV_HD) * (1.0 / RWKV_HD)
    y = dev * lax.rsqrt(var + RWKV_LN_EPS) * ln_w + ln_b
    bonus = _seg_sum(r * k2 * r_k, RWKV_HD) * v
    y_ref[...] = ((y + bonus) * g).astype(y_ref.dtype)


def _gate_forms(g_ref, gp_ref, L):
    raw = g_ref[:, 0:128] + gp_ref[0:1, :]
    lane = _iota(raw.shape, 1)
    lf = jnp.minimum(raw, 0.0) - jnp.log(1.0 + jnp.exp(-jnp.abs(raw)))
    beta = _sigmoid(raw)
    decay = -jnp.exp(gp_ref[1:2, :]) * _softplus(raw)
    cols = jnp.where(lane < 4, raw, jnp.where(lane < 8, lf, jnp.where(lane < 12, beta, decay)))
    rows = _mm_nt_hi(_eye(16, 128), cols)
    ccols = _mm_hi(_lower(L, True).astype(F32), cols)
    upper = (_iota((L, L), 0) <= _iota((L, L), 1)).astype(F32)
    crows = _mm_hi(rows, upper)
    return cols, rows, ccols, crows


def _mlstm_kernel(p_ref, g_ref, gp_ref, nw_ref, y_ref, c_ref, n_ref, m_ref, *, L):
    c = pl.program_id(1)

    @pl.when(c == 0)
    def _():
        c_ref[...] = jnp.zeros_like(c_ref)
        n_ref[...] = jnp.zeros_like(n_ref)
        m_ref[...] = jnp.zeros_like(m_ref)

    cols, rows, ccols, crows = _gate_forms(g_ref, gp_ref, L)
    causal = _lower(L, True)
    neg = jnp.float32(-jnp.inf)
    def head(h):
        q = p_ref[:, h * HD:(h + 1) * HD]
        k = p_ref[:, GW + h * HD:GW + (h + 1) * HD] * (HD ** -0.5)
        v = p_ref[:, 2 * GW + h * HD:2 * GW + (h + 1) * HD]
        ig_col, ig_row = cols[:, h:h + 1], rows[h:h + 1, :]
        b_col, b_row = ccols[:, 4 + h:5 + h], crows[4 + h:5 + h, :]
        m_prev = m_ref[:, h:h + 1]
        cmat, nvec = c_ref[h], n_ref[h:h + 1, :]
        qk, qc = _mm_nt(q, k), _mm_nt(q, cmat)
        dmat = jnp.where(causal, b_col - b_row + ig_row, neg)
        gcol = b_col + m_prev
        mt = jnp.maximum(gcol, _lane_max(dmat))
        m_last = mt[L - 1:L, :]
        wk = jnp.exp(b_col[L - 1:L, :] - b_col + ig_col - m_last)
        decay = jnp.exp(gcol[L - 1:L, :] - m_last)
        c_upd = _mm_tn(v * wk, k)
        yield
        smat = qk * jnp.exp(dmat - mt)
        s_v = _mm(smat, v)
        wg = jnp.exp(gcol - mt)
        den = _lane_sum(smat) + wg * _lane_sum(q * nvec)
        c_ref[h] = decay * cmat + c_upd
        n_ref[h:h + 1, :] = decay * nvec + jnp.sum(k * wk, axis=0, keepdims=True)
        m_ref[:, h:h + 1] = m_last
        yield
        return (s_v + wg * qc) / jnp.maximum(jnp.abs(den), jnp.exp(-mt))

    o = p_ref[:, 3 * GW:4 * GW]
    y = jnp.concatenate(_round_robin(head(h) for h in range(ML_H)), axis=1)
    y_ref[...] = _head_norm(y, nw_ref[...], _sigmoid(o)).astype(y_ref.dtype)


def _hgrn_kernel(p_ref, lb_ref, nw_ref, y_ref, s_ref, *, L):
    c = pl.program_id(1)

    @pl.when(c == 0)
    def _():
        s_ref[...] = jnp.zeros_like(s_ref)

    lb = lb_ref[...]
    qa = _silu(p_ref[:, 0:GW])
    fg = lb + (1.0 - lb) * _sigmoid(p_ref[:, GW:2 * GW])
    ka = 1.0 - fg
    va = p_ref[:, 2 * GW:3 * GW]
    cga = _mm_hi(_lower(L, True).astype(F32), jnp.log(fg))
    ones = jnp.ones((HD, HD), BF16)
    t3, s3 = _iota((SUB, SUB, HD), 0), _iota((SUB, SUB, HD), 1)
    neg = jnp.float32(-jnp.inf)
    def head(h):
        sl = slice(h * HD, (h + 1) * HD)
        q, k, v, cg = qa[:, sl], ka[:, sl], va[:, sl], cga[:, sl]
        st = s_ref[h]
        inter = _mm_nt(q * jnp.exp(cg), st)
        cg_last = cg[L - 1:L, :]
        s_upd = _mm_tn(v, k * jnp.exp(cg_last - cg))
        a3s, a_offs = [], []
        for i in range(L // SUB):
            lo = i * SUB
            qi, ki, cgi = q[lo:lo + SUB], k[lo:lo + SUB], cg[lo:lo + SUB]
            e3 = jnp.exp(jnp.where(s3 <= t3, cgi[:, None, :] - cgi[None, :, :], neg))
            x3 = qi[:, None, :] * ki[None, :, :] * e3
            a3s.append(jnp.dot(x3.reshape(SUB * SUB, HD).astype(BF16), ones, preferred_element_type=F32))
            if i > 0:
                ref = cg[lo - 1:lo, :]
                a_offs.append(_mm_nt(qi * jnp.exp(cgi - ref), k[0:lo] * jnp.exp(ref - cg[0:lo])))
        yield
        s_ref[h] = st * jnp.exp(cg_last) + s_upd
        offs = [_mm(a_off, v[0:(i + 1) * SUB]) for i, a_off in enumerate(a_offs)]
        yield
        rows_out = []
        for i in range(L // SUB):
            lo = i * SUB
            oi = inter[lo:lo + SUB] + jnp.sum(a3s[i].reshape(SUB, SUB, HD) * v[lo:lo + SUB][None, :, :], axis=1)
            rows_out.append(oi if i == 0 else oi + offs[i - 1])
        return jnp.concatenate(rows_out, axis=0)

    g = p_ref[:, 3 * GW:4 * GW]
    y = jnp.concatenate(_round_robin(head(h) for h in range(HG_H)), axis=1)
    y_ref[...] = _head_norm(y, nw_ref[...], _silu(g)).astype(y_ref.dtype)


def _gdn_kernel(p_ref, g_ref, gp_ref, cw_ref, nw_ref, y_ref, cv_ref, s_ref, ext_ref, *, L):
    c = pl.program_id(1)
    w3 = 3 * GW
    _carry_rows(ext_ref, p_ref[:, 0:w3], c == 0, L)
    conv = (cw_ref[3:4, :] * ext_ref[8:8 + L, :] + cw_ref[2:3, :] * ext_ref[7:7 + L, :]
            + cw_ref[1:2, :] * ext_ref[6:6 + L, :] + cw_ref[0:1, :] * ext_ref[5:5 + L, :])
    cv_ref[...] = ext_ref[L + 5:L + 8, :]
    ext_ref[0:8, :] = ext_ref[L:L + 8, :]

    @pl.when(c == 0)
    def _():
        s_ref[...] = jnp.zeros_like(s_ref)

    qkv = _silu(conv)
    cols, rows, ccols, crows = _gate_forms(g_ref, gp_ref, L)
    strict, causal = _lower(L, False), _lower(L, True)
    neg = jnp.float32(-jnp.inf)
    qa = qkv[:, 0:GW]
    ka = qkv[:, GW:2 * GW]
    qa = qa * lax.rsqrt(jnp.maximum(_seg_sum(qa * qa, HD), 1e-12)) * (HD ** -0.5)
    ka = ka * lax.rsqrt(jnp.maximum(_seg_sum(ka * ka, HD), 1e-12))

    def head(h):
        sl = slice(h * HD, (h + 1) * HD)
        q, k, v = qa[:, sl], ka[:, sl], qkv[:, 2 * GW + h * HD:2 * GW + (h + 1) * HD]
        beta = cols[:, COL_GDN_B + h:COL_GDN_B + h + 1]
        cg_col = ccols[:, COL_GDN_A + h:COL_GDN_A + h + 1]
        cg_row = crows[COL_GDN_A + h:COL_GDN_A + h + 1, :]
        diff = cg_col - cg_row
        s0 = s_ref[h]
        kb = k * beta
        g_kk, g_qk = _mm_nt(kb, k), _mm_nt(q, k)
        q_s0 = _mm(q * jnp.exp(cg_col), s0)
        yield
        n_mat = g_kk * jnp.exp(jnp.where(strict, diff, neg))
        m = yield from _tri_inv_minus_eye(n_mat, L)
        rhs = jnp.concatenate([v * beta, kb * jnp.exp(cg_col)], axis=1)
        m_rhs = _mm(m, rhs)
        yield
        uw = rhs + m_rhs
        w_s0 = _mm(uw[:, HD:2 * HD], s0)
        yield
        u2 = uw[:, 0:HD] - w_s0
        qk = g_qk * jnp.exp(jnp.where(causal, diff, neg))
        qk_u = _mm(qk, u2)
        cg_last = cg_col[L - 1:L, :]
        s_upd = _mm_tn(k * jnp.exp(cg_last - cg_col), u2)
        yield
        s_ref[h] = jnp.exp(cg_last) * s0 + s_upd
        return q_s0 + qk_u

    g = p_ref[:, w3:w3 + GW]
    y = jnp.concatenate(_round_robin(head(h) for h in range(GDN_H)), axis=1)
    y_ref[...] = _head_norm(y, nw_ref[...], _silu(g)).astype(y_ref.dtype)


def _prompt_mixers(p3, lp):
    b, t, _ = p3.shape
    L = math.gcd(t, CHUNK)
    grid = (b, t // L)
    cp = _cparams(("parallel", "arbitrary"))
    col = lambda width, idx: pl.BlockSpec((None, L, width), lambda i, c: (i, c, idx))
    const2 = lambda shape: pl.BlockSpec(shape, lambda i, c: (0, 0))
    y_spec = pl.BlockSpec((None, L, GW), lambda i, c: (i, c, 0))
    y_shape = jax.ShapeDtypeStruct((b, t, GW), BF16)
    state = lambda *s: (pl.BlockSpec((None,) + s, lambda i, c: (i,) + (0,) * len(s)),
                        jax.ShapeDtypeStruct((b,) + s, F32))
    gates = col(GATES_W, GATES_OFF // GATES_W)

    specs, shapes = zip(state(1, RWKV_PROJ), state(RWKV_H, RWKV_HD, RWKV_HD))
    yr, n_sh, n_wkv = pl.pallas_call(
        functools.partial(_rwkv_kernel, L=L), grid=grid,
        in_specs=[col(RWKV_PROJ, 0), const2((1, RWKV_PROJ)), const2((8, GW)), const2((64, GW)),
                  const2((64, GW)), const2((128, GW))],
        out_specs=[y_spec, *specs], out_shape=[y_shape, *shapes],
        scratch_shapes=[pltpu.VMEM((L + 8, RWKV_PROJ), F32)], compiler_params=cp, name="rwkv7",
    )(p3, lp["rwkv_mu"], lp["rwkv_vec"], lp["rwkv_w_up"], lp["rwkv_a_up"], lp["rwkv_g_up"])

    specs, shapes = zip(state(ML_H, HD, HD), state(ML_H, HD), state(1, 128))
    ym, n_c, n_n, n_m = pl.pallas_call(
        functools.partial(_mlstm_kernel, L=L), grid=grid,
        in_specs=[col(4 * GW, 1), gates, const2((8, 128)), const2((1, GW))],
        out_specs=[y_spec, *specs], out_shape=[y_shape, *shapes], compiler_params=cp, name="mlstm",
    )(p3, p3, lp["gate_par"], lp["ml_norm"])

    specs, shapes = zip(state(HG_H, HD, HD))
    yh, n_hg = pl.pallas_call(
        functools.partial(_hgrn_kernel, L=L), grid=grid,
        in_specs=[col(4 * GW, 2), const2((1, GW)), const2((1, GW))],
        out_specs=[y_spec, *specs], out_shape=[y_shape, *shapes], compiler_params=cp, name="hgrn2",
    )(p3, lp["hg_lb"], lp["hg_norm"])

    specs, shapes = zip(state(CONV_W - 1, 3 * GW), state(GDN_H, HD, HD))
    yg, n_cv, n_gd = pl.pallas_call(
        functools.partial(_gdn_kernel, L=L), grid=grid,
        in_specs=[col(4 * GW, 3), gates, const2((8, 128)), const2((CONV_W, 3 * GW)), const2((1, GW))],
        out_specs=[y_spec, *specs], out_shape=[y_shape, *shapes],
        scratch_shapes=[pltpu.VMEM((L + 8, 3 * GW), F32)], compiler_params=cp, name="gdn",
    )(p3, p3, lp["gate_par"], lp["gdn_conv_w"], lp["gdn_norm"])

    ys = [y.reshape(b * t, GW) for y in (yr, ym, yh, yg)]
    states = (n_sh, n_wkv, n_c, n_n, n_m[:, 0, :ML_H], jnp.swapaxes(n_hg, -1, -2), n_cv, n_gd)
    return ys, states


DEC_BS = 8


def _col_to_row(col):
    n = col.shape[0]
    return jnp.sum(jnp.where(_iota((n, n), 0) == _iota((n, n), 1), col, 0.0), axis=0, keepdims=True)


def _sub_sum(x):
    return jnp.sum(x, axis=0, keepdims=True)


def _decode_kernel(p_ref, sh_ref, wkv_ref, mc_ref, mn_ref, mm_ref, hg_ref, cv_ref, gd_ref,
                   mu_ref, vec_ref, wup_ref, aup_ref, gup_ref, gp_ref, mlw_ref, lb_ref, hgw_ref,
                   cw_ref, gdw_ref,
                   yr_ref, ym_ref, yh_ref, yg_ref,
                   nsh_ref, nwkv_ref, nmc_ref, nmn_ref, nmm_ref, nhg_ref, ncv_ref, ngd_ref):
    bs = DEC_BS
    w3 = 3 * GW
    pr = p_ref[:, 0:RWKV_PROJ]
    xs = pr + (sh_ref[...] - pr) * mu_ref[...]
    nsh_ref[...] = pr
    w0, a0, k_k, k_a, r_k, ln_w, ln_b = [vec_ref[i:i + 1, :] for i in range(7)]
    r, k, v = xs[:, 0:GW], xs[:, GW:2 * GW], xs[:, 2 * GW:3 * GW]
    dw, da, dg = xs[:, 1536:1600], xs[:, 1600:1664], xs[:, 1664:1792]
    logw = -_softplus(-(w0 + _mm(jnp.tanh(dw), wup_ref[...]))) - 0.5
    wdec = jnp.exp(-jnp.exp(logw))
    a = _sigmoid(a0 + _mm(da, aup_ref[...]))
    g_r = _mm(_sigmoid(dg), gup_ref[...])
    kk = k * k_k
    kk = kk * lax.rsqrt(jnp.maximum(_seg_sum(kk * kk, RWKV_HD), 1e-12))
    k2 = k * (1.0 + (a - 1.0) * k_a)
    ra, rb = -kk, kk * a

    gates = p_ref[:, GATES_OFF:GATES_OFF + 128] + gp_ref[0:1, :]
    ml_off = GATES_OFF + GATES_W
    mq = p_ref[:, ml_off:ml_off + GW]
    mk = p_ref[:, ml_off + GW:ml_off + 2 * GW] * (HD ** -0.5)
    mv = p_ref[:, ml_off + 2 * GW:ml_off + 3 * GW]
    mo = p_ref[:, ml_off + 3 * GW:ml_off + 4 * GW]
    m_ig = gates
    m_lf = jnp.minimum(gates, 0.0) - jnp.log(1.0 + jnp.exp(-jnp.abs(gates)))

    hg_off = ml_off + 4 * GW
    lb = lb_ref[...]
    hq = _silu(p_ref[:, hg_off:hg_off + GW])
    hfg = lb + (1.0 - lb) * _sigmoid(p_ref[:, hg_off + GW:hg_off + 2 * GW])
    hv = p_ref[:, hg_off + 2 * GW:hg_off + 3 * GW]
    hgate = p_ref[:, hg_off + 3 * GW:hg_off + 4 * GW]

    gd_off = hg_off + 4 * GW
    cur = p_ref[:, gd_off:gd_off + w3]
    conv = (cw_ref[3:4, :] * cur + cw_ref[2:3, :] * cv_ref[:, 2 * w3:3 * w3]
            + cw_ref[1:2, :] * cv_ref[:, w3:2 * w3] + cw_ref[0:1, :] * cv_ref[:, 0:w3])
    ncv_ref[:, 0:2 * w3] = cv_ref[:, w3:3 * w3]
    ncv_ref[:, 2 * w3:3 * w3] = cur
    qkv = _silu(conv)
    gq, gk, gv = qkv[:, 0:GW], qkv[:, GW:2 * GW], qkv[:, 2 * GW:3 * GW]
    gq = gq * lax.rsqrt(jnp.maximum(_seg_sum(gq * gq, HD), 1e-12)) * (HD ** -0.5)
    gk = gk * lax.rsqrt(jnp.maximum(_seg_sum(gk * gk, HD), 1e-12))
    ggate = p_ref[:, gd_off + w3:gd_off + w3 + GW]
    g_beta = _sigmoid(gates)
    g_dec = jnp.exp(-jnp.exp(gp_ref[1:2, :]) * _softplus(gates))

    blocks = ([mv[:, h * HD:(h + 1) * HD] for h in range(ML_H)]
              + [hq[:, h * HD:(h + 1) * HD] for h in range(HG_H)]
              + [hfg[:, h * HD:(h + 1) * HD] for h in range(HG_H)]
              + [gq[:, h * HD:(h + 1) * HD] for h in range(GDN_H)]
              + [gk[:, h * HD:(h + 1) * HD] for h in range(GDN_H)]
              + [v[:, j * 128:(j + 1) * 128] for j in range(4)])
    xt = _mm_nt_hi(_eye(128, 128), jnp.concatenate(blocks, axis=0))
    colf = lambda blk, s: xt[:, blk * bs + s:blk * bs + s + 1]

    yr_rows, ym_rows, yh_rows, yg_rows = [], [], [], []
    for s in range(bs):
        row = lambda arr, lo, width: arr[s:s + 1, lo:lo + width]
        parts = []
        for h in range(RWKV_H):
            lo = h * RWKV_HD
            st = wkv_ref[s, h]
            v_col = colf(20 + h // 2, s)[(h % 2) * 64:(h % 2) * 64 + 64, :]
            sa = _lane_sum(st * row(ra, lo, 64))
            st = st * row(wdec, lo, 64) + sa * row(rb, lo, 64) + v_col * row(k2, lo, 64)
            nwkv_ref[s, h] = st
            parts.append(_col_to_row(_lane_sum(st * row(r, lo, 64))))
        yr_rows.append(jnp.concatenate(parts, axis=1))
        parts = []
        for h in range(ML_H):
            lo = h * HD
            q_r, k_r = row(mq, lo, HD), row(mk, lo, HD)
            ig, lf = m_ig[s:s + 1, COL_ML_I + h:COL_ML_I + h + 1], m_lf[s:s + 1, COL_ML_F + h:COL_ML_F + h + 1]
            cmat, nvec, m_prev = mc_ref[s, h], mn_ref[s, h:h + 1, :], mm_ref[s:s + 1, h:h + 1]
            gsc = lf + m_prev
            mt = jnp.maximum(gsc, ig)
            wi, wg = jnp.exp(ig - mt), jnp.exp(gsc - mt)
            sc = _lane_sum(q_r * k_r) * wi
            v_col = colf(h, s)
            num = sc * v_col + wg * _lane_sum(cmat * q_r)
            den = sc + wg * _lane_sum(nvec * q_r)
            parts.append(_col_to_row(num / jnp.maximum(jnp.abs(den), jnp.exp(-mt))))
            nmc_ref[s, h] = wg * cmat + (wi * v_col) * k_r
            nmn_ref[s, h:h + 1, :] = wg * nvec + wi * k_r
            nmm_ref[s:s + 1, h:h + 1] = mt
        ym_rows.append(jnp.concatenate(parts, axis=1))
        parts = []
        for h in range(HG_H):
            lo = h * HD
            st = hg_ref[s, h]
            q_col, fg_col = colf(4 + h, s), colf(8 + h, s)
            q_r, fg_r, v_r = row(hq, lo, HD), row(hfg, lo, HD), row(hv, lo, HD)
            parts.append(_sub_sum(st * (q_col * fg_col)) + _lane_sum(q_r * (1.0 - fg_r)) * v_r)
            nhg_ref[s, h] = fg_col * st + (1.0 - fg_col) * v_r
        yh_rows.append(jnp.concatenate(parts, axis=1))
        parts = []
        for h in range(GDN_H):
            lo = h * HD
            st = gd_ref[s, h]
            q_col, k_col = colf(12 + h, s), colf(16 + h, s)
            beta = g_beta[s:s + 1, COL_GDN_B + h:COL_GDN_B + h + 1]
            dec = g_dec[s:s + 1, COL_GDN_A + h:COL_GDN_A + h + 1]
            u2 = beta * row(gv, lo, HD) - _sub_sum(st * (k_col * (beta * dec)))
            qk = _lane_sum(row(gq, lo, HD) * row(gk, lo, HD))
            parts.append(dec * _sub_sum(st * q_col) + qk * u2)
            ngd_ref[s, h] = dec * st + k_col * u2
        yg_rows.append(jnp.concatenate(parts, axis=1))

    yr = jnp.concatenate(yr_rows, axis=0)
    mean = _seg_sum(yr, RWKV_HD) * (1.0 / RWKV_HD)
    dev = yr - mean
    var = _seg_sum(dev * dev, RWKV_HD) * (1.0 / RWKV_HD)
    yr = dev * lax.rsqrt(var + RWKV_LN_EPS) * ln_w + ln_b
    bonus = _seg_sum(r * k2 * r_k, RWKV_HD) * v
    yr_ref[...] = ((yr + bonus) * g_r).astype(yr_ref.dtype)
    ym_ref[...] = _head_norm(jnp.concatenate(ym_rows, axis=0), mlw_ref[...], _sigmoid(mo)).astype(ym_ref.dtype)
    yh_ref[...] = _head_norm(jnp.concatenate(yh_rows, axis=0), hgw_ref[...], _silu(hgate)).astype(yh_ref.dtype)
    yg_ref[...] = _head_norm(jnp.concatenate(yg_rows, axis=0), gdw_ref[...], _silu(ggate)).astype(yg_ref.dtype)


def _sample_mixers(p, states, layer, lp):
    n = p.shape[0]
    bs = DEC_BS
    sh, wkv, mc, mn, mm, hg, cv, gd = states
    sh2 = sh.reshape(DEPTH, n, RWKV_PROJ)
    cv = cv.reshape(DEPTH, n, (CONV_W - 1) * 3 * GW)

    def st_in(arr):
        tail = arr.shape[2:]
        return pl.BlockSpec((None, bs) + tail, lambda i: (layer, i) + (0,) * len(tail))

    def st_out(arr):
        tail = arr.shape[2:]
        return (pl.BlockSpec((bs,) + tail, lambda i: (i,) + (0,) * len(tail)),
                jax.ShapeDtypeStruct((n,) + tail, F32))

    const2 = lambda shape: pl.BlockSpec(shape, lambda i: (0, 0))
    st_arrays = (sh2, wkv, mc, mn, mm, hg, cv, gd)
    o_specs, o_shapes = zip(*[st_out(a) for a in st_arrays])
    y_spec = pl.BlockSpec((bs, GW), lambda i: (i, 0))
    y_shape = jax.ShapeDtypeStruct((n, GW), BF16)
    res = pl.pallas_call(
        _decode_kernel,
        grid=(n // bs,),
        in_specs=[pl.BlockSpec((bs, PROJ_PAD), lambda i: (i, 0))] + [st_in(a) for a in st_arrays]
        + [const2((1, RWKV_PROJ)), const2((8, GW)), const2((64, GW)), const2((64, GW)), const2((128, GW)),
           const2((8, 128)), const2((1, GW)), const2((1, GW)), const2((1, GW)),
           const2((CONV_W, 3 * GW)), const2((1, GW))],
        out_specs=[y_spec] * 4 + list(o_specs),
        out_shape=[y_shape] * 4 + list(o_shapes),
        compiler_params=_cparams(("parallel",)),
        name="decode_mixers",
    )(p, *st_arrays, lp["rwkv_mu"], lp["rwkv_vec"], lp["rwkv_w_up"], lp["rwkv_a_up"], lp["rwkv_g_up"],
      lp["gate_par"], lp["ml_norm"], lp["hg_lb"], lp["hg_norm"], lp["gdn_conv_w"], lp["gdn_norm"])
    ys = list(res[:4])
    n_sh, n_wkv, n_mc, n_mn, n_mm, n_hg, n_cv, n_gd = res[4:]
    return ys, (n_sh.reshape(n, 1, RWKV_PROJ), n_wkv, n_mc, n_mn, n_mm, n_hg,
                n_cv.reshape(n, CONV_W - 1, 3 * GW), n_gd)


def _layer_params(l, a):
    gate_par = jnp.zeros((8, 128), F32)
    gate_par = gate_par.at[0, COL_ML_I:COL_ML_I + 4].set(a["ml_i_bias"][l])
    gate_par = gate_par.at[0, COL_ML_F:COL_ML_F + 4].set(a["ml_f_bias"][l])
    gate_par = gate_par.at[0, COL_GDN_A:COL_GDN_A + 4].set(a["gdn_dt_bias"][l])
    gate_par = gate_par.at[1, COL_GDN_A:COL_GDN_A + 4].set(a["gdn_a_log"][l])
    vec = jnp.stack([a[k][l] for k in ("rwkv_w0", "rwkv_a0", "rwkv_k_k", "rwkv_k_a", "rwkv_r_k",
                                        "rwkv_ln_w", "rwkv_ln_b")] + [jnp.zeros((GW,), F32)])
    return {
        "rwkv_mu": a["rwkv_mu"][l].reshape(1, RWKV_PROJ), "rwkv_vec": vec,
        "rwkv_w_up": a["rwkv_w_up"][l], "rwkv_a_up": a["rwkv_a_up"][l], "rwkv_g_up": a["rwkv_g_up"][l],
        "gate_par": gate_par, "ml_norm": a["ml_norm"][l].reshape(1, GW),
        "hg_lb": a["hg_lbs"][l].reshape(1, GW), "hg_norm": a["hg_norm"][l].reshape(1, GW),
        "gdn_conv_w": a["gdn_conv_w"][l], "gdn_norm": a["gdn_norm"][l].reshape(1, GW),
    }


def _pad_w_in(w_in):
    z = jnp.zeros(w_in.shape[:2] + (GATES_W - 16,), w_in.dtype)
    parts = [w_in[..., 0:1792], w_in[..., 3840:3848], w_in[..., 7944:7952], z,
             w_in[..., 1792:3840], w_in[..., 3848:5896], w_in[..., 5896:7944]]
    return jnp.concatenate(parts, axis=-1).astype(BF16)


def kernel(x_prompt, x_sample, state_rwkv_shift, state_rwkv_wkv, state_mlstm_c, state_mlstm_n,
           state_mlstm_m, state_hgrn, state_gdn_conv, state_gdn, c_prompt, c_sample,
           ada_w, ada_b, norm1, norm2, norm_f, w_in, w_out,
           rwkv_mu, rwkv_w0, rwkv_w_up, rwkv_a0, rwkv_a_up, rwkv_g_up, rwkv_k_k, rwkv_k_a, rwkv_r_k,
           rwkv_ln_w, rwkv_ln_b, ml_i_bias, ml_f_bias, ml_norm, hg_lb, hg_norm,
           gdn_conv_w, gdn_a_log, gdn_dt_bias, gdn_norm,
           moe_w_group, moe_b_group, moe_w_router, moe_b_router, moe_w_gate, moe_w_up, moe_w_down):
    bp, t, _ = x_prompt.shape
    ns = x_sample.shape[0]
    lbs = jax.nn.softmax(hg_lb.astype(F32), axis=0)
    hg_lbs = jnp.cumsum(lbs, axis=0) - lbs[0]
    a = dict(rwkv_mu=rwkv_mu, rwkv_w0=rwkv_w0, rwkv_w_up=rwkv_w_up, rwkv_a0=rwkv_a0, rwkv_a_up=rwkv_a_up,
             rwkv_g_up=rwkv_g_up, rwkv_k_k=rwkv_k_k, rwkv_k_a=rwkv_k_a, rwkv_r_k=rwkv_r_k,
             rwkv_ln_w=rwkv_ln_w, rwkv_ln_b=rwkv_ln_b, ml_i_bias=ml_i_bias, ml_f_bias=ml_f_bias,
             ml_norm=ml_norm, hg_lbs=hg_lbs, hg_norm=hg_norm, gdn_conv_w=gdn_conv_w, gdn_a_log=gdn_a_log,
             gdn_dt_bias=gdn_dt_bias, gdn_norm=gdn_norm)
    w_in_p = _pad_w_in(w_in)
    w_out_b = w_out.astype(BF16)
    w_route = jnp.concatenate([moe_w_group, moe_w_router,
                               jnp.zeros((DEPTH, D_MODEL, 128 - N_GROUPS - N_EXPERTS), F32)], axis=-1)
    w_route_hi = w_route.astype(BF16)
    w_route_lo = (w_route - w_route_hi.astype(F32)).astype(BF16)
    w_route = jnp.concatenate([w_route_hi, w_route_lo], axis=-1)
    b_route = jnp.concatenate([moe_b_group, moe_b_router,
                               jnp.zeros((DEPTH, 128 - N_GROUPS - N_EXPERTS), F32)], axis=-1)

    mod = _ada(jnp.concatenate([c_prompt, c_sample], axis=0), ada_w, ada_b)
    sample_states = (state_rwkv_shift, state_rwkv_wkv, state_mlstm_c, state_mlstm_n, state_mlstm_m,
                     state_hgrn, state_gdn_conv, state_gdn)

    xp = x_prompt.reshape(bp * t, D_MODEL)
    xs = x_sample.reshape(ns, D_MODEL)
    n_p = bp * t
    ff = None
    modp = mods = None
    new_p, new_s = [], []
    for l in range(DEPTH):
        lp = _layer_params(l, a)
        prev_modp, prev_mods = modp, mods
        modp = mod[l, :bp].reshape(bp, 1, 6 * D_MODEL)
        mods = mod[l, bp:].reshape(1, ns, 6 * D_MODEL)

        xp, pp = _in_proj(xp, modp, t, norm1[l], w_in_p[l], ff, 0, prev_modp)
        xs, ps = _in_proj(xs, mods, ns, norm1[l], w_in_p[l], ff, n_p, prev_mods)
        ysp, stp = _prompt_mixers(pp.reshape(bp, t, PROJ_PAD), lp)
        yss, sts = _sample_mixers(ps, sample_states, l, lp)
        new_p.append(stp)
        new_s.append(sts)
        xp, xs, h2, rt = _out_proj(ysp, xp, modp, t, yss, xs, mods, norm2[l], w_out_b[l],
                                   w_route[l], b_route[l].reshape(1, 128))
        ff = _moe(h2, rt, n_p + ns, moe_w_gate, moe_w_up, moe_w_down, l)
    yp = _final(xp, ff, 0, modp, t, norm_f).reshape(bp, t, D_MODEL)
    ys = _final(xs, ff, n_p, mods, ns, norm_f).reshape(ns, 1, D_MODEL)
    stack = lambda lst: tuple(jnp.stack([st[i] for st in lst]) for i in range(8))
    return (yp, ys) + stack(new_p) + stack(new_s)
```

```python
import functools
import math

import jax
import jax.numpy as jnp
from jax import lax
from jax.experimental import pallas as pl
from jax.experimental.pallas import tpu as pltpu

F32 = jnp.float32
BF16 = jnp.bfloat16
HI = lax.Precision.HIGHEST

D_MODEL = 2048
DEPTH = 4
GW = 512
RWKV_HD = 64
RWKV_H = 8
RWKV_PROJ = 1792
RWKV_LN_EPS = 64e-5
ML_H = 4
HG_H = 4
GDN_H = 4
HD = 128
CONV_W = 4
N_GROUPS = 4
EPG = 8
N_EXPERTS = 32
D_FF = 256
NORM_EPS = 1e-6

PROJ_PAD = 8192
GATES_OFF = 1792
GATES_W = 256
COL_ML_I, COL_ML_F, COL_GDN_B, COL_GDN_A = 0, 4, 8, 12

CHUNK = 64
SUB = 16
VMEM_LIMIT = 56 * 1024 * 1024


def _cparams(sem):
    return pltpu.CompilerParams(dimension_semantics=sem, vmem_limit_bytes=VMEM_LIMIT)


def _mm(a, b):
    return jnp.dot(a.astype(BF16), b.astype(BF16), preferred_element_type=F32)


def _mm_nt(a, b):
    return lax.dot_general(a.astype(BF16), b.astype(BF16), (((1,), (1,)), ((), ())),
                           preferred_element_type=F32)


def _mm_tn(a, b):
    return lax.dot_general(a.astype(BF16), b.astype(BF16), (((0,), (0,)), ((), ())),
                           preferred_element_type=F32)


def _mm_hi(a, b):
    return jnp.dot(a, b, precision=HI, preferred_element_type=F32)


def _mm_nt_hi(a, b):
    return lax.dot_general(a, b, (((1,), (1,)), ((), ())), precision=HI, preferred_element_type=F32)


def _sigmoid(x):
    return 1.0 / (1.0 + jnp.exp(-x))


def _silu(x):
    return x * _sigmoid(x)


def _softplus(x):
    return jnp.maximum(x, 0.0) + jnp.log(1.0 + jnp.exp(-jnp.abs(x)))


def _iota(shape, dim):
    return lax.broadcasted_iota(jnp.int32, shape, dim)


def _eye(n, m):
    return (_iota((n, m), 0) == _iota((n, m), 1)).astype(F32)


def _lower(n, inclusive):
    r, c = _iota((n, n), 0), _iota((n, n), 1)
    return (c <= r) if inclusive else (c < r)


def _rms_rows(x):
    return x * lax.rsqrt(jnp.mean(x * x, axis=-1, keepdims=True) + NORM_EPS)


def _tri_inv_minus_eye(n_mat, size):
    m = -n_mat
    steps = int(math.log2(size))
    pw = _mm(n_mat, n_mat)
    yield
    for i in range(1, steps):
        t = _mm(m, pw)
        nxt = _mm(pw, pw) if i < steps - 1 else None
        yield
        m = m + pw + t
        pw = nxt
    return m


def _round_robin(gens):
    gens = list(gens)
    results = [None] * len(gens)
    active = list(range(len(gens)))
    while active:
        for i in list(active):
            try:
                next(gens[i])
            except StopIteration as stop:
                results[i] = stop.value
                active.remove(i)
    return results


def _ada_kernel(c_ref, w_ref, b_ref, o_ref):
    c = c_ref[...]
    o_ref[...] = _mm(_silu(c), w_ref[...]) + b_ref[...]


def _ada(c_all, ada_w, ada_b):
    depth, d, n6 = ada_w.shape
    rows = c_all.shape[0]
    tn = 1024
    return pl.pallas_call(
        _ada_kernel,
        grid=(depth, n6 // tn),
        in_specs=[pl.BlockSpec((rows, d), lambda l, j: (0, 0)),
                  pl.BlockSpec((None, d, tn), lambda l, j: (l, 0, j)),
                  pl.BlockSpec((None, 1, tn), lambda l, j: (l, 0, j))],
        out_specs=pl.BlockSpec((None, rows, tn), lambda l, j: (l, 0, j)),
        out_shape=jax.ShapeDtypeStruct((depth, rows, n6), F32),
        compiler_params=_cparams(("parallel", "parallel")),
        name="ada",
    )(c_all, ada_w, ada_b.reshape(depth, 1, n6))


def _modulated_norm(x, nw, sc, sh):
    return _rms_rows(x) * nw * (1.0 + sc) + sh


def _inproj_kernel(*refs, combine):
    if combine:
        x_ref, ff_ref, gt_ref, sh_ref, sc_ref, nw_ref, w_ref, xo_ref, p_ref, h_ref = refs
    else:
        x_ref, sh_ref, sc_ref, nw_ref, w_ref, p_ref, h_ref = refs

    @pl.when(pl.program_id(1) == 0)
    def _():
        x = x_ref[...]
        if combine:
            rows = x.shape[0]
            x = x + gt_ref[...] * (_load_slabs(ff_ref.at[0], rows) + _load_slabs(ff_ref.at[1], rows))
            xo_ref[...] = x
        h_ref[...] = _modulated_norm(x, nw_ref[...], sc_ref[...], sh_ref[...]).astype(BF16)

    p_ref[...] = jnp.dot(h_ref[...], w_ref[...], preferred_element_type=F32)


def _mod_spec(mod3, which, tm, rows_per_group):
    r = mod3.shape[1]
    return pl.BlockSpec((None, r, D_MODEL),
                        lambda i, *_: ((i * tm) // rows_per_group, 0, which))


def _in_proj(x, mod3, rows_per_group, nw, w, ff=None, ff_row0=0, gate_mod3=None):
    n = x.shape[0]
    tm = min(512, rows_per_group)
    tn = 1024
    combine = ff is not None
    row = lambda i, j: (i, 0)
    in_specs = [pl.BlockSpec((tm, D_MODEL), row)]
    args = [x]
    if combine:
        blk0 = ff_row0 // tm
        in_specs += [pl.BlockSpec((2, tm * SLAB, 128), lambda i, j: (0, blk0 + i, 0)),
                     _mod_spec(gate_mod3, 5, tm, rows_per_group)]
        args += [ff, gate_mod3]
    in_specs += [_mod_spec(mod3, 0, tm, rows_per_group), _mod_spec(mod3, 1, tm, rows_per_group),
                 pl.BlockSpec((1, D_MODEL), lambda i, j: (0, 0)),
                 pl.BlockSpec((D_MODEL, tn), lambda i, j: (0, j))]
    args += [mod3, mod3, nw.reshape(1, D_MODEL), w]
    out_specs = [pl.BlockSpec((tm, tn), lambda i, j: (i, j))]
    out_shape = [jax.ShapeDtypeStruct((n, PROJ_PAD), F32)]
    if combine:
        out_specs = [pl.BlockSpec((tm, D_MODEL), row)] + out_specs
        out_shape = [jax.ShapeDtypeStruct((n, D_MODEL), F32)] + out_shape
    res = pl.pallas_call(
        functools.partial(_inproj_kernel, combine=combine),
        grid=(n // tm, PROJ_PAD // tn),
        in_specs=in_specs, out_specs=out_specs, out_shape=out_shape,
        scratch_shapes=[pltpu.VMEM((tm, D_MODEL), BF16)],
        compiler_params=_cparams(("parallel", "arbitrary")),
        name="in_proj",
    )(*args)
    return (res[0], res[1]) if combine else (x, res[0])


def _lane_max(x):
    return jnp.max(x, axis=-1, keepdims=True)


def _lane_sum(x):
    return jnp.sum(x, axis=-1, keepdims=True)


def _first_lane_of(mask, lane):
    return jnp.min(jnp.where(mask, lane, 4096), axis=-1, keepdims=True)


def _route(logits, biased):
    lane = _iota(logits.shape, 1)
    neg = jnp.float32(-jnp.inf)
    is_g = lane < N_GROUPS
    gmax = _lane_max(jnp.where(is_g, logits, neg))
    gexp = jnp.where(is_g, jnp.exp(logits - gmax), 0.0)
    gb = jnp.where(is_g, biased, neg)
    gsel = _first_lane_of(gb == _lane_max(gb), lane)
    pg = _lane_sum(jnp.where(lane == gsel, gexp, 0.0)) / _lane_sum(gexp)
    lo = N_GROUPS + gsel * EPG
    eb = jnp.where((lane >= lo) & (lane < lo + EPG), biased, neg)
    i1 = _first_lane_of(eb == _lane_max(eb), lane)
    eb2 = jnp.where(lane == i1, neg, eb)
    i2 = _first_lane_of(eb2 == _lane_max(eb2), lane)
    l1 = _lane_sum(jnp.where(lane == i1, logits, 0.0))
    l2 = _lane_sum(jnp.where(lane == i2, logits, 0.0))
    mx = jnp.maximum(l1, l2)
    e1, e2 = jnp.exp(l1 - mx), jnp.exp(l2 - mx)
    cw1, cw2 = pg * e1 / (e1 + e2), pg * e2 / (e1 + e2)
    id1, id2 = (i1 - N_GROUPS).astype(F32), (i2 - N_GROUPS).astype(F32)
    return jnp.where(lane == 0, id1, jnp.where(lane == 1, id2,
                     jnp.where(lane == 2, cw1, jnp.where(lane == 3, cw2, 0.0))))


SLAB = D_MODEL // 128


def _load_slabs(ref, rows):
    return jnp.concatenate([ref[pl.ds(s, rows, stride=SLAB), :] for s in range(SLAB)], axis=1)


def _store_slabs(ref, x, rows):
    for s in range(SLAB):
        ref[pl.ds(s, rows, stride=SLAB), :] = x[:, s * 128:(s + 1) * 128]


def _outproj_kernel(*refs, n_prompt_tiles, tm, rows_s):
    group_p, group_s = refs[0:8], refs[8:16]
    nw_ref, wo_ref, wr_ref, br_ref, xop_ref, xos_ref, h2_ref, rt_ref = refs[16:]
    i = pl.program_id(0)

    def body(group, xo_ref, rows):
        yr_ref, ym_ref, yh_ref, yg_ref, x_ref, gt_ref, sh_ref, sc_ref = group
        mix = jnp.dot(yr_ref[...], wo_ref[0:GW, :], preferred_element_type=F32)
        mix += jnp.dot(ym_ref[...], wo_ref[GW:2 * GW, :], preferred_element_type=F32)
        mix += jnp.dot(yh_ref[...], wo_ref[2 * GW:3 * GW, :], preferred_element_type=F32)
        mix += jnp.dot(yg_ref[...], wo_ref[3 * GW:4 * GW, :], preferred_element_type=F32)
        x = x_ref[...] + gt_ref[...] * mix
        xo_ref[...] = x
        h2 = _modulated_norm(x, nw_ref[...], sc_ref[...], sh_ref[...])
        h_hi = h2.astype(BF16)
        h_lo = (h2 - h_hi.astype(F32)).astype(BF16)
        hw = jnp.dot(h_hi, wr_ref[...], preferred_element_type=F32)
        logits = hw[:, 0:128] + hw[:, 128:256] + jnp.dot(h_lo, wr_ref[:, 0:128], preferred_element_type=F32)
        route = _route(logits, logits + br_ref[...])
        _store_slabs(h2_ref, h2, rows)
        rt_ref[0:rows, :] = route
        if rows < tm:
            h2_ref[rows * SLAB:tm * SLAB, :] = jnp.zeros(((tm - rows) * SLAB, 128), F32)
            rt_ref[rows:tm, :] = jnp.zeros((tm - rows, 128), F32)

    @pl.when(i < n_prompt_tiles)
    def _():
        body(group_p, xop_ref, tm)

    @pl.when(i == n_prompt_tiles)
    def _():
        body(group_s, xos_ref, rows_s)


def _out_proj(ys_p, x_p, mod_p, t, ys_s, x_s, mod_s, nw, wo, w_route, b_route):
    n_p, n_s = x_p.shape[0], x_s.shape[0]
    tm = min(256, t)
    assert n_s <= tm and n_p % tm == 0
    n_pt = n_p // tm
    prow = lambda i: (jnp.minimum(i, n_pt - 1), 0)
    const = lambda i: (0, 0)
    pmod = lambda which: pl.BlockSpec((None, 1, D_MODEL),
                                      lambda i: ((jnp.minimum(i, n_pt - 1) * tm) // t, 0, which))
    smod = lambda which: pl.BlockSpec((None, n_s, D_MODEL), lambda i: (0, 0, which))
    in_specs = ([pl.BlockSpec((tm, GW), prow)] * 4 + [pl.BlockSpec((tm, D_MODEL), prow), pmod(2), pmod(3), pmod(4)]
                + [pl.BlockSpec((n_s, GW), const)] * 4 + [pl.BlockSpec((n_s, D_MODEL), const),
                                                          smod(2), smod(3), smod(4)]
                + [pl.BlockSpec((1, D_MODEL), const), pl.BlockSpec((D_MODEL, D_MODEL), const),
                   pl.BlockSpec((D_MODEL, 256), const), pl.BlockSpec((1, 128), const)])
    n_all = n_p + tm
    return pl.pallas_call(
        functools.partial(_outproj_kernel, n_prompt_tiles=n_pt, tm=tm, rows_s=n_s),
        grid=(n_pt + 1,),
        in_specs=in_specs,
        out_specs=[pl.BlockSpec((tm, D_MODEL), prow), pl.BlockSpec((n_s, D_MODEL), const),
                   pl.BlockSpec((tm * SLAB, 128), lambda i: (i, 0)), pl.BlockSpec((tm, 128), lambda i: (i, 0))],
        out_shape=[jax.ShapeDtypeStruct((n_p, D_MODEL), F32), jax.ShapeDtypeStruct((n_s, D_MODEL), F32),
                   jax.ShapeDtypeStruct((n_all * SLAB, 128), F32), jax.ShapeDtypeStruct((n_all, 128), F32)],
        compiler_params=_cparams(("arbitrary",)),
        name="out_proj",
    )(*ys_p, x_p, mod_p, mod_p, mod_p, *ys_s, x_s, mod_s, mod_s, mod_s,
      nw.reshape(1, D_MODEL), wo, w_route, b_route)


def _moe_kernel(te_ref, tok_ref, dst_ref, h2_hbm, cw_ref, wg_ref, wu_ref, wd_ref, out_hbm,
                hbuf0, hbuf1, obuf0, obuf1, wgb, wub, wdb, gsem, ssem, *, tm, n_tiles):
    t = pl.program_id(0)
    hbufs, obufs = (hbuf0, hbuf1), (obuf0, obuf1)

    def start_gather(tile, slot):
        for r in range(tm):
            src = pl.multiple_of(tok_ref[tile * tm + r], SLAB)
            pltpu.make_async_copy(h2_hbm.at[pl.ds(src, SLAB)], hbufs[slot].at[pl.ds(r * SLAB, SLAB)],
                                  gsem.at[slot]).start()

    def start_scatter(tile, slot):
        for r in range(tm):
            dst = pl.multiple_of(dst_ref[tile * tm + r], SLAB)
            pltpu.make_async_copy(obufs[slot].at[pl.ds(r * SLAB, SLAB)], out_hbm.at[pl.ds(dst, SLAB)],
                                  ssem.at[slot]).start(priority=1)

    def wait_gather(slot):
        pltpu.make_async_copy(h2_hbm.at[pl.ds(0, tm * SLAB)], hbufs[slot], gsem.at[slot]).wait()

    def wait_scatter(slot):
        pltpu.make_async_copy(obufs[slot], out_hbm.at[pl.ds(0, tm * SLAB)], ssem.at[slot]).wait()

    @pl.when(t == 0)
    def _():
        start_gather(0, 0)

    @pl.when((t == 0) | (te_ref[t] != te_ref[jnp.maximum(t - 1, 0)]))
    def _():
        wgb[...] = wg_ref[...].astype(BF16)
        wub[...] = wu_ref[...].astype(BF16)
        wdb[...] = wd_ref[...].astype(BF16)

    def step(slot):
        wait_gather(slot)
        start_gather(t + 1, 1 - slot)
        h = _load_slabs(hbufs[slot], tm).astype(BF16)
        g = jnp.dot(h, wgb[...], preferred_element_type=F32)
        u = jnp.dot(h, wub[...], preferred_element_type=F32)
        hid = (_silu(g) * u * cw_ref[...]).astype(BF16)
        _store_slabs(obufs[slot], jnp.dot(hid, wdb[...], preferred_element_type=F32), tm)
        start_scatter(t, slot)

    for slot in (0, 1):
        @pl.when(t % 2 == slot)
        def _(slot=slot):
            step(slot)

        @pl.when((t >= 1) & (t % 2 == slot))
        def _(slot=slot):
            wait_scatter(1 - slot)

    @pl.when(t == n_tiles - 1)
    def _():
        wait_scatter((n_tiles - 1) % 2)
        wait_gather(n_tiles % 2)


MOE_TM = 256


def _moe(h2, route, n_tok, w_gate, w_up, w_down, layer):
    tm = MOE_TM
    slots = 2 * n_tok
    n_tiles = -(-(slots + N_EXPERTS * (tm - 1)) // tm) + 2
    n_pad = n_tok + tm
    n_rows = (n_tiles + 1) * tm
    eid = jnp.concatenate([route[:n_tok, 0], route[:n_tok, 1]]).astype(jnp.int32)
    cw = jnp.concatenate([route[:n_tok, 2], route[:n_tok, 3]])
    onehot = (eid[:, None] == jnp.arange(N_EXPERTS, dtype=jnp.int32)[None, :]).astype(jnp.int32)
    n_blk = -(-slots // tm)
    oh3 = jnp.pad(onehot, ((0, n_blk * tm - slots), (0, 0))).reshape(n_blk, tm, N_EXPERTS).astype(BF16)
    tri = (jnp.arange(tm)[:, None] >= jnp.arange(tm)[None, :]).astype(BF16)
    within = jnp.einsum("ij,bjk->bik", tri, oh3, preferred_element_type=F32).astype(jnp.int32)
    blk_tot = within[:, -1, :]
    blk_off = jnp.cumsum(blk_tot, axis=0) - blk_tot
    running = (within + blk_off[:, None, :]).reshape(n_blk * tm, N_EXPERTS)[:slots]
    counts = jnp.sum(blk_tot, axis=0)
    tiles_per = (counts + tm - 1) // tm
    tile_end = jnp.cumsum(tiles_per)
    tile_start = tile_end - tiles_per
    rank = jnp.sum(running * onehot, axis=1) - 1
    pos = jnp.sum(onehot * tile_start[None, :], axis=1) * tm + rank
    s_idx = jnp.arange(slots, dtype=jnp.int32)
    tok = s_idx % n_tok
    dst = (s_idx // n_tok) * n_pad + tok
    j = jnp.arange(n_rows, dtype=jnp.int32)
    dump = ((j // tm) % 2) * n_pad + n_tok + j % tm
    default = jnp.stack([jnp.zeros_like(j), dump, jnp.zeros_like(j)], axis=1).astype(F32)
    packed = jnp.stack([tok.astype(F32), dst.astype(F32), cw], axis=1)
    rows = default.at[pos].set(packed)
    row_tok = rows[:, 0].astype(jnp.int32) * SLAB
    row_dst = rows[:, 1].astype(jnp.int32) * SLAB
    row_cw = rows[:, 2].reshape(n_rows, 1)
    tile_ids = jnp.arange(n_tiles, dtype=jnp.int32)
    tile_e = jnp.sum((tile_end[None, :] <= tile_ids[:, None]).astype(jnp.int32), axis=1)
    tile_e = jnp.minimum(tile_e, N_EXPERTS - 1)

    wspec = lambda shape: pl.BlockSpec((None, None) + shape, lambda t, te, *_: (layer, te[t], 0, 0))
    grid_spec = pltpu.PrefetchScalarGridSpec(
        num_scalar_prefetch=3,
        grid=(n_tiles,),
        in_specs=[pl.BlockSpec(memory_space=pl.ANY),
                  pl.BlockSpec((tm, 1), lambda t, *_: (t, 0)),
                  wspec((D_MODEL, D_FF)), wspec((D_MODEL, D_FF)), wspec((D_FF, D_MODEL))],
        out_specs=pl.BlockSpec(memory_space=pl.ANY),
        scratch_shapes=[pltpu.VMEM((tm * SLAB, 128), F32)] * 4
        + [pltpu.VMEM((D_MODEL, D_FF), BF16), pltpu.VMEM((D_MODEL, D_FF), BF16),
           pltpu.VMEM((D_FF, D_MODEL), BF16),
           pltpu.SemaphoreType.DMA((2,)), pltpu.SemaphoreType.DMA((2,))])
    out = pl.pallas_call(
        functools.partial(_moe_kernel, tm=tm, n_tiles=n_tiles),
        grid_spec=grid_spec,
        out_shape=jax.ShapeDtypeStruct((2 * n_pad * SLAB, 128), F32),
        compiler_params=_cparams(("arbitrary",)),
        name="moe",
    )(tile_e, row_tok, row_dst, h2, row_cw, w_gate, w_up, w_down)
    return out.reshape(2, n_pad * SLAB, 128)


def _final_kernel(x_ref, ff_ref, gt_ref, nw_ref, o_ref):
    rows = x_ref.shape[0]
    x = x_ref[...] + gt_ref[...] * (_load_slabs(ff_ref.at[0], rows) + _load_slabs(ff_ref.at[1], rows))
    o_ref[...] = _rms_rows(x) * nw_ref[...]


def _final(x, ff, ff_row0, mod3, rows_per_group, nw):
    n = x.shape[0]
    tm = min(512, rows_per_group)
    blk0 = ff_row0 // tm
    return pl.pallas_call(
        _final_kernel,
        grid=(n // tm,),
        in_specs=[pl.BlockSpec((tm, D_MODEL), lambda i: (i, 0)),
                  pl.BlockSpec((2, tm * SLAB, 128), lambda i: (0, blk0 + i, 0)),
                  _mod_spec(mod3, 5, tm, rows_per_group),
                  pl.BlockSpec((1, D_MODEL), lambda i: (0, 0))],
        out_specs=pl.BlockSpec((tm, D_MODEL), lambda i: (i, 0)),
        out_shape=jax.ShapeDtypeStruct((n, D_MODEL), F32),
        compiler_params=_cparams(("parallel",)),
        name="final_norm",
    )(x, ff, mod3, nw.reshape(1, D_MODEL))


def _seg_sum(x, width):
    seg = (_iota((128, 128), 0) // width == _iota((128, 128), 1) // width).astype(F32)
    parts = [_mm_hi(x[:, j:j + 128], seg) for j in range(0, x.shape[1], 128)]
    return parts[0] if len(parts) == 1 else jnp.concatenate(parts, axis=1)


def _head_norm(o, nw, gate):
    return o * lax.rsqrt(_seg_sum(o * o, HD) * (1.0 / HD) + NORM_EPS) * nw * gate


def _carry_rows(ext_ref, cur, first, L):
    @pl.when(first)
    def _():
        ext_ref[0:8, :] = jnp.zeros((8, ext_ref.shape[1]), F32)

    ext_ref[8:8 + L, :] = cur


def _rwkv_parts(p_ref, mu_ref, vec_ref, wup_ref, aup_ref, gup_ref, y_ref, sh_ref, s_ref, ext_ref, L):
    c = pl.program_id(1)
    p = p_ref[...]
    _carry_rows(ext_ref, p, c == 0, L)
    prev = ext_ref[7:7 + L, :]
    xs = p + (prev - p) * mu_ref[...]
    ext_ref[0:8, :] = ext_ref[L:L + 8, :]
    sh_ref[...] = p[L - 1:L, :]

    @pl.when(c == 0)
    def _():
        s_ref[...] = jnp.zeros_like(s_ref)

    w0, a0, k_k, k_a, r_k, ln_w, ln_b = [vec_ref[i:i + 1, :] for i in range(7)]
    r, k, v = xs[:, 0:GW], xs[:, GW:2 * GW], xs[:, 2 * GW:3 * GW]
    dw, da, dg = xs[:, 1536:1600], xs[:, 1600:1664], xs[:, 1664:1792]
    logw = -_softplus(-(w0 + _mm(jnp.tanh(dw), wup_ref[...]))) - 0.5
    lw = -jnp.exp(logw)
    a = _sigmoid(a0 + _mm(da, aup_ref[...]))
    g = _mm(_sigmoid(dg), gup_ref[...])
    kk = k * k_k
    kk = kk * lax.rsqrt(jnp.maximum(_seg_sum(kk * kk, RWKV_HD), 1e-12))
    k2 = k * (1.0 + (a - 1.0) * k_a)
    cl = _mm_hi(_lower(L, True).astype(F32), lw)
    cl_last = cl[L - 1:L, :]
    at = -kk * jnp.exp(cl - lw)
    bt = kk * a * jnp.exp(-cl)
    kt = k2 * jnp.exp(-cl)
    rt = r * jnp.exp(cl)
    b_end = kk * a * jnp.exp(cl_last - cl)
    k_end = k2 * jnp.exp(cl_last - cl)
    strict, incl = _lower(L, False), _lower(L, True)
    def head(h):
        sl = slice(h * RWKV_HD, (h + 1) * RWKV_HD)
        s0 = s_ref[h]
        ath, bth, kth, rth, vh = at[:, sl], bt[:, sl], kt[:, sl], rt[:, sl], v[:, sl]
        g_ab, g_ak, g_rb, g_rk = _mm_nt(ath, bth), _mm_nt(ath, kth), _mm_nt(rth, bth), _mm_nt(rth, kth)
        a_s0, r_s0 = _mm_nt(ath, s0), _mm_nt(rth, s0)
        s_v = _mm_tn(vh, k_end[:, sl])
        yield
        n_ab = jnp.where(strict, g_ab, 0.0)
        ak_v = _mm(jnp.where(strict, g_ak, 0.0), vh)
        rk_v = _mm(jnp.where(incl, g_rk, 0.0), vh)
        m = yield from _tri_inv_minus_eye(-n_ab, L)
        rhs = a_s0 + ak_v
        m_rhs = _mm(m, rhs)
        yield
        u = rhs + m_rhs
        rb_u = _mm(jnp.where(incl, g_rb, 0.0), u)
        s_u = _mm_tn(u, b_end[:, sl])
        yield
        s_ref[h] = s0 * jnp.exp(cl_last[:, sl]) + s_u + s_v
        return r_s0 + rb_u + rk_v

    def finish(ys):
        y = jnp.concatenate(ys, axis=1)
        mean = _seg_sum(y, RWKV_HD) * (1.0 / RWKV_HD)
        dev = y - mean
        var = _seg_sum(dev * dev, RWKV_HD) * (1.0 / RWKV_HD)
        y = dev * lax.rsqrt(var + RWKV_LN_EPS) * ln_w + ln_b
        bonus = _seg_sum(r * k2 * r_k, RWKV_HD) * v
        y_ref[...] = ((y + bonus) * g).astype(y_ref.dtype)

    return [head(h) for h in range(RWKV_H)], finish


def _gate_forms(g_ref, gp_ref, L):
    raw = g_ref[:, 0:128] + gp_ref[0:1, :]
    lane = _iota(raw.shape, 1)
    lf = jnp.minimum(raw, 0.0) - jnp.log(1.0 + jnp.exp(-jnp.abs(raw)))
    beta = _sigmoid(raw)
    decay = -jnp.exp(gp_ref[1:2, :]) * _softplus(raw)
    cols = jnp.where(lane < 4, raw, jnp.where(lane < 8, lf, jnp.where(lane < 12, beta, decay)))
    rows = _mm_nt_hi(_eye(16, 128), cols)
    ccols = _mm_hi(_lower(L, True).astype(F32), cols)
    upper = (_iota((L, L), 0) <= _iota((L, L), 1)).astype(F32)
    crows = _mm_hi(rows, upper)
    return cols, rows, ccols, crows


def _mlstm_parts(p_ref, gate_forms, nw_ref, y_ref, c_ref, n_ref, m_ref, L):
    c = pl.program_id(1)

    @pl.when(c == 0)
    def _():
        c_ref[...] = jnp.zeros_like(c_ref)
        n_ref[...] = jnp.zeros_like(n_ref)
        m_ref[...] = jnp.zeros_like(m_ref)

    cols, rows, ccols, crows = gate_forms
    causal = _lower(L, True)
    neg = jnp.float32(-jnp.inf)
    def head(h):
        q = p_ref[:, h * HD:(h + 1) * HD]
        k = p_ref[:, GW + h * HD:GW + (h + 1) * HD] * (HD ** -0.5)
        v = p_ref[:, 2 * GW + h * HD:2 * GW + (h + 1) * HD]
        ig_col, ig_row = cols[:, h:h + 1], rows[h:h + 1, :]
        b_col, b_row = ccols[:, 4 + h:5 + h], crows[4 + h:5 + h, :]
        m_prev = m_ref[:, h:h + 1]
        cmat, nvec = c_ref[h], n_ref[h:h + 1, :]
        qk, qc = _mm_nt(q, k), _mm_nt(q, cmat)
        dmat = jnp.where(causal, b_col - b_row + ig_row, neg)
        gcol = b_col + m_prev
        mt = jnp.maximum(gcol, _lane_max(dmat))
        m_last = mt[L - 1:L, :]
        wk = jnp.exp(b_col[L - 1:L, :] - b_col + ig_col - m_last)
        decay = jnp.exp(gcol[L - 1:L, :] - m_last)
        c_upd = _mm_tn(v * wk, k)
        yield
        smat = qk * jnp.exp(dmat - mt)
        s_v = _mm(smat, v)
        wg = jnp.exp(gcol - mt)
        den = _lane_sum(smat) + wg * _lane_sum(q * nvec)
        c_ref[h] = decay * cmat + c_upd
        n_ref[h:h + 1, :] = decay * nvec + jnp.sum(k * wk, axis=0, keepdims=True)
        m_ref[:, h:h + 1] = m_last
        yield
        return (s_v + wg * qc) / jnp.maximum(jnp.abs(den), jnp.exp(-mt))

    def finish(ys):
        o = p_ref[:, 3 * GW:4 * GW]
        y_ref[...] = _head_norm(jnp.concatenate(ys, axis=1), nw_ref[...], _sigmoid(o)).astype(y_ref.dtype)

    return [head(h) for h in range(ML_H)], finish


def _hgrn_parts(p_ref, lb_ref, nw_ref, y_ref, s_ref, L):
    c = pl.program_id(1)

    @pl.when(c == 0)
    def _():
        s_ref[...] = jnp.zeros_like(s_ref)

    lb = lb_ref[...]
    qa = _silu(p_ref[:, 0:GW])
    fg = lb + (1.0 - lb) * _sigmoid(p_ref[:, GW:2 * GW])
    ka = 1.0 - fg
    va = p_ref[:, 2 * GW:3 * GW]
    cga = _mm_hi(_lower(L, True).astype(F32), jnp.log(fg))
    ones = jnp.ones((HD, HD), BF16)
    t3, s3 = _iota((SUB, SUB, HD), 0), _iota((SUB, SUB, HD), 1)
    neg = jnp.float32(-jnp.inf)
    def head(h):
        sl = slice(h * HD, (h + 1) * HD)
        q, k, v, cg = qa[:, sl], ka[:, sl], va[:, sl], cga[:, sl]
        st = s_ref[h]
        inter = _mm_nt(q * jnp.exp(cg), st)
        cg_last = cg[L - 1:L, :]
        s_upd = _mm_tn(v, k * jnp.exp(cg_last - cg))
        a3s, a_offs = [], []
        for i in range(L // SUB):
            lo = i * SUB
            qi, ki, cgi = q[lo:lo + SUB], k[lo:lo + SUB], cg[lo:lo + SUB]
            e3 = jnp.exp(jnp.where(s3 <= t3, cgi[:, None, :] - cgi[None, :, :], neg))
            x3 = qi[:, None, :] * ki[None, :, :] * e3
            a3s.append(jnp.dot(x3.reshape(SUB * SUB, HD).astype(BF16), ones, preferred_element_type=F32))
            if i > 0:
                ref = cg[lo - 1:lo, :]
                a_offs.append(_mm_nt(qi * jnp.exp(cgi - ref), k[0:lo] * jnp.exp(ref - cg[0:lo])))
        yield
        s_ref[h] = st * jnp.exp(cg_last) + s_upd
        offs = [_mm(a_off, v[0:(i + 1) * SUB]) for i, a_off in enumerate(a_offs)]
        yield
        rows_out = []
        for i in range(L // SUB):
            lo = i * SUB
            oi = inter[lo:lo + SUB] + jnp.sum(a3s[i].reshape(SUB, SUB, HD) * v[lo:lo + SUB][None, :, :], axis=1)
            rows_out.append(oi if i == 0 else oi + offs[i - 1])
        return jnp.concatenate(rows_out, axis=0)

    def finish(ys):
        g = p_ref[:, 3 * GW:4 * GW]
        y_ref[...] = _head_norm(jnp.concatenate(ys, axis=1), nw_ref[...], _silu(g)).astype(y_ref.dtype)

    return [head(h) for h in range(HG_H)], finish


def _gdn_parts(p_ref, gate_forms, cw_ref, nw_ref, y_ref, cv_ref, s_ref, ext_ref, L):
    c = pl.program_id(1)
    w3 = 3 * GW
    _carry_rows(ext_ref, p_ref[:, 0:w3], c == 0, L)
    conv = (cw_ref[3:4, :] * ext_ref[8:8 + L, :] + cw_ref[2:3, :] * ext_ref[7:7 + L, :]
            + cw_ref[1:2, :] * ext_ref[6:6 + L, :] + cw_ref[0:1, :] * ext_ref[5:5 + L, :])
    cv_ref[...] = ext_ref[L + 5:L + 8, :]
    ext_ref[0:8, :] = ext_ref[L:L + 8, :]

    @pl.when(c == 0)
    def _():
        s_ref[...] = jnp.zeros_like(s_ref)

    qkv = _silu(conv)
    cols, rows, ccols, crows = gate_forms
    strict, causal = _lower(L, False), _lower(L, True)
    neg = jnp.float32(-jnp.inf)
    qa = qkv[:, 0:GW]
    ka = qkv[:, GW:2 * GW]
    qa = qa * lax.rsqrt(jnp.maximum(_seg_sum(qa * qa, HD), 1e-12)) * (HD ** -0.5)
    ka = ka * lax.rsqrt(jnp.maximum(_seg_sum(ka * ka, HD), 1e-12))

    def head(h):
        sl = slice(h * HD, (h + 1) * HD)
        q, k, v = qa[:, sl], ka[:, sl], qkv[:, 2 * GW + h * HD:2 * GW + (h + 1) * HD]
        beta = cols[:, COL_GDN_B + h:COL_GDN_B + h + 1]
        cg_col = ccols[:, COL_GDN_A + h:COL_GDN_A + h + 1]
        cg_row = crows[COL_GDN_A + h:COL_GDN_A + h + 1, :]
        diff = cg_col - cg_row
        s0 = s_ref[h]
        kb = k * beta
        g_kk, g_qk = _mm_nt(kb, k), _mm_nt(q, k)
        q_s0 = _mm(q * jnp.exp(cg_col), s0)
        yield
        n_mat = g_kk * jnp.exp(jnp.where(strict, diff, neg))
        m = yield from _tri_inv_minus_eye(n_mat, L)
        rhs = jnp.concatenate([v * beta, kb * jnp.exp(cg_col)], axis=1)
        m_rhs = _mm(m, rhs)
        yield
        uw = rhs + m_rhs
        w_s0 = _mm(uw[:, HD:2 * HD], s0)
        yield
        u2 = uw[:, 0:HD] - w_s0
        qk = g_qk * jnp.exp(jnp.where(causal, diff, neg))
        qk_u = _mm(qk, u2)
        cg_last = cg_col[L - 1:L, :]
        s_upd = _mm_tn(k * jnp.exp(cg_last - cg_col), u2)
        yield
        s_ref[h] = jnp.exp(cg_last) * s0 + s_upd
        return q_s0 + qk_u

    def finish(ys):
        g = p_ref[:, w3:w3 + GW]
        y_ref[...] = _head_norm(jnp.concatenate(ys, axis=1), nw_ref[...], _silu(g)).astype(y_ref.dtype)

    return [head(h) for h in range(GDN_H)], finish


def _mixers_kernel(pr_ref, pg_ref, pm_ref, ph_ref, pd_ref,
                   mu_ref, vec_ref, wup_ref, aup_ref, gup_ref, gp_ref, mlw_ref, lb_ref, hgw_ref, cw_ref, gdw_ref,
                   yr_ref, sh_ref, wkv_ref, ym_ref, mc_ref, mn_ref, mm_ref, yh_ref, hg_ref, yg_ref, cv_ref, gd_ref,
                   ext_r, ext_g, *, L):
    gate_forms = _gate_forms(pg_ref, gp_ref, L)
    parts = [
        _rwkv_parts(pr_ref, mu_ref, vec_ref, wup_ref, aup_ref, gup_ref, yr_ref, sh_ref, wkv_ref, ext_r, L),
        _mlstm_parts(pm_ref, gate_forms, mlw_ref, ym_ref, mc_ref, mn_ref, mm_ref, L),
        _hgrn_parts(ph_ref, lb_ref, hgw_ref, yh_ref, hg_ref, L),
        _gdn_parts(pd_ref, gate_forms, cw_ref, gdw_ref, yg_ref, cv_ref, gd_ref, ext_g, L),
    ]
    results = _round_robin(g for gens, _ in parts for g in gens)
    at = 0
    for gens, finish in parts:
        finish(results[at:at + len(gens)])
        at += len(gens)


def _prompt_mixers(p3, lp):
    b, t, _ = p3.shape
    L = math.gcd(t, CHUNK)
    grid = (b, t // L)
    cp = _cparams(("parallel", "arbitrary"))
    col = lambda width, idx: pl.BlockSpec((None, L, width), lambda i, c: (i, c, idx))
    const2 = lambda shape: pl.BlockSpec(shape, lambda i, c: (0, 0))
    y_spec = pl.BlockSpec((None, L, GW), lambda i, c: (i, c, 0))
    y_shape = jax.ShapeDtypeStruct((b, t, GW), BF16)
    state = lambda *s: (pl.BlockSpec((None,) + s, lambda i, c: (i,) + (0,) * len(s)),
                        jax.ShapeDtypeStruct((b,) + s, F32))
    gates = col(GATES_W, GATES_OFF // GATES_W)
    y_out = (y_spec, y_shape)
    outs = [y_out, state(1, RWKV_PROJ), state(RWKV_H, RWKV_HD, RWKV_HD),
            y_out, state(ML_H, HD, HD), state(ML_H, HD), state(1, 128),
            y_out, state(HG_H, HD, HD),
            y_out, state(CONV_W - 1, 3 * GW), state(GDN_H, HD, HD)]
    specs, shapes = zip(*outs)
    yr, n_sh, n_wkv, ym, n_c, n_n, n_m, yh, n_hg, yg, n_cv, n_gd = pl.pallas_call(
        functools.partial(_mixers_kernel, L=L), grid=grid,
        in_specs=[col(RWKV_PROJ, 0), gates, col(4 * GW, 1), col(4 * GW, 2), col(4 * GW, 3),
                  const2((1, RWKV_PROJ)), const2((8, GW)), const2((64, GW)), const2((64, GW)), const2((128, GW)),
                  const2((8, 128)), const2((1, GW)), const2((1, GW)), const2((1, GW)),
                  const2((CONV_W, 3 * GW)), const2((1, GW))],
        out_specs=list(specs), out_shape=list(shapes),
        scratch_shapes=[pltpu.VMEM((L + 8, RWKV_PROJ), F32), pltpu.VMEM((L + 8, 3 * GW), F32)],
        compiler_params=cp, name="mixers",
    )(p3, p3, p3, p3, p3, lp["rwkv_mu"], lp["rwkv_vec"], lp["rwkv_w_up"], lp["rwkv_a_up"], lp["rwkv_g_up"],
      lp["gate_par"], lp["ml_norm"], lp["hg_lb"], lp["hg_norm"], lp["gdn_conv_w"], lp["gdn_norm"])

    ys = [y.reshape(b * t, GW) for y in (yr, ym, yh, yg)]
    states = (n_sh, n_wkv, n_c, n_n, n_m[:, 0, :ML_H], jnp.swapaxes(n_hg, -1, -2), n_cv, n_gd)
    return ys, states


DEC_BS = 8


def _col_to_row(col):
    n = col.shape[0]
    return jnp.sum(jnp.where(_iota((n, n), 0) == _iota((n, n), 1), col, 0.0), axis=0, keepdims=True)


def _sub_sum(x):
    return jnp.sum(x, axis=0, keepdims=True)


def _decode_kernel(p_ref, sh_ref, wkv_ref, mc_ref, mn_ref, mm_ref, hg_ref, cv_ref, gd_ref,
                   mu_ref, vec_ref, wup_ref, aup_ref, gup_ref, gp_ref, mlw_ref, lb_ref, hgw_ref,
                   cw_ref, gdw_ref,
                   yr_ref, ym_ref, yh_ref, yg_ref,
                   nsh_ref, nwkv_ref, nmc_ref, nmn_ref, nmm_ref, nhg_ref, ncv_ref, ngd_ref):
    bs = DEC_BS
    w3 = 3 * GW
    pr = p_ref[:, 0:RWKV_PROJ]
    xs = pr + (sh_ref[...] - pr) * mu_ref[...]
    nsh_ref[...] = pr
    w0, a0, k_k, k_a, r_k, ln_w, ln_b = [vec_ref[i:i + 1, :] for i in range(7)]
    r, k, v = xs[:, 0:GW], xs[:, GW:2 * GW], xs[:, 2 * GW:3 * GW]
    dw, da, dg = xs[:, 1536:1600], xs[:, 1600:1664], xs[:, 1664:1792]
    logw = -_softplus(-(w0 + _mm(jnp.tanh(dw), wup_ref[...]))) - 0.5
    wdec = jnp.exp(-jnp.exp(logw))
    a = _sigmoid(a0 + _mm(da, aup_ref[...]))
    g_r = _mm(_sigmoid(dg), gup_ref[...])
    kk = k * k_k
    kk = kk * lax.rsqrt(jnp.maximum(_seg_sum(kk * kk, RWKV_HD), 1e-12))
    k2 = k * (1.0 + (a - 1.0) * k_a)
    ra, rb = -kk, kk * a

    gates = p_ref[:, GATES_OFF:GATES_OFF + 128] + gp_ref[0:1, :]
    ml_off = GATES_OFF + GATES_W
    mq = p_ref[:, ml_off:ml_off + GW]
    mk = p_ref[:, ml_off + GW:ml_off + 2 * GW] * (HD ** -0.5)
    mv = p_ref[:, ml_off + 2 * GW:ml_off + 3 * GW]
    mo = p_ref[:, ml_off + 3 * GW:ml_off + 4 * GW]
    m_ig = gates
    m_lf = jnp.minimum(gates, 0.0) - jnp.log(1.0 + jnp.exp(-jnp.abs(gates)))

    hg_off = ml_off + 4 * GW
    lb = lb_ref[...]
    hq = _silu(p_ref[:, hg_off:hg_off + GW])
    hfg = lb + (1.0 - lb) * _sigmoid(p_ref[:, hg_off + GW:hg_off + 2 * GW])
    hv = p_ref[:, hg_off + 2 * GW:hg_off + 3 * GW]
    hgate = p_ref[:, hg_off + 3 * GW:hg_off + 4 * GW]

    gd_off = hg_off + 4 * GW
    cur = p_ref[:, gd_off:gd_off + w3]
    conv = (cw_ref[3:4, :] * cur + cw_ref[2:3, :] * cv_ref[:, 2 * w3:3 * w3]
            + cw_ref[1:2, :] * cv_ref[:, w3:2 * w3] + cw_ref[0:1, :] * cv_ref[:, 0:w3])
    ncv_ref[:, 0:2 * w3] = cv_ref[:, w3:3 * w3]
    ncv_ref[:, 2 * w3:3 * w3] = cur
    qkv = _silu(conv)
    gq, gk, gv = qkv[:, 0:GW], qkv[:, GW:2 * GW], qkv[:, 2 * GW:3 * GW]
    gq = gq * lax.rsqrt(jnp.maximum(_seg_sum(gq * gq, HD), 1e-12)) * (HD ** -0.5)
    gk = gk * lax.rsqrt(jnp.maximum(_seg_sum(gk * gk, HD), 1e-12))
    ggate = p_ref[:, gd_off + w3:gd_off + w3 + GW]
    g_beta = _sigmoid(gates)
    g_dec = jnp.exp(-jnp.exp(gp_ref[1:2, :]) * _softplus(gates))

    blocks = ([mv[:, h * HD:(h + 1) * HD] for h in range(ML_H)]
              + [hq[:, h * HD:(h + 1) * HD] for h in range(HG_H)]
              + [hfg[:, h * HD:(h + 1) * HD] for h in range(HG_H)]
              + [gq[:, h * HD:(h + 1) * HD] for h in range(GDN_H)]
              + [gk[:, h * HD:(h + 1) * HD] for h in range(GDN_H)]
              + [v[:, j * 128:(j + 1) * 128] for j in range(4)])
    xt = _mm_nt_hi(_eye(128, 128), jnp.concatenate(blocks, axis=0))
    colf = lambda blk, s: xt[:, blk * bs + s:blk * bs + s + 1]

    yr_rows, ym_rows, yh_rows, yg_rows = [], [], [], []
    for s in range(bs):
        row = lambda arr, lo, width: arr[s:s + 1, lo:lo + width]
        parts = []
        for h in range(RWKV_H):
            lo = h * RWKV_HD
            st = wkv_ref[s, h]
            v_col = colf(20 + h // 2, s)[(h % 2) * 64:(h % 2) * 64 + 64, :]
            sa = _lane_sum(st * row(ra, lo, 64))
            st = st * row(wdec, lo, 64) + sa * row(rb, lo, 64) + v_col * row(k2, lo, 64)
            nwkv_ref[s, h] = st
            parts.append(_col_to_row(_lane_sum(st * row(r, lo, 64))))
        yr_rows.append(jnp.concatenate(parts, axis=1))
        parts = []
        for h in range(ML_H):
            lo = h * HD
            q_r, k_r = row(mq, lo, HD), row(mk, lo, HD)
            ig, lf = m_ig[s:s + 1, COL_ML_I + h:COL_ML_I + h + 1], m_lf[s:s + 1, COL_ML_F + h:COL_ML_F + h + 1]
            cmat, nvec, m_prev = mc_ref[s, h], mn_ref[s, h:h + 1, :], mm_ref[s:s + 1, h:h + 1]
            gsc = lf + m_prev
            mt = jnp.maximum(gsc, ig)
            wi, wg = jnp.exp(ig - mt), jnp.exp(gsc - mt)
            sc = _lane_sum(q_r * k_r) * wi
            v_col = colf(h, s)
            num = sc * v_col + wg * _lane_sum(cmat * q_r)
            den = sc + wg * _lane_sum(nvec * q_r)
            parts.append(_col_to_row(num / jnp.maximum(jnp.abs(den), jnp.exp(-mt))))
            nmc_ref[s, h] = wg * cmat + (wi * v_col) * k_r
            nmn_ref[s, h:h + 1, :] = wg * nvec + wi * k_r
            nmm_ref[s:s + 1, h:h + 1] = mt
        ym_rows.append(jnp.concatenate(parts, axis=1))
        parts = []
        for h in range(HG_H):
            lo = h * HD
            st = hg_ref[s, h]
            q_col, fg_col = colf(4 + h, s), colf(8 + h, s)
            q_r, fg_r, v_r = row(hq, lo, HD), row(hfg, lo, HD), row(hv, lo, HD)
            parts.append(_sub_sum(st * (q_col * fg_col)) + _lane_sum(q_r * (1.0 - fg_r)) * v_r)
            nhg_ref[s, h] = fg_col * st + (1.0 - fg_col) * v_r
        yh_rows.append(jnp.concatenate(parts, axis=1))
        parts = []
        for h in range(GDN_H):
            lo = h * HD
            st = gd_ref[s, h]
            q_col, k_col = colf(12 + h, s), colf(16 + h, s)
            beta = g_beta[s:s + 1, COL_GDN_B + h:COL_GDN_B + h + 1]
            dec = g_dec[s:s + 1, COL_GDN_A + h:COL_GDN_A + h + 1]
            u2 = beta * row(gv, lo, HD) - _sub_sum(st * (k_col * (beta * dec)))
            qk = _lane_sum(row(gq, lo, HD) * row(gk, lo, HD))
            parts.append(dec * _sub_sum(st * q_col) + qk * u2)
            ngd_ref[s, h] = dec * st + k_col * u2
        yg_rows.append(jnp.concatenate(parts, axis=1))

    yr = jnp.concatenate(yr_rows, axis=0)
    mean = _seg_sum(yr, RWKV_HD) * (1.0 / RWKV_HD)
    dev = yr - mean
    var = _seg_sum(dev * dev, RWKV_HD) * (1.0 / RWKV_HD)
    yr = dev * lax.rsqrt(var + RWKV_LN_EPS) * ln_w + ln_b
    bonus = _seg_sum(r * k2 * r_k, RWKV_HD) * v
    yr_ref[...] = ((yr + bonus) * g_r).astype(yr_ref.dtype)
    ym_ref[...] = _head_norm(jnp.concatenate(ym_rows, axis=0), mlw_ref[...], _sigmoid(mo)).astype(ym_ref.dtype)
    yh_ref[...] = _head_norm(jnp.concatenate(yh_rows, axis=0), hgw_ref[...], _silu(hgate)).astype(yh_ref.dtype)
    yg_ref[...] = _head_norm(jnp.concatenate(yg_rows, axis=0), gdw_ref[...], _silu(ggate)).astype(yg_ref.dtype)


def _sample_mixers(p, states, layer, lp):
    n = p.shape[0]
    bs = DEC_BS
    sh, wkv, mc, mn, mm, hg, cv, gd = states
    sh2 = sh.reshape(DEPTH, n, RWKV_PROJ)
    cv = cv.reshape(DEPTH, n, (CONV_W - 1) * 3 * GW)

    def st_in(arr):
        tail = arr.shape[2:]
        return pl.BlockSpec((None, bs) + tail, lambda i: (layer, i) + (0,) * len(tail))

    def st_out(arr):
        tail = arr.shape[2:]
        return (pl.BlockSpec((bs,) + tail, lambda i: (i,) + (0,) * len(tail)),
                jax.ShapeDtypeStruct((n,) + tail, F32))

    const2 = lambda shape: pl.BlockSpec(shape, lambda i: (0, 0))
    st_arrays = (sh2, wkv, mc, mn, mm, hg, cv, gd)
    o_specs, o_shapes = zip(*[st_out(a) for a in st_arrays])
    y_spec = pl.BlockSpec((bs, GW), lambda i: (i, 0))
    y_shape = jax.ShapeDtypeStruct((n, GW), BF16)
    res = pl.pallas_call(
        _decode_kernel,
        grid=(n // bs,),
        in_specs=[pl.BlockSpec((bs, PROJ_PAD), lambda i: (i, 0))] + [st_in(a) for a in st_arrays]
        + [const2((1, RWKV_PROJ)), const2((8, GW)), const2((64, GW)), const2((64, GW)), const2((128, GW)),
           const2((8, 128)), const2((1, GW)), const2((1, GW)), const2((1, GW)),
           const2((CONV_W, 3 * GW)), const2((1, GW))],
        out_specs=[y_spec] * 4 + list(o_specs),
        out_shape=[y_shape] * 4 + list(o_shapes),
        compiler_params=_cparams(("parallel",)),
        name="decode_mixers",
    )(p, *st_arrays, lp["rwkv_mu"], lp["rwkv_vec"], lp["rwkv_w_up"], lp["rwkv_a_up"], lp["rwkv_g_up"],
      lp["gate_par"], lp["ml_norm"], lp["hg_lb"], lp["hg_norm"], lp["gdn_conv_w"], lp["gdn_norm"])
    ys = list(res[:4])
    n_sh, n_wkv, n_mc, n_mn, n_mm, n_hg, n_cv, n_gd = res[4:]
    return ys, (n_sh.reshape(n, 1, RWKV_PROJ), n_wkv, n_mc, n_mn, n_mm, n_hg,
                n_cv.reshape(n, CONV_W - 1, 3 * GW), n_gd)


def _layer_params(l, a):
    gate_par = jnp.zeros((8, 128), F32)
    gate_par = gate_par.at[0, COL_ML_I:COL_ML_I + 4].set(a["ml_i_bias"][l])
    gate_par = gate_par.at[0, COL_ML_F:COL_ML_F + 4].set(a["ml_f_bias"][l])
    gate_par = gate_par.at[0, COL_GDN_A:COL_GDN_A + 4].set(a["gdn_dt_bias"][l])
    gate_par = gate_par.at[1, COL_GDN_A:COL_GDN_A + 4].set(a["gdn_a_log"][l])
    vec = jnp.stack([a[k][l] for k in ("rwkv_w0", "rwkv_a0", "rwkv_k_k", "rwkv_k_a", "rwkv_r_k",
                                        "rwkv_ln_w", "rwkv_ln_b")] + [jnp.zeros((GW,), F32)])
    return {
        "rwkv_mu": a["rwkv_mu"][l].reshape(1, RWKV_PROJ), "rwkv_vec": vec,
        "rwkv_w_up": a["rwkv_w_up"][l], "rwkv_a_up": a["rwkv_a_up"][l], "rwkv_g_up": a["rwkv_g_up"][l],
        "gate_par": gate_par, "ml_norm": a["ml_norm"][l].reshape(1, GW),
        "hg_lb": a["hg_lbs"][l].reshape(1, GW), "hg_norm": a["hg_norm"][l].reshape(1, GW),
        "gdn_conv_w": a["gdn_conv_w"][l], "gdn_norm": a["gdn_norm"][l].reshape(1, GW),
    }


def _pad_w_in(w_in):
    z = jnp.zeros(w_in.shape[:2] + (GATES_W - 16,), w_in.dtype)
    parts = [w_in[..., 0:1792], w_in[..., 3840:3848], w_in[..., 7944:7952], z,
             w_in[..., 1792:3840], w_in[..., 3848:5896], w_in[..., 5896:7944]]
    return jnp.concatenate(parts, axis=-1).astype(BF16)


def kernel(x_prompt, x_sample, state_rwkv_shift, state_rwkv_wkv, state_mlstm_c, state_mlstm_n,
           state_mlstm_m, state_hgrn, state_gdn_conv, state_gdn, c_prompt, c_sample,
           ada_w, ada_b, norm1, norm2, norm_f, w_in, w_out,
           rwkv_mu, rwkv_w0, rwkv_w_up, rwkv_a0, rwkv_a_up, rwkv_g_up, rwkv_k_k, rwkv_k_a, rwkv_r_k,
           rwkv_ln_w, rwkv_ln_b, ml_i_bias, ml_f_bias, ml_norm, hg_lb, hg_norm,
           gdn_conv_w, gdn_a_log, gdn_dt_bias, gdn_norm,
           moe_w_group, moe_b_group, moe_w_router, moe_b_router, moe_w_gate, moe_w_up, moe_w_down):
    bp, t, _ = x_prompt.shape
    ns = x_sample.shape[0]
    lbs = jax.nn.softmax(hg_lb.astype(F32), axis=0)
    hg_lbs = jnp.cumsum(lbs, axis=0) - lbs[0]
    a = dict(rwkv_mu=rwkv_mu, rwkv_w0=rwkv_w0, rwkv_w_up=rwkv_w_up, rwkv_a0=rwkv_a0, rwkv_a_up=rwkv_a_up,
             rwkv_g_up=rwkv_g_up, rwkv_k_k=rwkv_k_k, rwkv_k_a=rwkv_k_a, rwkv_r_k=rwkv_r_k,
             rwkv_ln_w=rwkv_ln_w, rwkv_ln_b=rwkv_ln_b, ml_i_bias=ml_i_bias, ml_f_bias=ml_f_bias,
             ml_norm=ml_norm, hg_lbs=hg_lbs, hg_norm=hg_norm, gdn_conv_w=gdn_conv_w, gdn_a_log=gdn_a_log,
             gdn_dt_bias=gdn_dt_bias, gdn_norm=gdn_norm)
    w_in_p = _pad_w_in(w_in)
    w_out_b = w_out.astype(BF16)
    w_route = jnp.concatenate([moe_w_group, moe_w_router,
                               jnp.zeros((DEPTH, D_MODEL, 128 - N_GROUPS - N_EXPERTS), F32)], axis=-1)
    w_route_hi = w_route.astype(BF16)
    w_route_lo = (w_route - w_route_hi.astype(F32)).astype(BF16)
    w_route = jnp.concatenate([w_route_hi, w_route_lo], axis=-1)
    b_route = jnp.concatenate([moe_b_group, moe_b_router,
                               jnp.zeros((DEPTH, 128 - N_GROUPS - N_EXPERTS), F32)], axis=-1)

    mod = _ada(jnp.concatenate([c_prompt, c_sample], axis=0), ada_w, ada_b)
    sample_states = (state_rwkv_shift, state_rwkv_wkv, state_mlstm_c, state_mlstm_n, state_mlstm_m,
                     state_hgrn, state_gdn_conv, state_gdn)

    xp = x_prompt.reshape(bp * t, D_MODEL)
    xs = x_sample.reshape(ns, D_MODEL)
    n_p = bp * t
    ff = None
    modp = mods = None
    new_p, new_s = [], []
    for l in range(DEPTH):
        lp = _layer_params(l, a)
        prev_modp, prev_mods = modp, mods
        modp = mod[l, :bp].reshape(bp, 1, 6 * D_MODEL)
        mods = mod[l, bp:].reshape(1, ns, 6 * D_MODEL)

        xp, pp = _in_proj(xp, modp, t, norm1[l], w_in_p[l], ff, 0, prev_modp)
        xs, ps = _in_proj(xs, mods, ns, norm1[l], w_in_p[l], ff, n_p, prev_mods)
        ysp, stp = _prompt_mixers(pp.reshape(bp, t, PROJ_PAD), lp)
        yss, sts = _sample_mixers(ps, sample_states, l, lp)
        new_p.append(stp)
        new_s.append(sts)
        xp, xs, h2, rt = _out_proj(ysp, xp, modp, t, yss, xs, mods, norm2[l], w_out_b[l],
                                   w_route[l], b_route[l].reshape(1, 128))
        ff = _moe(h2, rt, n_p + ns, moe_w_gate, moe_w_up, moe_w_down, l)
    yp = _final(xp, ff, 0, modp, t, norm_f).reshape(bp, t, D_MODEL)
    ys = _final(xs, ff, n_p, mods, ns, norm_f).reshape(ns, 1, D_MODEL)
    stack = lambda lst: tuple(jnp.stack([st[i] for st in lst]) for i in range(8))
    return (yp, ys) + stack(new_p) + stack(new_s)
```

```python
import functools
import math

import jax
import jax.numpy as jnp
from jax import lax
from jax.experimental import pallas as pl
from jax.experimental.pallas import tpu as pltpu

F32 = jnp.float32
BF16 = jnp.bfloat16
HI = lax.Precision.HIGHEST

D_MODEL = 2048
DEPTH = 4
GW = 512
RWKV_HD = 64
RWKV_H = 8
RWKV_PROJ = 1792
RWKV_LN_EPS = 64e-5
ML_H = 4
HG_H = 4
GDN_H = 4
HD = 128
CONV_W = 4
N_GROUPS = 4
EPG = 8
N_EXPERTS = 32
D_FF = 256
NORM_EPS = 1e-6

PROJ_PAD = 8192
GATES_OFF = 1792
GATES_W = 256
COL_ML_I, COL_ML_F, COL_GDN_B, COL_GDN_A = 0, 4, 8, 12

CHUNK = 64
SUB = 16
VMEM_LIMIT = 56 * 1024 * 1024


def _cparams(sem):
    return pltpu.CompilerParams(dimension_semantics=sem, vmem_limit_bytes=VMEM_LIMIT)


def _mm(a, b):
    return jnp.dot(a.astype(BF16), b.astype(BF16), preferred_element_type=F32)


def _mm_nt(a, b):
    return lax.dot_general(a.astype(BF16), b.astype(BF16), (((1,), (1,)), ((), ())),
                           preferred_element_type=F32)


def _mm_tn(a, b):
    return lax.dot_general(a.astype(BF16), b.astype(BF16), (((0,), (0,)), ((), ())),
                           preferred_element_type=F32)


def _mm_hi(a, b):
    return jnp.dot(a, b, precision=HI, preferred_element_type=F32)


def _mm_nt_hi(a, b):
    return lax.dot_general(a, b, (((1,), (1,)), ((), ())), precision=HI, preferred_element_type=F32)


def _sigmoid(x):
    return 1.0 / (1.0 + jnp.exp(-x))


def _silu(x):
    return x * _sigmoid(x)


def _softplus(x):
    return jnp.maximum(x, 0.0) + jnp.log(1.0 + jnp.exp(-jnp.abs(x)))


def _iota(shape, dim):
    return lax.broadcasted_iota(jnp.int32, shape, dim)


def _eye(n, m):
    return (_iota((n, m), 0) == _iota((n, m), 1)).astype(F32)


def _lower(n, inclusive):
    r, c = _iota((n, n), 0), _iota((n, n), 1)
    return (c <= r) if inclusive else (c < r)


def _rms_rows(x):
    return x * lax.rsqrt(jnp.mean(x * x, axis=-1, keepdims=True) + NORM_EPS)


def _tri_inv_minus_eye(n_mat, size):
    m = -n_mat
    steps = int(math.log2(size))
    pw = _mm(n_mat, n_mat)
    yield
    for i in range(1, steps):
        t = _mm(m, pw)
        nxt = _mm(pw, pw) if i < steps - 1 else None
        yield
        m = m + pw + t
        pw = nxt
    return m


def _round_robin(gens):
    gens = list(gens)
    results = [None] * len(gens)
    active = list(range(len(gens)))
    while active:
        for i in list(active):
            try:
                next(gens[i])
            except StopIteration as stop:
                results[i] = stop.value
                active.remove(i)
    return results


def _ada_kernel(c_ref, w_ref, b_ref, o_ref):
    c = c_ref[...]
    o_ref[...] = _mm(_silu(c), w_ref[...]) + b_ref[...]


def _ada(c_all, ada_w, ada_b):
    depth, d, n6 = ada_w.shape
    rows = c_all.shape[0]
    tn = 1024
    return pl.pallas_call(
        _ada_kernel,
        grid=(depth, n6 // tn),
        in_specs=[pl.BlockSpec((rows, d), lambda l, j: (0, 0)),
                  pl.BlockSpec((None, d, tn), lambda l, j: (l, 0, j)),
                  pl.BlockSpec((None, 1, tn), lambda l, j: (l, 0, j))],
        out_specs=pl.BlockSpec((None, rows, tn), lambda l, j: (l, 0, j)),
        out_shape=jax.ShapeDtypeStruct((depth, rows, n6), F32),
        compiler_params=_cparams(("parallel", "parallel")),
        name="ada",
    )(c_all, ada_w, ada_b.reshape(depth, 1, n6))


def _modulated_norm(x, nw, sc, sh):
    return _rms_rows(x) * nw * (1.0 + sc) + sh


def _inproj_kernel(*refs, combine):
    if combine:
        x_ref, ff_ref, gt_ref, sh_ref, sc_ref, nw_ref, w_ref, xo_ref, p_ref, h_ref = refs
    else:
        x_ref, sh_ref, sc_ref, nw_ref, w_ref, p_ref, h_ref = refs

    @pl.when(pl.program_id(1) == 0)
    def _():
        x = x_ref[...]
        if combine:
            rows = x.shape[0]
            x = x + gt_ref[...] * (_load_slabs(ff_ref.at[0], rows) + _load_slabs(ff_ref.at[1], rows))
            xo_ref[...] = x
        h_ref[...] = _modulated_norm(x, nw_ref[...], sc_ref[...], sh_ref[...]).astype(BF16)

    p_ref[...] = jnp.dot(h_ref[...], w_ref[...], preferred_element_type=F32)


def _mod_spec(mod3, which, tm, rows_per_group):
    r = mod3.shape[1]
    return pl.BlockSpec((None, r, D_MODEL),
                        lambda i, *_: ((i * tm) // rows_per_group, 0, which))


def _in_proj(x, mod3, rows_per_group, nw, w, ff=None, ff_row0=0, gate_mod3=None):
    n = x.shape[0]
    tm = min(512, rows_per_group)
    tn = 1024
    combine = ff is not None
    row = lambda i, j: (i, 0)
    in_specs = [pl.BlockSpec((tm, D_MODEL), row)]
    args = [x]
    if combine:
        blk0 = ff_row0 // tm
        in_specs += [pl.BlockSpec((2, tm * SLAB, 128), lambda i, j: (0, blk0 + i, 0)),
                     _mod_spec(gate_mod3, 5, tm, rows_per_group)]
        args += [ff, gate_mod3]
    in_specs += [_mod_spec(mod3, 0, tm, rows_per_group), _mod_spec(mod3, 1, tm, rows_per_group),
                 pl.BlockSpec((1, D_MODEL), lambda i, j: (0, 0)),
                 pl.BlockSpec((D_MODEL, tn), lambda i, j: (0, j))]
    args += [mod3, mod3, nw.reshape(1, D_MODEL), w]
    out_specs = [pl.BlockSpec((tm, tn), lambda i, j: (i, j))]
    out_shape = [jax.ShapeDtypeStruct((n, PROJ_PAD), F32)]
    if combine:
        out_specs = [pl.BlockSpec((tm, D_MODEL), row)] + out_specs
        out_shape = [jax.ShapeDtypeStruct((n, D_MODEL), F32)] + out_shape
    res = pl.pallas_call(
        functools.partial(_inproj_kernel, combine=combine),
        grid=(n // tm, PROJ_PAD // tn),
        in_specs=in_specs, out_specs=out_specs, out_shape=out_shape,
        scratch_shapes=[pltpu.VMEM((tm, D_MODEL), BF16)],
        compiler_params=_cparams(("parallel", "arbitrary")),
        name="in_proj",
    )(*args)
    return (res[0], res[1]) if combine else (x, res[0])


def _lane_max(x):
    return jnp.max(x, axis=-1, keepdims=True)


def _lane_sum(x):
    return jnp.sum(x, axis=-1, keepdims=True)


def _first_lane_of(mask, lane):
    return jnp.min(jnp.where(mask, lane, 4096), axis=-1, keepdims=True)


def _route(logits, biased):
    lane = _iota(logits.shape, 1)
    neg = jnp.float32(-jnp.inf)
    is_g = lane < N_GROUPS
    gmax = _lane_max(jnp.where(is_g, logits, neg))
    gexp = jnp.where(is_g, jnp.exp(logits - gmax), 0.0)
    gb = jnp.where(is_g, biased, neg)
    gsel = _first_lane_of(gb == _lane_max(gb), lane)
    pg = _lane_sum(jnp.where(lane == gsel, gexp, 0.0)) / _lane_sum(gexp)
    lo = N_GROUPS + gsel * EPG
    eb = jnp.where((lane >= lo) & (lane < lo + EPG), biased, neg)
    i1 = _first_lane_of(eb == _lane_max(eb), lane)
    eb2 = jnp.where(lane == i1, neg, eb)
    i2 = _first_lane_of(eb2 == _lane_max(eb2), lane)
    l1 = _lane_sum(jnp.where(lane == i1, logits, 0.0))
    l2 = _lane_sum(jnp.where(lane == i2, logits, 0.0))
    mx = jnp.maximum(l1, l2)
    e1, e2 = jnp.exp(l1 - mx), jnp.exp(l2 - mx)
    cw1, cw2 = pg * e1 / (e1 + e2), pg * e2 / (e1 + e2)
    id1, id2 = (i1 - N_GROUPS).astype(F32), (i2 - N_GROUPS).astype(F32)
    return jnp.where(lane == 0, id1, jnp.where(lane == 1, id2,
                     jnp.where(lane == 2, cw1, jnp.where(lane == 3, cw2, 0.0))))


SLAB = D_MODEL // 128


def _load_slabs(ref, rows):
    return jnp.concatenate([ref[pl.ds(s, rows, stride=SLAB), :] for s in range(SLAB)], axis=1)


def _store_slabs(ref, x, rows):
    for s in range(SLAB):
        ref[pl.ds(s, rows, stride=SLAB), :] = x[:, s * 128:(s + 1) * 128]


def _outproj_kernel(*refs, n_prompt_tiles, tm, rows_s):
    group_p, group_s = refs[0:8], refs[8:16]
    nw_ref, wo_ref, wr_ref, br_ref, xop_ref, xos_ref, h2_ref, rt_ref = refs[16:]
    i = pl.program_id(0)

    def body(group, xo_ref, rows):
        yr_ref, ym_ref, yh_ref, yg_ref, x_ref, gt_ref, sh_ref, sc_ref = group
        mix = jnp.dot(yr_ref[...], wo_ref[0:GW, :], preferred_element_type=F32)
        mix += jnp.dot(ym_ref[...], wo_ref[GW:2 * GW, :], preferred_element_type=F32)
        mix += jnp.dot(yh_ref[...], wo_ref[2 * GW:3 * GW, :], preferred_element_type=F32)
        mix += jnp.dot(yg_ref[...], wo_ref[3 * GW:4 * GW, :], preferred_element_type=F32)
        x = x_ref[...] + gt_ref[...] * mix
        xo_ref[...] = x
        h2 = _modulated_norm(x, nw_ref[...], sc_ref[...], sh_ref[...])
        h_hi = h2.astype(BF16)
        h_lo = (h2 - h_hi.astype(F32)).astype(BF16)
        hw = jnp.dot(h_hi, wr_ref[...], preferred_element_type=F32)
        logits = hw[:, 0:128] + hw[:, 128:256] + jnp.dot(h_lo, wr_ref[:, 0:128], preferred_element_type=F32)
        route = _route(logits, logits + br_ref[...])
        _store_slabs(h2_ref, h2, rows)
        rt_ref[0:rows, :] = route
        if rows < tm:
            h2_ref[rows * SLAB:tm * SLAB, :] = jnp.zeros(((tm - rows) * SLAB, 128), F32)
            rt_ref[rows:tm, :] = jnp.zeros((tm - rows, 128), F32)

    @pl.when(i < n_prompt_tiles)
    def _():
        body(group_p, xop_ref, tm)

    @pl.when(i == n_prompt_tiles)
    def _():
        body(group_s, xos_ref, rows_s)


def _out_proj(ys_p, x_p, mod_p, t, ys_s, x_s, mod_s, nw, wo, w_route, b_route):
    n_p, n_s = x_p.shape[0], x_s.shape[0]
    tm = min(256, t)
    assert n_s <= tm and n_p % tm == 0
    n_pt = n_p // tm
    prow = lambda i: (jnp.minimum(i, n_pt - 1), 0)
    const = lambda i: (0, 0)
    pmod = lambda which: pl.BlockSpec((None, 1, D_MODEL),
                                      lambda i: ((jnp.minimum(i, n_pt - 1) * tm) // t, 0, which))
    smod = lambda which: pl.BlockSpec((None, n_s, D_MODEL), lambda i: (0, 0, which))
    in_specs = ([pl.BlockSpec((tm, GW), prow)] * 4 + [pl.BlockSpec((tm, D_MODEL), prow), pmod(2), pmod(3), pmod(4)]
                + [pl.BlockSpec((n_s, GW), const)] * 4 + [pl.BlockSpec((n_s, D_MODEL), const),
                                                          smod(2), smod(3), smod(4)]
                + [pl.BlockSpec((1, D_MODEL), const), pl.BlockSpec((D_MODEL, D_MODEL), const),
                   pl.BlockSpec((D_MODEL, 256), const), pl.BlockSpec((1, 128), const)])
    n_all = n_p + tm
    return pl.pallas_call(
        functools.partial(_outproj_kernel, n_prompt_tiles=n_pt, tm=tm, rows_s=n_s),
        grid=(n_pt + 1,),
        in_specs=in_specs,
        out_specs=[pl.BlockSpec((tm, D_MODEL), prow), pl.BlockSpec((n_s, D_MODEL), const),
                   pl.BlockSpec((tm * SLAB, 128), lambda i: (i, 0)), pl.BlockSpec((tm, 128), lambda i: (i, 0))],
        out_shape=[jax.ShapeDtypeStruct((n_p, D_MODEL), F32), jax.ShapeDtypeStruct((n_s, D_MODEL), F32),
                   jax.ShapeDtypeStruct((n_all * SLAB, 128), F32), jax.ShapeDtypeStruct((n_all, 128), F32)],
        compiler_params=_cparams(("arbitrary",)),
        name="out_proj",
    )(*ys_p, x_p, mod_p, mod_p, mod_p, *ys_s, x_s, mod_s, mod_s, mod_s,
      nw.reshape(1, D_MODEL), wo, w_route, b_route)


def _moe_kernel(te_ref, tok_ref, dst_ref, h2_hbm, cw_ref, wg_ref, wu_ref, wd_ref, out_hbm,
                hbuf0, hbuf1, obuf0, obuf1, wgb, wub, wdb, gsem, ssem, *, tm, n_tiles):
    t = pl.program_id(0)
    hbufs, obufs = (hbuf0, hbuf1), (obuf0, obuf1)

    def start_gather(tile, slot):
        for r in range(tm):
            src = pl.multiple_of(tok_ref[tile * tm + r], SLAB)
            pltpu.make_async_copy(h2_hbm.at[pl.ds(src, SLAB)], hbufs[slot].at[pl.ds(r * SLAB, SLAB)],
                                  gsem.at[slot]).start()

    def start_scatter(tile, slot):
        for r in range(tm):
            dst = pl.multiple_of(dst_ref[tile * tm + r], SLAB)
            pltpu.make_async_copy(obufs[slot].at[pl.ds(r * SLAB, SLAB)], out_hbm.at[pl.ds(dst, SLAB)],
                                  ssem.at[slot]).start(priority=1)

    def wait_gather(slot):
        pltpu.make_async_copy(h2_hbm.at[pl.ds(0, tm * SLAB)], hbufs[slot], gsem.at[slot]).wait()

    def wait_scatter(slot):
        pltpu.make_async_copy(obufs[slot], out_hbm.at[pl.ds(0, tm * SLAB)], ssem.at[slot]).wait()

    @pl.when(t == 0)
    def _():
        start_gather(0, 0)

    @pl.when((t == 0) | (te_ref[t] != te_ref[jnp.maximum(t - 1, 0)]))
    def _():
        wgb[...] = wg_ref[...].astype(BF16)
        wub[...] = wu_ref[...].astype(BF16)
        wdb[...] = wd_ref[...].astype(BF16)

    def step(slot):
        wait_gather(slot)
        start_gather(t + 1, 1 - slot)
        h = _load_slabs(hbufs[slot], tm).astype(BF16)
        g = jnp.dot(h, wgb[...], preferred_element_type=F32)
        u = jnp.dot(h, wub[...], preferred_element_type=F32)
        hid = (_silu(g) * u * cw_ref[...]).astype(BF16)
        _store_slabs(obufs[slot], jnp.dot(hid, wdb[...], preferred_element_type=F32), tm)
        start_scatter(t, slot)

    for slot in (0, 1):
        @pl.when(t % 2 == slot)
        def _(slot=slot):
            step(slot)

        @pl.when((t >= 1) & (t % 2 == slot))
        def _(slot=slot):
            wait_scatter(1 - slot)

    @pl.when(t == n_tiles - 1)
    def _():
        wait_scatter((n_tiles - 1) % 2)
        wait_gather(n_tiles % 2)


MOE_TM = 128


def _moe(h2, route, n_tok, w_gate, w_up, w_down, layer):
    tm = MOE_TM
    slots = 2 * n_tok
    n_tiles = -(-(slots + N_EXPERTS * (tm - 1)) // tm) + 2
    n_pad = n_tok + tm
    n_rows = (n_tiles + 1) * tm
    eid = jnp.concatenate([route[:n_tok, 0], route[:n_tok, 1]]).astype(jnp.int32)
    cw = jnp.concatenate([route[:n_tok, 2], route[:n_tok, 3]])
    onehot = (eid[:, None] == jnp.arange(N_EXPERTS, dtype=jnp.int32)[None, :]).astype(jnp.int32)
    n_blk = -(-slots // tm)
    oh3 = jnp.pad(onehot, ((0, n_blk * tm - slots), (0, 0))).reshape(n_blk, tm, N_EXPERTS).astype(BF16)
    tri = (jnp.arange(tm)[:, None] >= jnp.arange(tm)[None, :]).astype(BF16)
    within = jnp.einsum("ij,bjk->bik", tri, oh3, preferred_element_type=F32).astype(jnp.int32)
    blk_tot = within[:, -1, :]
    blk_off = jnp.cumsum(blk_tot, axis=0) - blk_tot
    running = (within + blk_off[:, None, :]).reshape(n_blk * tm, N_EXPERTS)[:slots]
    counts = jnp.sum(blk_tot, axis=0)
    tiles_per = (counts + tm - 1) // tm
    tile_end = jnp.cumsum(tiles_per)
    tile_start = tile_end - tiles_per
    rank = jnp.sum(running * onehot, axis=1) - 1
    pos = jnp.sum(onehot * tile_start[None, :], axis=1) * tm + rank
    s_idx = jnp.arange(slots, dtype=jnp.int32)
    tok = s_idx % n_tok
    dst = (s_idx // n_tok) * n_pad + tok
    j = jnp.arange(n_rows, dtype=jnp.int32)
    dump = ((j // tm) % 2) * n_pad + n_tok + j % tm
    default = jnp.stack([jnp.zeros_like(j), dump, jnp.zeros_like(j)], axis=1).astype(F32)
    packed = jnp.stack([tok.astype(F32), dst.astype(F32), cw], axis=1)
    rows = default.at[pos].set(packed)
    row_tok = rows[:, 0].astype(jnp.int32) * SLAB
    row_dst = rows[:, 1].astype(jnp.int32) * SLAB
    row_cw = rows[:, 2].reshape(n_rows, 1)
    tile_ids = jnp.arange(n_tiles, dtype=jnp.int32)
    tile_e = jnp.sum((tile_end[None, :] <= tile_ids[:, None]).astype(jnp.int32), axis=1)
    tile_e = jnp.minimum(tile_e, N_EXPERTS - 1)

    wspec = lambda shape: pl.BlockSpec((None, None) + shape, lambda t, te, *_: (layer, te[t], 0, 0))
    grid_spec = pltpu.PrefetchScalarGridSpec(
        num_scalar_prefetch=3,
        grid=(n_tiles,),
        in_specs=[pl.BlockSpec(memory_space=pl.ANY),
                  pl.BlockSpec((tm, 1), lambda t, *_: (t, 0)),
                  wspec((D_MODEL, D_FF)), wspec((D_MODEL, D_FF)), wspec((D_FF, D_MODEL))],
        out_specs=pl.BlockSpec(memory_space=pl.ANY),
        scratch_shapes=[pltpu.VMEM((tm * SLAB, 128), F32)] * 4
        + [pltpu.VMEM((D_MODEL, D_FF), BF16), pltpu.VMEM((D_MODEL, D_FF), BF16),
           pltpu.VMEM((D_FF, D_MODEL), BF16),
           pltpu.SemaphoreType.DMA((2,)), pltpu.SemaphoreType.DMA((2,))])
    out = pl.pallas_call(
        functools.partial(_moe_kernel, tm=tm, n_tiles=n_tiles),
        grid_spec=grid_spec,
        out_shape=jax.ShapeDtypeStruct((2 * n_pad * SLAB, 128), F32),
        compiler_params=_cparams(("arbitrary",)),
        name="moe",
    )(tile_e, row_tok, row_dst, h2, row_cw, w_gate, w_up, w_down)
    return out.reshape(2, n_pad * SLAB, 128)


def _final_kernel(x_ref, ff_ref, gt_ref, nw_ref, o_ref):
    rows = x_ref.shape[0]
    x = x_ref[...] + gt_ref[...] * (_load_slabs(ff_ref.at[0], rows) + _load_slabs(ff_ref.at[1], rows))
    o_ref[...] = _rms_rows(x) * nw_ref[...]


def _final(x, ff, ff_row0, mod3, rows_per_group, nw):
    n = x.shape[0]
    tm = min(512, rows_per_group)
    blk0 = ff_row0 // tm
    return pl.pallas_call(
        _final_kernel,
        grid=(n // tm,),
        in_specs=[pl.BlockSpec((tm, D_MODEL), lambda i: (i, 0)),
                  pl.BlockSpec((2, tm * SLAB, 128), lambda i: (0, blk0 + i, 0)),
                  _mod_spec(mod3, 5, tm, rows_per_group),
                  pl.BlockSpec((1, D_MODEL), lambda i: (0, 0))],
        out_specs=pl.BlockSpec((tm, D_MODEL), lambda i: (i, 0)),
        out_shape=jax.ShapeDtypeStruct((n, D_MODEL), F32),
        compiler_params=_cparams(("parallel",)),
        name="final_norm",
    )(x, ff, mod3, nw.reshape(1, D_MODEL))


def _seg_sum(x, width):
    seg = (_iota((128, 128), 0) // width == _iota((128, 128), 1) // width).astype(F32)
    parts = [_mm_hi(x[:, j:j + 128], seg) for j in range(0, x.shape[1], 128)]
    return parts[0] if len(parts) == 1 else jnp.concatenate(parts, axis=1)


def _head_norm(o, nw, gate):
    return o * lax.rsqrt(_seg_sum(o * o, HD) * (1.0 / HD) + NORM_EPS) * nw * gate


def _carry_rows(ext_ref, cur, first, L):
    @pl.when(first)
    def _():
        ext_ref[0:8, :] = jnp.zeros((8, ext_ref.shape[1]), F32)

    ext_ref[8:8 + L, :] = cur


def _rwkv_parts(p_ref, mu_ref, vec_ref, wup_ref, aup_ref, gup_ref, y_ref, sh_ref, s_ref, ext_ref, L):
    c = pl.program_id(1)
    p = p_ref[...]
    _carry_rows(ext_ref, p, c == 0, L)
    prev = ext_ref[7:7 + L, :]
    xs = p + (prev - p) * mu_ref[...]
    ext_ref[0:8, :] = ext_ref[L:L + 8, :]
    sh_ref[...] = p[L - 1:L, :]

    @pl.when(c == 0)
    def _():
        s_ref[...] = jnp.zeros_like(s_ref)

    w0, a0, k_k, k_a, r_k, ln_w, ln_b = [vec_ref[i:i + 1, :] for i in range(7)]
    r, k, v = xs[:, 0:GW], xs[:, GW:2 * GW], xs[:, 2 * GW:3 * GW]
    dw, da, dg = xs[:, 1536:1600], xs[:, 1600:1664], xs[:, 1664:1792]
    logw = -_softplus(-(w0 + _mm(jnp.tanh(dw), wup_ref[...]))) - 0.5
    lw = -jnp.exp(logw)
    a = _sigmoid(a0 + _mm(da, aup_ref[...]))
    g = _mm(_sigmoid(dg), gup_ref[...])
    kk = k * k_k
    kk = kk * lax.rsqrt(jnp.maximum(_seg_sum(kk * kk, RWKV_HD), 1e-12))
    k2 = k * (1.0 + (a - 1.0) * k_a)
    cl = _mm_hi(_lower(L, True).astype(F32), lw)
    cl_last = cl[L - 1:L, :]
    at = -kk * jnp.exp(cl - lw)
    bt = kk * a * jnp.exp(-cl)
    kt = k2 * jnp.exp(-cl)
    rt = r * jnp.exp(cl)
    b_end = kk * a * jnp.exp(cl_last - cl)
    k_end = k2 * jnp.exp(cl_last - cl)
    strict, incl = _lower(L, False), _lower(L, True)
    def head(h):
        sl = slice(h * RWKV_HD, (h + 1) * RWKV_HD)
        s0 = s_ref[h]
        ath, bth, kth, rth, vh = at[:, sl], bt[:, sl], kt[:, sl], rt[:, sl], v[:, sl]
        g_ab, g_ak, g_rb, g_rk = _mm_nt(ath, bth), _mm_nt(ath, kth), _mm_nt(rth, bth), _mm_nt(rth, kth)
        a_s0, r_s0 = _mm_nt(ath, s0), _mm_nt(rth, s0)
        s_v = _mm_tn(vh, k_end[:, sl])
        yield
        n_ab = jnp.where(strict, g_ab, 0.0)
        ak_v = _mm(jnp.where(strict, g_ak, 0.0), vh)
        rk_v = _mm(jnp.where(incl, g_rk, 0.0), vh)
        m = yield from _tri_inv_minus_eye(-n_ab, L)
        rhs = a_s0 + ak_v
        m_rhs = _mm(m, rhs)
        yield
        u = rhs + m_rhs
        rb_u = _mm(jnp.where(incl, g_rb, 0.0), u)
        s_u = _mm_tn(u, b_end[:, sl])
        yield
        s_ref[h] = s0 * jnp.exp(cl_last[:, sl]) + s_u + s_v
        return r_s0 + rb_u + rk_v

    def finish(ys):
        y = jnp.concatenate(ys, axis=1)
        mean = _seg_sum(y, RWKV_HD) * (1.0 / RWKV_HD)
        dev = y - mean
        var = _seg_sum(dev * dev, RWKV_HD) * (1.0 / RWKV_HD)
        y = dev * lax.rsqrt(var + RWKV_LN_EPS) * ln_w + ln_b
        bonus = _seg_sum(r * k2 * r_k, RWKV_HD) * v
        y_ref[...] = ((y + bonus) * g).astype(y_ref.dtype)

    return [head(h) for h in range(RWKV_H)], finish


def _gate_forms(g_ref, gp_ref, L):
    raw = g_ref[:, 0:128] + gp_ref[0:1, :]
    lane = _iota(raw.shape, 1)
    lf = jnp.minimum(raw, 0.0) - jnp.log(1.0 + jnp.exp(-jnp.abs(raw)))
    beta = _sigmoid(raw)
    decay = -jnp.exp(gp_ref[1:2, :]) * _softplus(raw)
    cols = jnp.where(lane < 4, raw, jnp.where(lane < 8, lf, jnp.where(lane < 12, beta, decay)))
    rows = _mm_nt_hi(_eye(16, 128), cols)
    ccols = _mm_hi(_lower(L, True).astype(F32), cols)
    upper = (_iota((L, L), 0) <= _iota((L, L), 1)).astype(F32)
    crows = _mm_hi(rows, upper)
    return cols, rows, ccols, crows


def _mlstm_parts(p_ref, gate_forms, nw_ref, y_ref, c_ref, n_ref, m_ref, L):
    c = pl.program_id(1)

    @pl.when(c == 0)
    def _():
        c_ref[...] = jnp.zeros_like(c_ref)
        n_ref[...] = jnp.zeros_like(n_ref)
        m_ref[...] = jnp.zeros_like(m_ref)

    cols, rows, ccols, crows = gate_forms
    causal = _lower(L, True)
    neg = jnp.float32(-jnp.inf)
    def head(h):
        q = p_ref[:, h * HD:(h + 1) * HD]
        k = p_ref[:, GW + h * HD:GW + (h + 1) * HD] * (HD ** -0.5)
        v = p_ref[:, 2 * GW + h * HD:2 * GW + (h + 1) * HD]
        ig_col, ig_row = cols[:, h:h + 1], rows[h:h + 1, :]
        b_col, b_row = ccols[:, 4 + h:5 + h], crows[4 + h:5 + h, :]
        m_prev = m_ref[:, h:h + 1]
        cmat, nvec = c_ref[h], n_ref[h:h + 1, :]
        qk, qc = _mm_nt(q, k), _mm_nt(q, cmat)
        dmat = jnp.where(causal, b_col - b_row + ig_row, neg)
        gcol = b_col + m_prev
        mt = jnp.maximum(gcol, _lane_max(dmat))
        m_last = mt[L - 1:L, :]
        wk = jnp.exp(b_col[L - 1:L, :] - b_col + ig_col - m_last)
        decay = jnp.exp(gcol[L - 1:L, :] - m_last)
        c_upd = _mm_tn(v * wk, k)
        yield
        smat = qk * jnp.exp(dmat - mt)
        s_v = _mm(smat, v)
        wg = jnp.exp(gcol - mt)
        den = _lane_sum(smat) + wg * _lane_sum(q * nvec)
        c_ref[h] = decay * cmat + c_upd
        n_ref[h:h + 1, :] = decay * nvec + jnp.sum(k * wk, axis=0, keepdims=True)
        m_ref[:, h:h + 1] = m_last
        yield
        return (s_v + wg * qc) / jnp.maximum(jnp.abs(den), jnp.exp(-mt))

    def finish(ys):
        o = p_ref[:, 3 * GW:4 * GW]
        y_ref[...] = _head_norm(jnp.concatenate(ys, axis=1), nw_ref[...], _sigmoid(o)).astype(y_ref.dtype)

    return [head(h) for h in range(ML_H)], finish


def _hgrn_parts(p_ref, lb_ref, nw_ref, y_ref, s_ref, L):
    c = pl.program_id(1)

    @pl.when(c == 0)
    def _():
        s_ref[...] = jnp.zeros_like(s_ref)

    lb = lb_ref[...]
    qa = _silu(p_ref[:, 0:GW])
    fg = lb + (1.0 - lb) * _sigmoid(p_ref[:, GW:2 * GW])
    ka = 1.0 - fg
    va = p_ref[:, 2 * GW:3 * GW]
    cga = _mm_hi(_lower(L, True).astype(F32), jnp.log(fg))
    ones = jnp.ones((HD, HD), BF16)
    t3, s3 = _iota((SUB, SUB, HD), 0), _iota((SUB, SUB, HD), 1)
    neg = jnp.float32(-jnp.inf)
    def head(h):
        sl = slice(h * HD, (h + 1) * HD)
        q, k, v, cg = qa[:, sl], ka[:, sl], va[:, sl], cga[:, sl]
        st = s_ref[h]
        inter = _mm_nt(q * jnp.exp(cg), st)
        cg_last = cg[L - 1:L, :]
        s_upd = _mm_tn(v, k * jnp.exp(cg_last - cg))
        a3s, a_offs = [], []
        for i in range(L // SUB):
            lo = i * SUB
            qi, ki, cgi = q[lo:lo + SUB], k[lo:lo + SUB], cg[lo:lo + SUB]
            e3 = jnp.exp(jnp.where(s3 <= t3, cgi[:, None, :] - cgi[None, :, :], neg))
            x3 = qi[:, None, :] * ki[None, :, :] * e3
            a3s.append(jnp.dot(x3.reshape(SUB * SUB, HD).astype(BF16), ones, preferred_element_type=F32))
            if i > 0:
                ref = cg[lo - 1:lo, :]
                a_offs.append(_mm_nt(qi * jnp.exp(cgi - ref), k[0:lo] * jnp.exp(ref - cg[0:lo])))
        yield
        s_ref[h] = st * jnp.exp(cg_last) + s_upd
        offs = [_mm(a_off, v[0:(i + 1) * SUB]) for i, a_off in enumerate(a_offs)]
        yield
        rows_out = []
        for i in range(L // SUB):
            lo = i * SUB
            oi = inter[lo:lo + SUB] + jnp.sum(a3s[i].reshape(SUB, SUB, HD) * v[lo:lo + SUB][None, :, :], axis=1)
            rows_out.append(oi if i == 0 else oi + offs[i - 1])
        return jnp.concatenate(rows_out, axis=0)

    def finish(ys):
        g = p_ref[:, 3 * GW:4 * GW]
        y_ref[...] = _head_norm(jnp.concatenate(ys, axis=1), nw_ref[...], _silu(g)).astype(y_ref.dtype)

    return [head(h) for h in range(HG_H)], finish


def _gdn_parts(p_ref, gate_forms, cw_ref, nw_ref, y_ref, cv_ref, s_ref, ext_ref, L):
    c = pl.program_id(1)
    w3 = 3 * GW
    _carry_rows(ext_ref, p_ref[:, 0:w3], c == 0, L)
    conv = (cw_ref[3:4, :] * ext_ref[8:8 + L, :] + cw_ref[2:3, :] * ext_ref[7:7 + L, :]
            + cw_ref[1:2, :] * ext_ref[6:6 + L, :] + cw_ref[0:1, :] * ext_ref[5:5 + L, :])
    cv_ref[...] = ext_ref[L + 5:L + 8, :]
    ext_ref[0:8, :] = ext_ref[L:L + 8, :]

    @pl.when(c == 0)
    def _():
        s_ref[...] = jnp.zeros_like(s_ref)

    qkv = _silu(conv)
    cols, rows, ccols, crows = gate_forms
    strict, causal = _lower(L, False), _lower(L, True)
    neg = jnp.float32(-jnp.inf)
    qa = qkv[:, 0:GW]
    ka = qkv[:, GW:2 * GW]
    qa = qa * lax.rsqrt(jnp.maximum(_seg_sum(qa * qa, HD), 1e-12)) * (HD ** -0.5)
    ka = ka * lax.rsqrt(jnp.maximum(_seg_sum(ka * ka, HD), 1e-12))

    def head(h):
        sl = slice(h * HD, (h + 1) * HD)
        q, k, v = qa[:, sl], ka[:, sl], qkv[:, 2 * GW + h * HD:2 * GW + (h + 1) * HD]
        beta = cols[:, COL_GDN_B + h:COL_GDN_B + h + 1]
        cg_col = ccols[:, COL_GDN_A + h:COL_GDN_A + h + 1]
        cg_row = crows[COL_GDN_A + h:COL_GDN_A + h + 1, :]
        diff = cg_col - cg_row
        s0 = s_ref[h]
        kb = k * beta
        g_kk, g_qk = _mm_nt(kb, k), _mm_nt(q, k)
        q_s0 = _mm(q * jnp.exp(cg_col), s0)
        yield
        n_mat = g_kk * jnp.exp(jnp.where(strict, diff, neg))
        m = yield from _tri_inv_minus_eye(n_mat, L)
        rhs = jnp.concatenate([v * beta, kb * jnp.exp(cg_col)], axis=1)
        m_rhs = _mm(m, rhs)
        yield
        uw = rhs + m_rhs
        w_s0 = _mm(uw[:, HD:2 * HD], s0)
        yield
        u2 = uw[:, 0:HD] - w_s0
        qk = g_qk * jnp.exp(jnp.where(causal, diff, neg))
        qk_u = _mm(qk, u2)
        cg_last = cg_col[L - 1:L, :]
        s_upd = _mm_tn(k * jnp.exp(cg_last - cg_col), u2)
        yield
        s_ref[h] = jnp.exp(cg_last) * s0 + s_upd
        return q_s0 + qk_u

    def finish(ys):
        g = p_ref[:, w3:w3 + GW]
        y_ref[...] = _head_norm(jnp.concatenate(ys, axis=1), nw_ref[...], _silu(g)).astype(y_ref.dtype)

    return [head(h) for h in range(GDN_H)], finish


def _mixers_kernel(pr_ref, pg_ref, pm_ref, ph_ref, pd_ref,
                   mu_ref, vec_ref, wup_ref, aup_ref, gup_ref, gp_ref, mlw_ref, lb_ref, hgw_ref, cw_ref, gdw_ref,
                   yr_ref, sh_ref, wkv_ref, ym_ref, mc_ref, mn_ref, mm_ref, yh_ref, hg_ref, yg_ref, cv_ref, gd_ref,
                   ext_r, ext_g, *, L):
    gate_forms = _gate_forms(pg_ref, gp_ref, L)
    parts = [
        _rwkv_parts(pr_ref, mu_ref, vec_ref, wup_ref, aup_ref, gup_ref, yr_ref, sh_ref, wkv_ref, ext_r, L),
        _mlstm_parts(pm_ref, gate_forms, mlw_ref, ym_ref, mc_ref, mn_ref, mm_ref, L),
        _hgrn_parts(ph_ref, lb_ref, hgw_ref, yh_ref, hg_ref, L),
        _gdn_parts(pd_ref, gate_forms, cw_ref, gdw_ref, yg_ref, cv_ref, gd_ref, ext_g, L),
    ]
    results = _round_robin(g for gens, _ in parts for g in gens)
    at = 0
    for gens, finish in parts:
        finish(results[at:at + len(gens)])
        at += len(gens)


def _prompt_mixers(p3, lp):
    b, t, _ = p3.shape
    L = math.gcd(t, CHUNK)
    grid = (b, t // L)
    cp = _cparams(("parallel", "arbitrary"))
    col = lambda width, idx: pl.BlockSpec((None, L, width), lambda i, c: (i, c, idx))
    const2 = lambda shape: pl.BlockSpec(shape, lambda i, c: (0, 0))
    y_spec = pl.BlockSpec((None, L, GW), lambda i, c: (i, c, 0))
    y_shape = jax.ShapeDtypeStruct((b, t, GW), BF16)
    state = lambda *s: (pl.BlockSpec((None,) + s, lambda i, c: (i,) + (0,) * len(s)),
                        jax.ShapeDtypeStruct((b,) + s, F32))
    gates = col(GATES_W, GATES_OFF // GATES_W)
    y_out = (y_spec, y_shape)
    outs = [y_out, state(1, RWKV_PROJ), state(RWKV_H, RWKV_HD, RWKV_HD),
            y_out, state(ML_H, HD, HD), state(ML_H, HD), state(1, 128),
            y_out, state(HG_H, HD, HD),
            y_out, state(CONV_W - 1, 3 * GW), state(GDN_H, HD, HD)]
    specs, shapes = zip(*outs)
    yr, n_sh, n_wkv, ym, n_c, n_n, n_m, yh, n_hg, yg, n_cv, n_gd = pl.pallas_call(
        functools.partial(_mixers_kernel, L=L), grid=grid,
        in_specs=[col(RWKV_PROJ, 0), gates, col(4 * GW, 1), col(4 * GW, 2), col(4 * GW, 3),
                  const2((1, RWKV_PROJ)), const2((8, GW)), const2((64, GW)), const2((64, GW)), const2((128, GW)),
                  const2((8, 128)), const2((1, GW)), const2((1, GW)), const2((1, GW)),
                  const2((CONV_W, 3 * GW)), const2((1, GW))],
        out_specs=list(specs), out_shape=list(shapes),
        scratch_shapes=[pltpu.VMEM((L + 8, RWKV_PROJ), F32), pltpu.VMEM((L + 8, 3 * GW), F32)],
        compiler_params=cp, name="mixers",
    )(p3, p3, p3, p3, p3, lp["rwkv_mu"], lp["rwkv_vec"], lp["rwkv_w_up"], lp["rwkv_a_up"], lp["rwkv_g_up"],
      lp["gate_par"], lp["ml_norm"], lp["hg_lb"], lp["hg_norm"], lp["gdn_conv_w"], lp["gdn_norm"])

    ys = [y.reshape(b * t, GW) for y in (yr, ym, yh, yg)]
    states = (n_sh, n_wkv, n_c, n_n, n_m[:, 0, :ML_H], jnp.swapaxes(n_hg, -1, -2), n_cv, n_gd)
    return ys, states


DEC_BS = 8


def _col_to_row(col):
    n = col.shape[0]
    return jnp.sum(jnp.where(_iota((n, n), 0) == _iota((n, n), 1), col, 0.0), axis=0, keepdims=True)


def _sub_sum(x):
    return jnp.sum(x, axis=0, keepdims=True)


def _decode_kernel(p_ref, sh_ref, wkv_ref, mc_ref, mn_ref, mm_ref, hg_ref, cv_ref, gd_ref,
                   mu_ref, vec_ref, wup_ref, aup_ref, gup_ref, gp_ref, mlw_ref, lb_ref, hgw_ref,
                   cw_ref, gdw_ref,
                   yr_ref, ym_ref, yh_ref, yg_ref,
                   nsh_ref, nwkv_ref, nmc_ref, nmn_ref, nmm_ref, nhg_ref, ncv_ref, ngd_ref):
    bs = DEC_BS
    w3 = 3 * GW
    pr = p_ref[:, 0:RWKV_PROJ]
    xs = pr + (sh_ref[...] - pr) * mu_ref[...]
    nsh_ref[...] = pr
    w0, a0, k_k, k_a, r_k, ln_w, ln_b = [vec_ref[i:i + 1, :] for i in range(7)]
    r, k, v = xs[:, 0:GW], xs[:, GW:2 * GW], xs[:, 2 * GW:3 * GW]
    dw, da, dg = xs[:, 1536:1600], xs[:, 1600:1664], xs[:, 1664:1792]
    logw = -_softplus(-(w0 + _mm(jnp.tanh(dw), wup_ref[...]))) - 0.5
    wdec = jnp.exp(-jnp.exp(logw))
    a = _sigmoid(a0 + _mm(da, aup_ref[...]))
    g_r = _mm(_sigmoid(dg), gup_ref[...])
    kk = k * k_k
    kk = kk * lax.rsqrt(jnp.maximum(_seg_sum(kk * kk, RWKV_HD), 1e-12))
    k2 = k * (1.0 + (a - 1.0) * k_a)
    ra, rb = -kk, kk * a

    gates = p_ref[:, GATES_OFF:GATES_OFF + 128] + gp_ref[0:1, :]
    ml_off = GATES_OFF + GATES_W
    mq = p_ref[:, ml_off:ml_off + GW]
    mk = p_ref[:, ml_off + GW:ml_off + 2 * GW] * (HD ** -0.5)
    mv = p_ref[:, ml_off + 2 * GW:ml_off + 3 * GW]
    mo = p_ref[:, ml_off + 3 * GW:ml_off + 4 * GW]
    m_ig = gates
    m_lf = jnp.minimum(gates, 0.0) - jnp.log(1.0 + jnp.exp(-jnp.abs(gates)))

    hg_off = ml_off + 4 * GW
    lb = lb_ref[...]
    hq = _silu(p_ref[:, hg_off:hg_off + GW])
    hfg = lb + (1.0 - lb) * _sigmoid(p_ref[:, hg_off + GW:hg_off + 2 * GW])
    hv = p_ref[:, hg_off + 2 * GW:hg_off + 3 * GW]
    hgate = p_ref[:, hg_off + 3 * GW:hg_off + 4 * GW]

    gd_off = hg_off + 4 * GW
    cur = p_ref[:, gd_off:gd_off + w3]
    conv = (cw_ref[3:4, :] * cur + cw_ref[2:3, :] * cv_ref[:, 2 * w3:3 * w3]
            + cw_ref[1:2, :] * cv_ref[:, w3:2 * w3] + cw_ref[0:1, :] * cv_ref[:, 0:w3])
    ncv_ref[:, 0:2 * w3] = cv_ref[:, w3:3 * w3]
    ncv_ref[:, 2 * w3:3 * w3] = cur
    qkv = _silu(conv)
    gq, gk, gv = qkv[:, 0:GW], qkv[:, GW:2 * GW], qkv[:, 2 * GW:3 * GW]
    gq = gq * lax.rsqrt(jnp.maximum(_seg_sum(gq * gq, HD), 1e-12)) * (HD ** -0.5)
    gk = gk * lax.rsqrt(jnp.maximum(_seg_sum(gk * gk, HD), 1e-12))
    ggate = p_ref[:, gd_off + w3:gd_off + w3 + GW]
    g_beta = _sigmoid(gates)
    g_dec = jnp.exp(-jnp.exp(gp_ref[1:2, :]) * _softplus(gates))

    blocks = ([mv[:, h * HD:(h + 1) * HD] for h in range(ML_H)]
              + [hq[:, h * HD:(h + 1) * HD] for h in range(HG_H)]
              + [hfg[:, h * HD:(h + 1) * HD] for h in range(HG_H)]
              + [gq[:, h * HD:(h + 1) * HD] for h in range(GDN_H)]
              + [gk[:, h * HD:(h + 1) * HD] for h in range(GDN_H)]
              + [v[:, j * 128:(j + 1) * 128] for j in range(4)])
    xt = _mm_nt_hi(_eye(128, 128), jnp.concatenate(blocks, axis=0))
    colf = lambda blk, s: xt[:, blk * bs + s:blk * bs + s + 1]

    yr_rows, ym_rows, yh_rows, yg_rows = [], [], [], []
    for s in range(bs):
        row = lambda arr, lo, width: arr[s:s + 1, lo:lo + width]
        parts = []
        for h in range(RWKV_H):
            lo = h * RWKV_HD
            st = wkv_ref[s, h]
            v_col = colf(20 + h // 2, s)[(h % 2) * 64:(h % 2) * 64 + 64, :]
            sa = _lane_sum(st * row(ra, lo, 64))
            st = st * row(wdec, lo, 64) + sa * row(rb, lo, 64) + v_col * row(k2, lo, 64)
            nwkv_ref[s, h] = st
            parts.append(_col_to_row(_lane_sum(st * row(r, lo, 64))))
        yr_rows.append(jnp.concatenate(parts, axis=1))
        parts = []
        for h in range(ML_H):
            lo = h * HD
            q_r, k_r = row(mq, lo, HD), row(mk, lo, HD)
            ig, lf = m_ig[s:s + 1, COL_ML_I + h:COL_ML_I + h + 1], m_lf[s:s + 1, COL_ML_F + h:COL_ML_F + h + 1]
            cmat, nvec, m_prev = mc_ref[s, h], mn_ref[s, h:h + 1, :], mm_ref[s:s + 1, h:h + 1]
            gsc = lf + m_prev
            mt = jnp.maximum(gsc, ig)
            wi, wg = jnp.exp(ig - mt), jnp.exp(gsc - mt)
            sc = _lane_sum(q_r * k_r) * wi
            v_col = colf(h, s)
            num = sc * v_col + wg * _lane_sum(cmat * q_r)
            den = sc + wg * _lane_sum(nvec * q_r)
            parts.append(_col_to_row(num / jnp.maximum(jnp.abs(den), jnp.exp(-mt))))
            nmc_ref[s, h] = wg * cmat + (wi * v_col) * k_r
            nmn_ref[s, h:h + 1, :] = wg * nvec + wi * k_r
            nmm_ref[s:s + 1, h:h + 1] = mt
        ym_rows.append(jnp.concatenate(parts, axis=1))
        parts = []
        for h in range(HG_H):
            lo = h * HD
            st = hg_ref[s, h]
            q_col, fg_col = colf(4 + h, s), colf(8 + h, s)
            q_r, fg_r, v_r = row(hq, lo, HD), row(hfg, lo, HD), row(hv, lo, HD)
            parts.append(_sub_sum(st * (q_col * fg_col)) + _lane_sum(q_r * (1.0 - fg_r)) * v_r)
            nhg_ref[s, h] = fg_col * st + (1.0 - fg_col) * v_r
        yh_rows.append(jnp.concatenate(parts, axis=1))
        parts = []
        for h in range(GDN_H):
            lo = h * HD
            st = gd_ref[s, h]
            q_col, k_col = colf(12 + h, s), colf(16 + h, s)
            beta = g_beta[s:s + 1, COL_GDN_B + h:COL_GDN_B + h + 1]
            dec = g_dec[s:s + 1, COL_GDN_A + h:COL_GDN_A + h + 1]
            u2 = beta * row(gv, lo, HD) - _sub_sum(st * (k_col * (beta * dec)))
            qk = _lane_sum(row(gq, lo, HD) * row(gk, lo, HD))
            parts.append(dec * _sub_sum(st * q_col) + qk * u2)
            ngd_ref[s, h] = dec * st + k_col * u2
        yg_rows.append(jnp.concatenate(parts, axis=1))

    yr = jnp.concatenate(yr_rows, axis=0)
    mean = _seg_sum(yr, RWKV_HD) * (1.0 / RWKV_HD)
    dev = yr - mean
    var = _seg_sum(dev * dev, RWKV_HD) * (1.0 / RWKV_HD)
    yr = dev * lax.rsqrt(var + RWKV_LN_EPS) * ln_w + ln_b
    bonus = _seg_sum(r * k2 * r_k, RWKV_HD) * v
    yr_ref[...] = ((yr + bonus) * g_r).astype(yr_ref.dtype)
    ym_ref[...] = _head_norm(jnp.concatenate(ym_rows, axis=0), mlw_ref[...], _sigmoid(mo)).astype(ym_ref.dtype)
    yh_ref[...] = _head_norm(jnp.concatenate(yh_rows, axis=0), hgw_ref[...], _silu(hgate)).astype(yh_ref.dtype)
    yg_ref[...] = _head_norm(jnp.concatenate(yg_rows, axis=0), gdw_ref[...], _silu(ggate)).astype(yg_ref.dtype)


def _sample_mixers(p, states, layer, lp):
    n = p.shape[0]
    bs = DEC_BS

    def st_spec(arr):
        tail = arr.shape[2:]
        return pl.BlockSpec((None, bs) + tail, lambda i: (layer, i) + (0,) * len(tail))

    const2 = lambda shape: pl.BlockSpec(shape, lambda i: (0, 0))
    st_specs = [st_spec(a) for a in states]
    y_spec = pl.BlockSpec((bs, GW), lambda i: (i, 0))
    y_shape = jax.ShapeDtypeStruct((n, GW), BF16)
    res = pl.pallas_call(
        _decode_kernel,
        grid=(n // bs,),
        in_specs=[pl.BlockSpec((bs, PROJ_PAD), lambda i: (i, 0))] + st_specs
        + [const2((1, RWKV_PROJ)), const2((8, GW)), const2((64, GW)), const2((64, GW)), const2((128, GW)),
           const2((8, 128)), const2((1, GW)), const2((1, GW)), const2((1, GW)),
           const2((CONV_W, 3 * GW)), const2((1, GW))],
        out_specs=[y_spec] * 4 + st_specs,
        out_shape=[y_shape] * 4 + [jax.ShapeDtypeStruct(a.shape, F32) for a in states],
        input_output_aliases={1 + k: 4 + k for k in range(len(states))},
        compiler_params=_cparams(("arbitrary",)),
        name="decode_mixers",
    )(p, *states, lp["rwkv_mu"], lp["rwkv_vec"], lp["rwkv_w_up"], lp["rwkv_a_up"], lp["rwkv_g_up"],
      lp["gate_par"], lp["ml_norm"], lp["hg_lb"], lp["hg_norm"], lp["gdn_conv_w"], lp["gdn_norm"])
    return list(res[:4]), tuple(res[4:])


def _layer_params(l, a):
    gate_par = jnp.zeros((8, 128), F32)
    gate_par = gate_par.at[0, COL_ML_I:COL_ML_I + 4].set(a["ml_i_bias"][l])
    gate_par = gate_par.at[0, COL_ML_F:COL_ML_F + 4].set(a["ml_f_bias"][l])
    gate_par = gate_par.at[0, COL_GDN_A:COL_GDN_A + 4].set(a["gdn_dt_bias"][l])
    gate_par = gate_par.at[1, COL_GDN_A:COL_GDN_A + 4].set(a["gdn_a_log"][l])
    vec = jnp.stack([a[k][l] for k in ("rwkv_w0", "rwkv_a0", "rwkv_k_k", "rwkv_k_a", "rwkv_r_k",
                                        "rwkv_ln_w", "rwkv_ln_b")] + [jnp.zeros((GW,), F32)])
    return {
        "rwkv_mu": a["rwkv_mu"][l].reshape(1, RWKV_PROJ), "rwkv_vec": vec,
        "rwkv_w_up": a["rwkv_w_up"][l], "rwkv_a_up": a["rwkv_a_up"][l], "rwkv_g_up": a["rwkv_g_up"][l],
        "gate_par": gate_par, "ml_norm": a["ml_norm"][l].reshape(1, GW),
        "hg_lb": a["hg_lbs"][l].reshape(1, GW), "hg_norm": a["hg_norm"][l].reshape(1, GW),
        "gdn_conv_w": a["gdn_conv_w"][l], "gdn_norm": a["gdn_norm"][l].reshape(1, GW),
    }


def _pad_w_in(w_in):
    z = jnp.zeros(w_in.shape[:2] + (GATES_W - 16,), w_in.dtype)
    parts = [w_in[..., 0:1792], w_in[..., 3840:3848], w_in[..., 7944:7952], z,
             w_in[..., 1792:3840], w_in[..., 3848:5896], w_in[..., 5896:7944]]
    return jnp.concatenate(parts, axis=-1).astype(BF16)


def kernel(x_prompt, x_sample, state_rwkv_shift, state_rwkv_wkv, state_mlstm_c, state_mlstm_n,
           state_mlstm_m, state_hgrn, state_gdn_conv, state_gdn, c_prompt, c_sample,
           ada_w, ada_b, norm1, norm2, norm_f, w_in, w_out,
           rwkv_mu, rwkv_w0, rwkv_w_up, rwkv_a0, rwkv_a_up, rwkv_g_up, rwkv_k_k, rwkv_k_a, rwkv_r_k,
           rwkv_ln_w, rwkv_ln_b, ml_i_bias, ml_f_bias, ml_norm, hg_lb, hg_norm,
           gdn_conv_w, gdn_a_log, gdn_dt_bias, gdn_norm,
           moe_w_group, moe_b_group, moe_w_router, moe_b_router, moe_w_gate, moe_w_up, moe_w_down):
    bp, t, _ = x_prompt.shape
    ns = x_sample.shape[0]
    lbs = jax.nn.softmax(hg_lb.astype(F32), axis=0)
    hg_lbs = jnp.cumsum(lbs, axis=0) - lbs[0]
    a = dict(rwkv_mu=rwkv_mu, rwkv_w0=rwkv_w0, rwkv_w_up=rwkv_w_up, rwkv_a0=rwkv_a0, rwkv_a_up=rwkv_a_up,
             rwkv_g_up=rwkv_g_up, rwkv_k_k=rwkv_k_k, rwkv_k_a=rwkv_k_a, rwkv_r_k=rwkv_r_k,
             rwkv_ln_w=rwkv_ln_w, rwkv_ln_b=rwkv_ln_b, ml_i_bias=ml_i_bias, ml_f_bias=ml_f_bias,
             ml_norm=ml_norm, hg_lbs=hg_lbs, hg_norm=hg_norm, gdn_conv_w=gdn_conv_w, gdn_a_log=gdn_a_log,
             gdn_dt_bias=gdn_dt_bias, gdn_norm=gdn_norm)
    w_in_p = _pad_w_in(w_in)
    w_out_b = w_out.astype(BF16)
    w_route = jnp.concatenate([moe_w_group, moe_w_router,
                               jnp.zeros((DEPTH, D_MODEL, 128 - N_GROUPS - N_EXPERTS), F32)], axis=-1)
    w_route_hi = w_route.astype(BF16)
    w_route_lo = (w_route - w_route_hi.astype(F32)).astype(BF16)
    w_route = jnp.concatenate([w_route_hi, w_route_lo], axis=-1)
    b_route = jnp.concatenate([moe_b_group, moe_b_router,
                               jnp.zeros((DEPTH, 128 - N_GROUPS - N_EXPERTS), F32)], axis=-1)

    mod = _ada(jnp.concatenate([c_prompt, c_sample], axis=0), ada_w, ada_b)
    sample_states = (state_rwkv_shift.reshape(DEPTH, ns, RWKV_PROJ), state_rwkv_wkv, state_mlstm_c,
                     state_mlstm_n, state_mlstm_m, state_hgrn,
                     state_gdn_conv.reshape(DEPTH, ns, (CONV_W - 1) * 3 * GW), state_gdn)

    xp = x_prompt.reshape(bp * t, D_MODEL)
    xs = x_sample.reshape(ns, D_MODEL)
    n_p = bp * t
    ff = None
    modp = mods = None
    new_p = []
    for l in range(DEPTH):
        lp = _layer_params(l, a)
        prev_modp, prev_mods = modp, mods
        modp = mod[l, :bp].reshape(bp, 1, 6 * D_MODEL)
        mods = mod[l, bp:].reshape(1, ns, 6 * D_MODEL)

        xp, pp = _in_proj(xp, modp, t, norm1[l], w_in_p[l], ff, 0, prev_modp)
        xs, ps = _in_proj(xs, mods, ns, norm1[l], w_in_p[l], ff, n_p, prev_mods)
        ysp, stp = _prompt_mixers(pp.reshape(bp, t, PROJ_PAD), lp)
        yss, sample_states = _sample_mixers(ps, sample_states, l, lp)
        new_p.append(stp)
        xp, xs, h2, rt = _out_proj(ysp, xp, modp, t, yss, xs, mods, norm2[l], w_out_b[l],
                                   w_route[l], b_route[l].reshape(1, 128))
        ff = _moe(h2, rt, n_p + ns, moe_w_gate, moe_w_up, moe_w_down, l)
    yp = _final(xp, ff, 0, modp, t, norm_f).reshape(bp, t, D_MODEL)
    ys = _final(xs, ff, n_p, mods, ns, norm_f).reshape(ns, 1, D_MODEL)
    prompt_states = tuple(jnp.stack([st[i] for st in new_p]) for i in range(8))
    s_sh, s_wkv, s_mc, s_mn, s_mm, s_hg, s_cv, s_gd = sample_states
    return (yp, ys) + prompt_states + (s_sh.reshape(DEPTH, ns, 1, RWKV_PROJ), s_wkv, s_mc, s_mn, s_mm, s_hg,
                                       s_cv.reshape(DEPTH, ns, CONV_W - 1, 3 * GW), s_gd)
```

```python
import functools
import math

import jax
import jax.numpy as jnp
from jax import lax
from jax.experimental import pallas as pl
from jax.experimental.pallas import tpu as pltpu

F32 = jnp.float32
BF16 = jnp.bfloat16
HI = lax.Precision.HIGHEST

D_MODEL = 2048
DEPTH = 4
GW = 512
RWKV_HD = 64
RWKV_H = 8
RWKV_PROJ = 1792
RWKV_LN_EPS = 64e-5
ML_H = 4
HG_H = 4
GDN_H = 4
HD = 128
CONV_W = 4
N_GROUPS = 4
EPG = 8
N_EXPERTS = 32
D_FF = 256
NORM_EPS = 1e-6

PROJ_PAD = 8192
GATES_OFF = 1792
GATES_W = 256
COL_ML_I, COL_ML_F, COL_GDN_B, COL_GDN_A = 0, 4, 8, 12

CHUNK = 64
MIX_NSEQ = 2
SUB = 16
VMEM_LIMIT = 56 * 1024 * 1024


def _cparams(sem):
    return pltpu.CompilerParams(dimension_semantics=sem, vmem_limit_bytes=VMEM_LIMIT)


def _mm(a, b):
    return jnp.dot(a.astype(BF16), b.astype(BF16), preferred_element_type=F32)


def _mm_nt(a, b):
    return lax.dot_general(a.astype(BF16), b.astype(BF16), (((1,), (1,)), ((), ())),
                           preferred_element_type=F32)


def _mm_tn(a, b):
    return lax.dot_general(a.astype(BF16), b.astype(BF16), (((0,), (0,)), ((), ())),
                           preferred_element_type=F32)


def _mm_hi(a, b):
    return jnp.dot(a, b, precision=HI, preferred_element_type=F32)


def _mm_nt_hi(a, b):
    return lax.dot_general(a, b, (((1,), (1,)), ((), ())), precision=HI, preferred_element_type=F32)


def _sigmoid(x):
    return 1.0 / (1.0 + jnp.exp(-x))


def _silu(x):
    return x * _sigmoid(x)


def _softplus(x):
    return jnp.maximum(x, 0.0) + jnp.log(1.0 + jnp.exp(-jnp.abs(x)))


def _iota(shape, dim):
    return lax.broadcasted_iota(jnp.int32, shape, dim)


def _eye(n, m):
    return (_iota((n, m), 0) == _iota((n, m), 1)).astype(F32)


def _lower(n, inclusive):
    r, c = _iota((n, n), 0), _iota((n, n), 1)
    return (c <= r) if inclusive else (c < r)


def _rms_rows(x):
    return x * lax.rsqrt(jnp.mean(x * x, axis=-1, keepdims=True) + NORM_EPS)


def _tri_inv_minus_eye(n_mat, size):
    m = -n_mat
    steps = int(math.log2(size))
    pw = _mm(n_mat, n_mat)
    yield
    for i in range(1, steps):
        t = _mm(m, pw)
        nxt = _mm(pw, pw) if i < steps - 1 else None
        yield
        m = m + pw + t
        pw = nxt
    return m


def _round_robin(gens):
    gens = list(gens)
    results = [None] * len(gens)
    active = list(range(len(gens)))
    while active:
        for i in list(active):
            try:
                next(gens[i])
            except StopIteration as stop:
                results[i] = stop.value
                active.remove(i)
    return results


def _ada_kernel(c_ref, w_ref, b_ref, o_ref):
    c = c_ref[...]
    o_ref[...] = _mm(_silu(c), w_ref[...]) + b_ref[...]


def _ada(c_all, ada_w, ada_b):
    depth, d, n6 = ada_w.shape
    rows = c_all.shape[0]
    tn = 1024
    return pl.pallas_call(
        _ada_kernel,
        grid=(depth, n6 // tn),
        in_specs=[pl.BlockSpec((rows, d), lambda l, j: (0, 0)),
                  pl.BlockSpec((None, d, tn), lambda l, j: (l, 0, j)),
                  pl.BlockSpec((None, 1, tn), lambda l, j: (l, 0, j))],
        out_specs=pl.BlockSpec((None, rows, tn), lambda l, j: (l, 0, j)),
        out_shape=jax.ShapeDtypeStruct((depth, rows, n6), F32),
        compiler_params=_cparams(("parallel", "parallel")),
        name="ada",
    )(c_all, ada_w, ada_b.reshape(depth, 1, n6))


def _modulated_norm(x, nw, sc, sh):
    return _rms_rows(x) * nw * (1.0 + sc) + sh


def _inproj_kernel(*refs, combine):
    if combine:
        x_ref, ff_ref, gt_ref, sh_ref, sc_ref, nw_ref, w_ref, xo_ref, p_ref, h_ref = refs
    else:
        x_ref, sh_ref, sc_ref, nw_ref, w_ref, p_ref, h_ref = refs

    @pl.when(pl.program_id(1) == 0)
    def _():
        x = x_ref[...]
        if combine:
            rows = x.shape[0]
            x = x + gt_ref[...] * (_load_slabs(ff_ref.at[0], rows) + _load_slabs(ff_ref.at[1], rows))
            xo_ref[...] = x
        h_ref[...] = _modulated_norm(x, nw_ref[...], sc_ref[...], sh_ref[...]).astype(BF16)

    p_ref[...] = jnp.dot(h_ref[...], w_ref[...], preferred_element_type=F32)


def _mod_spec(mod3, which, tm, rows_per_group):
    r = mod3.shape[1]
    return pl.BlockSpec((None, r, D_MODEL),
                        lambda i, *_: ((i * tm) // rows_per_group, 0, which))


def _in_proj(x, mod3, rows_per_group, nw, w, ff=None, ff_row0=0, gate_mod3=None):
    n = x.shape[0]
    tm = min(512, rows_per_group)
    tn = 1024
    combine = ff is not None
    row = lambda i, j: (i, 0)
    in_specs = [pl.BlockSpec((tm, D_MODEL), row)]
    args = [x]
    if combine:
        blk0 = ff_row0 // tm
        in_specs += [pl.BlockSpec((2, tm * SLAB, 128), lambda i, j: (0, blk0 + i, 0)),
                     _mod_spec(gate_mod3, 5, tm, rows_per_group)]
        args += [ff, gate_mod3]
    in_specs += [_mod_spec(mod3, 0, tm, rows_per_group), _mod_spec(mod3, 1, tm, rows_per_group),
                 pl.BlockSpec((1, D_MODEL), lambda i, j: (0, 0)),
                 pl.BlockSpec((D_MODEL, tn), lambda i, j: (0, j))]
    args += [mod3, mod3, nw.reshape(1, D_MODEL), w]
    out_specs = [pl.BlockSpec((tm, tn), lambda i, j: (i, j))]
    out_shape = [jax.ShapeDtypeStruct((n, PROJ_PAD), F32)]
    if combine:
        out_specs = [pl.BlockSpec((tm, D_MODEL), row)] + out_specs
        out_shape = [jax.ShapeDtypeStruct((n, D_MODEL), F32)] + out_shape
    res = pl.pallas_call(
        functools.partial(_inproj_kernel, combine=combine),
        grid=(n // tm, PROJ_PAD // tn),
        in_specs=in_specs, out_specs=out_specs, out_shape=out_shape,
        scratch_shapes=[pltpu.VMEM((tm, D_MODEL), BF16)],
        compiler_params=_cparams(("parallel", "arbitrary")),
        name="in_proj",
    )(*args)
    return (res[0], res[1]) if combine else (x, res[0])


def _lane_max(x):
    return jnp.max(x, axis=-1, keepdims=True)


def _lane_sum(x):
    return jnp.sum(x, axis=-1, keepdims=True)


def _first_lane_of(mask, lane):
    return jnp.min(jnp.where(mask, lane, 4096), axis=-1, keepdims=True)


def _route(logits, biased):
    lane = _iota(logits.shape, 1)
    neg = jnp.float32(-jnp.inf)
    is_g = lane < N_GROUPS
    gmax = _lane_max(jnp.where(is_g, logits, neg))
    gexp = jnp.where(is_g, jnp.exp(logits - gmax), 0.0)
    gb = jnp.where(is_g, biased, neg)
    gsel = _first_lane_of(gb == _lane_max(gb), lane)
    pg = _lane_sum(jnp.where(lane == gsel, gexp, 0.0)) / _lane_sum(gexp)
    lo = N_GROUPS + gsel * EPG
    eb = jnp.where((lane >= lo) & (lane < lo + EPG), biased, neg)
    i1 = _first_lane_of(eb == _lane_max(eb), lane)
    eb2 = jnp.where(lane == i1, neg, eb)
    i2 = _first_lane_of(eb2 == _lane_max(eb2), lane)
    l1 = _lane_sum(jnp.where(lane == i1, logits, 0.0))
    l2 = _lane_sum(jnp.where(lane == i2, logits, 0.0))
    mx = jnp.maximum(l1, l2)
    e1, e2 = jnp.exp(l1 - mx), jnp.exp(l2 - mx)
    cw1, cw2 = pg * e1 / (e1 + e2), pg * e2 / (e1 + e2)
    id1, id2 = (i1 - N_GROUPS).astype(F32), (i2 - N_GROUPS).astype(F32)
    return jnp.where(lane == 0, id1, jnp.where(lane == 1, id2,
                     jnp.where(lane == 2, cw1, jnp.where(lane == 3, cw2, 0.0))))


SLAB = D_MODEL // 128


def _load_slabs(ref, rows):
    return jnp.concatenate([ref[pl.ds(s, rows, stride=SLAB), :] for s in range(SLAB)], axis=1)


def _store_slabs(ref, x, rows):
    for s in range(SLAB):
        ref[pl.ds(s, rows, stride=SLAB), :] = x[:, s * 128:(s + 1) * 128]


def _outproj_kernel(*refs, n_prompt_tiles, tm, rows_s):
    group_p, group_s = refs[0:8], refs[8:16]
    nw_ref, wo_ref, wr_ref, br_ref, xop_ref, xos_ref, h2_ref, rt_ref = refs[16:]
    i = pl.program_id(0)

    def body(group, xo_ref, rows):
        yr_ref, ym_ref, yh_ref, yg_ref, x_ref, gt_ref, sh_ref, sc_ref = group
        mix = jnp.dot(yr_ref[...], wo_ref[0:GW, :], preferred_element_type=F32)
        mix += jnp.dot(ym_ref[...], wo_ref[GW:2 * GW, :], preferred_element_type=F32)
        mix += jnp.dot(yh_ref[...], wo_ref[2 * GW:3 * GW, :], preferred_element_type=F32)
        mix += jnp.dot(yg_ref[...], wo_ref[3 * GW:4 * GW, :], preferred_element_type=F32)
        x = x_ref[...] + gt_ref[...] * mix
        xo_ref[...] = x
        h2 = _modulated_norm(x, nw_ref[...], sc_ref[...], sh_ref[...])
        h_hi = h2.astype(BF16)
        h_lo = (h2 - h_hi.astype(F32)).astype(BF16)
        hw = jnp.dot(h_hi, wr_ref[...], preferred_element_type=F32)
        logits = hw[:, 0:128] + hw[:, 128:256] + jnp.dot(h_lo, wr_ref[:, 0:128], preferred_element_type=F32)
        route = _route(logits, logits + br_ref[...])
        _store_slabs(h2_ref, h2, rows)
        rt_ref[0:rows, :] = route
        if rows < tm:
            h2_ref[rows * SLAB:tm * SLAB, :] = jnp.zeros(((tm - rows) * SLAB, 128), F32)
            rt_ref[rows:tm, :] = jnp.zeros((tm - rows, 128), F32)

    @pl.when(i < n_prompt_tiles)
    def _():
        body(group_p, xop_ref, tm)

    @pl.when(i == n_prompt_tiles)
    def _():
        body(group_s, xos_ref, rows_s)


def _out_proj(ys_p, x_p, mod_p, t, ys_s, x_s, mod_s, nw, wo, w_route, b_route):
    n_p, n_s = x_p.shape[0], x_s.shape[0]
    tm = min(256, t)
    assert n_s <= tm and n_p % tm == 0
    n_pt = n_p // tm
    prow = lambda i: (jnp.minimum(i, n_pt - 1), 0)
    const = lambda i: (0, 0)
    pmod = lambda which: pl.BlockSpec((None, 1, D_MODEL),
                                      lambda i: ((jnp.minimum(i, n_pt - 1) * tm) // t, 0, which))
    smod = lambda which: pl.BlockSpec((None, n_s, D_MODEL), lambda i: (0, 0, which))
    in_specs = ([pl.BlockSpec((tm, GW), prow)] * 4 + [pl.BlockSpec((tm, D_MODEL), prow), pmod(2), pmod(3), pmod(4)]
                + [pl.BlockSpec((n_s, GW), const)] * 4 + [pl.BlockSpec((n_s, D_MODEL), const),
                                                          smod(2), smod(3), smod(4)]
                + [pl.BlockSpec((1, D_MODEL), const), pl.BlockSpec((D_MODEL, D_MODEL), const),
                   pl.BlockSpec((D_MODEL, 256), const), pl.BlockSpec((1, 128), const)])
    n_all = n_p + tm
    return pl.pallas_call(
        functools.partial(_outproj_kernel, n_prompt_tiles=n_pt, tm=tm, rows_s=n_s),
        grid=(n_pt + 1,),
        in_specs=in_specs,
        out_specs=[pl.BlockSpec((tm, D_MODEL), prow), pl.BlockSpec((n_s, D_MODEL), const),
                   pl.BlockSpec((tm * SLAB, 128), lambda i: (i, 0)), pl.BlockSpec((tm, 128), lambda i: (i, 0))],
        out_shape=[jax.ShapeDtypeStruct((n_p, D_MODEL), F32), jax.ShapeDtypeStruct((n_s, D_MODEL), F32),
                   jax.ShapeDtypeStruct((n_all * SLAB, 128), F32), jax.ShapeDtypeStruct((n_all, 128), F32)],
        compiler_params=_cparams(("arbitrary",)),
        name="out_proj",
    )(*ys_p, x_p, mod_p, mod_p, mod_p, *ys_s, x_s, mod_s, mod_s, mod_s,
      nw.reshape(1, D_MODEL), wo, w_route, b_route)


def _moe_kernel(te_ref, tok_ref, dst_ref, h2_hbm, cw_ref, wg_ref, wu_ref, wd_ref, out_hbm,
                hbuf0, hbuf1, obuf0, obuf1, wgb, wub, wdb, gsem, ssem, *, tm, n_tiles):
    t = pl.program_id(0)
    hbufs, obufs = (hbuf0, hbuf1), (obuf0, obuf1)

    def start_gather(tile, slot):
        for r in range(tm):
            src = pl.multiple_of(tok_ref[tile * tm + r], SLAB)
            pltpu.make_async_copy(h2_hbm.at[pl.ds(src, SLAB)], hbufs[slot].at[pl.ds(r * SLAB, SLAB)],
                                  gsem.at[slot]).start()

    def start_scatter(tile, slot):
        for r in range(tm):
            dst = pl.multiple_of(dst_ref[tile * tm + r], SLAB)
            pltpu.make_async_copy(obufs[slot].at[pl.ds(r * SLAB, SLAB)], out_hbm.at[pl.ds(dst, SLAB)],
                                  ssem.at[slot]).start(priority=1)

    def wait_gather(slot):
        pltpu.make_async_copy(h2_hbm.at[pl.ds(0, tm * SLAB)], hbufs[slot], gsem.at[slot]).wait()

    def wait_scatter(slot):
        pltpu.make_async_copy(obufs[slot], out_hbm.at[pl.ds(0, tm * SLAB)], ssem.at[slot]).wait()

    @pl.when(t == 0)
    def _():
        start_gather(0, 0)

    @pl.when((t == 0) | (te_ref[t] != te_ref[jnp.maximum(t - 1, 0)]))
    def _():
        wgb[...] = wg_ref[...].astype(BF16)
        wub[...] = wu_ref[...].astype(BF16)
        wdb[...] = wd_ref[...].astype(BF16)

    def step(slot):
        wait_gather(slot)
        start_gather(t + 1, 1 - slot)
        h = _load_slabs(hbufs[slot], tm).astype(BF16)
        g = jnp.dot(h, wgb[...], preferred_element_type=F32)
        u = jnp.dot(h, wub[...], preferred_element_type=F32)
        hid = (_silu(g) * u * cw_ref[...]).astype(BF16)
        _store_slabs(obufs[slot], jnp.dot(hid, wdb[...], preferred_element_type=F32), tm)
        start_scatter(t, slot)

    for slot in (0, 1):
        @pl.when(t % 2 == slot)
        def _(slot=slot):
            step(slot)

        @pl.when((t >= 1) & (t % 2 == slot))
        def _(slot=slot):
            wait_scatter(1 - slot)

    @pl.when(t == n_tiles - 1)
    def _():
        wait_scatter((n_tiles - 1) % 2)
        wait_gather(n_tiles % 2)


MOE_TM = 128


def _moe(h2, route, n_tok, w_gate, w_up, w_down, layer):
    tm = MOE_TM
    slots = 2 * n_tok
    n_tiles = -(-(slots + N_EXPERTS * (tm - 1)) // tm) + 2
    n_pad = n_tok + tm
    n_rows = (n_tiles + 1) * tm
    eid = jnp.concatenate([route[:n_tok, 0], route[:n_tok, 1]]).astype(jnp.int32)
    cw = jnp.concatenate([route[:n_tok, 2], route[:n_tok, 3]])
    onehot = (eid[:, None] == jnp.arange(N_EXPERTS, dtype=jnp.int32)[None, :]).astype(jnp.int32)
    n_blk = -(-slots // tm)
    oh3 = jnp.pad(onehot, ((0, n_blk * tm - slots), (0, 0))).reshape(n_blk, tm, N_EXPERTS).astype(BF16)
    tri = (jnp.arange(tm)[:, None] >= jnp.arange(tm)[None, :]).astype(BF16)
    within = jnp.einsum("ij,bjk->bik", tri, oh3, preferred_element_type=F32).astype(jnp.int32)
    blk_tot = within[:, -1, :]
    blk_off = jnp.cumsum(blk_tot, axis=0) - blk_tot
    running = (within + blk_off[:, None, :]).reshape(n_blk * tm, N_EXPERTS)[:slots]
    counts = jnp.sum(blk_tot, axis=0)
    tiles_per = (counts + tm - 1) // tm
    tile_end = jnp.cumsum(tiles_per)
    tile_start = tile_end - tiles_per
    rank = jnp.sum(running * onehot, axis=1) - 1
    pos = jnp.sum(onehot * tile_start[None, :], axis=1) * tm + rank
    s_idx = jnp.arange(slots, dtype=jnp.int32)
    tok = s_idx % n_tok
    dst = (s_idx // n_tok) * n_pad + tok
    j = jnp.arange(n_rows, dtype=jnp.int32)
    dump = ((j // tm) % 2) * n_pad + n_tok + j % tm
    default = jnp.stack([jnp.zeros_like(j), dump, jnp.zeros_like(j)], axis=1).astype(F32)
    packed = jnp.stack([tok.astype(F32), dst.astype(F32), cw], axis=1)
    rows = default.at[pos].set(packed)
    row_tok = rows[:, 0].astype(jnp.int32) * SLAB
    row_dst = rows[:, 1].astype(jnp.int32) * SLAB
    row_cw = rows[:, 2].reshape(n_rows, 1)
    tile_ids = jnp.arange(n_tiles, dtype=jnp.int32)
    tile_e = jnp.sum((tile_end[None, :] <= tile_ids[:, None]).astype(jnp.int32), axis=1)
    tile_e = jnp.minimum(tile_e, N_EXPERTS - 1)

    wspec = lambda shape: pl.BlockSpec((None, None) + shape, lambda t, te, *_: (layer, te[t], 0, 0))
    grid_spec = pltpu.PrefetchScalarGridSpec(
        num_scalar_prefetch=3,
        grid=(n_tiles,),
        in_specs=[pl.BlockSpec(memory_space=pl.ANY),
                  pl.BlockSpec((tm, 1), lambda t, *_: (t, 0)),
                  wspec((D_MODEL, D_FF)), wspec((D_MODEL, D_FF)), wspec((D_FF, D_MODEL))],
        out_specs=pl.BlockSpec(memory_space=pl.ANY),
        scratch_shapes=[pltpu.VMEM((tm * SLAB, 128), F32)] * 4
        + [pltpu.VMEM((D_MODEL, D_FF), BF16), pltpu.VMEM((D_MODEL, D_FF), BF16),
           pltpu.VMEM((D_FF, D_MODEL), BF16),
           pltpu.SemaphoreType.DMA((2,)), pltpu.SemaphoreType.DMA((2,))])
    out = pl.pallas_call(
        functools.partial(_moe_kernel, tm=tm, n_tiles=n_tiles),
        grid_spec=grid_spec,
        out_shape=jax.ShapeDtypeStruct((2 * n_pad * SLAB, 128), F32),
        compiler_params=_cparams(("arbitrary",)),
        name="moe",
    )(tile_e, row_tok, row_dst, h2, row_cw, w_gate, w_up, w_down)
    return out.reshape(2, n_pad * SLAB, 128)


def _final_kernel(x_ref, ff_ref, gt_ref, nw_ref, o_ref):
    rows = x_ref.shape[0]
    x = x_ref[...] + gt_ref[...] * (_load_slabs(ff_ref.at[0], rows) + _load_slabs(ff_ref.at[1], rows))
    o_ref[...] = _rms_rows(x) * nw_ref[...]


def _final(x, ff, ff_row0, mod3, rows_per_group, nw):
    n = x.shape[0]
    tm = min(512, rows_per_group)
    blk0 = ff_row0 // tm
    return pl.pallas_call(
        _final_kernel,
        grid=(n // tm,),
        in_specs=[pl.BlockSpec((tm, D_MODEL), lambda i: (i, 0)),
                  pl.BlockSpec((2, tm * SLAB, 128), lambda i: (0, blk0 + i, 0)),
                  _mod_spec(mod3, 5, tm, rows_per_group),
                  pl.BlockSpec((1, D_MODEL), lambda i: (0, 0))],
        out_specs=pl.BlockSpec((tm, D_MODEL), lambda i: (i, 0)),
        out_shape=jax.ShapeDtypeStruct((n, D_MODEL), F32),
        compiler_params=_cparams(("parallel",)),
        name="final_norm",
    )(x, ff, mod3, nw.reshape(1, D_MODEL))


def _seg_sum(x, width):
    seg = (_iota((128, 128), 0) // width == _iota((128, 128), 1) // width).astype(F32)
    parts = [_mm_hi(x[:, j:j + 128], seg) for j in range(0, x.shape[1], 128)]
    return parts[0] if len(parts) == 1 else jnp.concatenate(parts, axis=1)


def _head_norm(o, nw, gate):
    return o * lax.rsqrt(_seg_sum(o * o, HD) * (1.0 / HD) + NORM_EPS) * nw * gate


def _carry_rows(ext_ref, cur, first, L):
    @pl.when(first)
    def _():
        ext_ref[0:8, :] = jnp.zeros((8, ext_ref.shape[1]), F32)

    ext_ref[8:8 + L, :] = cur


def _rwkv_parts(p_ref, mu_ref, vec_ref, wup_ref, aup_ref, gup_ref, y_ref, sh_ref, s_ref, ext_ref, L):
    c = pl.program_id(1)
    p = p_ref[...]
    _carry_rows(ext_ref, p, c == 0, L)
    prev = ext_ref[7:7 + L, :]
    xs = p + (prev - p) * mu_ref[...]
    ext_ref[0:8, :] = ext_ref[L:L + 8, :]
    sh_ref[...] = p[L - 1:L, :]

    @pl.when(c == 0)
    def _():
        s_ref[...] = jnp.zeros_like(s_ref)

    w0, a0, k_k, k_a, r_k, ln_w, ln_b = [vec_ref[i:i + 1, :] for i in range(7)]
    r, k, v = xs[:, 0:GW], xs[:, GW:2 * GW], xs[:, 2 * GW:3 * GW]
    dw, da, dg = xs[:, 1536:1600], xs[:, 1600:1664], xs[:, 1664:1792]
    logw = -_softplus(-(w0 + _mm(jnp.tanh(dw), wup_ref[...]))) - 0.5
    lw = -jnp.exp(logw)
    a = _sigmoid(a0 + _mm(da, aup_ref[...]))
    g = _mm(_sigmoid(dg), gup_ref[...])
    kk = k * k_k
    kk = kk * lax.rsqrt(jnp.maximum(_seg_sum(kk * kk, RWKV_HD), 1e-12))
    k2 = k * (1.0 + (a - 1.0) * k_a)
    cl = _mm_hi(_lower(L, True).astype(F32), lw)
    cl_last = cl[L - 1:L, :]
    at = -kk * jnp.exp(cl - lw)
    bt = kk * a * jnp.exp(-cl)
    kt = k2 * jnp.exp(-cl)
    rt = r * jnp.exp(cl)
    b_end = kk * a * jnp.exp(cl_last - cl)
    k_end = k2 * jnp.exp(cl_last - cl)
    strict, incl = _lower(L, False), _lower(L, True)
    def head(h):
        sl = slice(h * RWKV_HD, (h + 1) * RWKV_HD)
        s0 = s_ref[h]
        ath, bth, kth, rth, vh = at[:, sl], bt[:, sl], kt[:, sl], rt[:, sl], v[:, sl]
        g_ab, g_ak, g_rb, g_rk = _mm_nt(ath, bth), _mm_nt(ath, kth), _mm_nt(rth, bth), _mm_nt(rth, kth)
        a_s0, r_s0 = _mm_nt(ath, s0), _mm_nt(rth, s0)
        s_v = _mm_tn(vh, k_end[:, sl])
        yield
        n_ab = jnp.where(strict, g_ab, 0.0)
        ak_v = _mm(jnp.where(strict, g_ak, 0.0), vh)
        rk_v = _mm(jnp.where(incl, g_rk, 0.0), vh)
        m = yield from _tri_inv_minus_eye(-n_ab, L)
        rhs = a_s0 + ak_v
        m_rhs = _mm(m, rhs)
        yield
        u = rhs + m_rhs
        rb_u = _mm(jnp.where(incl, g_rb, 0.0), u)
        s_u = _mm_tn(u, b_end[:, sl])
        yield
        s_ref[h] = s0 * jnp.exp(cl_last[:, sl]) + s_u + s_v
        return r_s0 + rb_u + rk_v

    def finish(ys):
        y = jnp.concatenate(ys, axis=1)
        mean = _seg_sum(y, RWKV_HD) * (1.0 / RWKV_HD)
        dev = y - mean
        var = _seg_sum(dev * dev, RWKV_HD) * (1.0 / RWKV_HD)
        y = dev * lax.rsqrt(var + RWKV_LN_EPS) * ln_w + ln_b
        bonus = _seg_sum(r * k2 * r_k, RWKV_HD) * v
        y_ref[...] = ((y + bonus) * g).astype(y_ref.dtype)

    return [head(h) for h in range(RWKV_H)], finish


def _gate_forms(g_ref, gp_ref, L):
    raw = g_ref[:, 0:128] + gp_ref[0:1, :]
    lane = _iota(raw.shape, 1)
    lf = jnp.minimum(raw, 0.0) - jnp.log(1.0 + jnp.exp(-jnp.abs(raw)))
    beta = _sigmoid(raw)
    decay = -jnp.exp(gp_ref[1:2, :]) * _softplus(raw)
    cols = jnp.where(lane < 4, raw, jnp.where(lane < 8, lf, jnp.where(lane < 12, beta, decay)))
    rows = _mm_nt_hi(_eye(16, 128), cols)
    ccols = _mm_hi(_lower(L, True).astype(F32), cols)
    upper = (_iota((L, L), 0) <= _iota((L, L), 1)).astype(F32)
    crows = _mm_hi(rows, upper)
    return cols, rows, ccols, crows


def _mlstm_parts(p_ref, gate_forms, nw_ref, y_ref, c_ref, n_ref, m_ref, L):
    c = pl.program_id(1)

    @pl.when(c == 0)
    def _():
        c_ref[...] = jnp.zeros_like(c_ref)
        n_ref[...] = jnp.zeros_like(n_ref)
        m_ref[...] = jnp.zeros_like(m_ref)

    cols, rows, ccols, crows = gate_forms
    causal = _lower(L, True)
    neg = jnp.float32(-jnp.inf)
    def head(h):
        q = p_ref[:, h * HD:(h + 1) * HD]
        k = p_ref[:, GW + h * HD:GW + (h + 1) * HD] * (HD ** -0.5)
        v = p_ref[:, 2 * GW + h * HD:2 * GW + (h + 1) * HD]
        ig_col, ig_row = cols[:, h:h + 1], rows[h:h + 1, :]
        b_col, b_row = ccols[:, 4 + h:5 + h], crows[4 + h:5 + h, :]
        m_prev = m_ref[:, h:h + 1]
        cmat, nvec = c_ref[h], n_ref[h:h + 1, :]
        qk, qc = _mm_nt(q, k), _mm_nt(q, cmat)
        dmat = jnp.where(causal, b_col - b_row + ig_row, neg)
        gcol = b_col + m_prev
        mt = jnp.maximum(gcol, _lane_max(dmat))
        m_last = mt[L - 1:L, :]
        wk = jnp.exp(b_col[L - 1:L, :] - b_col + ig_col - m_last)
        decay = jnp.exp(gcol[L - 1:L, :] - m_last)
        c_upd = _mm_tn(v * wk, k)
        yield
        smat = qk * jnp.exp(dmat - mt)
        s_v = _mm(smat, v)
        wg = jnp.exp(gcol - mt)
        den = _lane_sum(smat) + wg * _lane_sum(q * nvec)
        c_ref[h] = decay * cmat + c_upd
        n_ref[h:h + 1, :] = decay * nvec + jnp.sum(k * wk, axis=0, keepdims=True)
        m_ref[:, h:h + 1] = m_last
        yield
        return (s_v + wg * qc) / jnp.maximum(jnp.abs(den), jnp.exp(-mt))

    def finish(ys):
        o = p_ref[:, 3 * GW:4 * GW]
        y_ref[...] = _head_norm(jnp.concatenate(ys, axis=1), nw_ref[...], _sigmoid(o)).astype(y_ref.dtype)

    return [head(h) for h in range(ML_H)], finish


def _hgrn_parts(p_ref, lb_ref, nw_ref, y_ref, s_ref, L):
    c = pl.program_id(1)

    @pl.when(c == 0)
    def _():
        s_ref[...] = jnp.zeros_like(s_ref)

    lb = lb_ref[...]
    qa = _silu(p_ref[:, 0:GW])
    fg = lb + (1.0 - lb) * _sigmoid(p_ref[:, GW:2 * GW])
    ka = 1.0 - fg
    va = p_ref[:, 2 * GW:3 * GW]
    cga = _mm_hi(_lower(L, True).astype(F32), jnp.log(fg))
    ones = jnp.ones((HD, HD), BF16)
    t3, s3 = _iota((SUB, SUB, HD), 0), _iota((SUB, SUB, HD), 1)
    neg = jnp.float32(-jnp.inf)
    def head(h):
        sl = slice(h * HD, (h + 1) * HD)
        q, k, v, cg = qa[:, sl], ka[:, sl], va[:, sl], cga[:, sl]
        st = s_ref[h]
        inter = _mm_nt(q * jnp.exp(cg), st)
        cg_last = cg[L - 1:L, :]
        s_upd = _mm_tn(v, k * jnp.exp(cg_last - cg))
        a3s, a_offs = [], []
        for i in range(L // SUB):
            lo = i * SUB
            qi, ki, cgi = q[lo:lo + SUB], k[lo:lo + SUB], cg[lo:lo + SUB]
            e3 = jnp.exp(jnp.where(s3 <= t3, cgi[:, None, :] - cgi[None, :, :], neg))
            x3 = qi[:, None, :] * ki[None, :, :] * e3
            a3s.append(jnp.dot(x3.reshape(SUB * SUB, HD).astype(BF16), ones, preferred_element_type=F32))
            if i > 0:
                ref = cg[lo - 1:lo, :]
                a_offs.append(_mm_nt(qi * jnp.exp(cgi - ref), k[0:lo] * jnp.exp(ref - cg[0:lo])))
        yield
        s_ref[h] = st * jnp.exp(cg_last) + s_upd
        offs = [_mm(a_off, v[0:(i + 1) * SUB]) for i, a_off in enumerate(a_offs)]
        yield
        rows_out = []
        for i in range(L // SUB):
            lo = i * SUB
            oi = inter[lo:lo + SUB] + jnp.sum(a3s[i].reshape(SUB, SUB, HD) * v[lo:lo + SUB][None, :, :], axis=1)
            rows_out.append(oi if i == 0 else oi + offs[i - 1])
        return jnp.concatenate(rows_out, axis=0)

    def finish(ys):
        g = p_ref[:, 3 * GW:4 * GW]
        y_ref[...] = _head_norm(jnp.concatenate(ys, axis=1), nw_ref[...], _silu(g)).astype(y_ref.dtype)

    return [head(h) for h in range(HG_H)], finish


def _gdn_parts(p_ref, gate_forms, cw_ref, nw_ref, y_ref, cv_ref, s_ref, ext_ref, L):
    c = pl.program_id(1)
    w3 = 3 * GW
    _carry_rows(ext_ref, p_ref[:, 0:w3], c == 0, L)
    conv = (cw_ref[3:4, :] * ext_ref[8:8 + L, :] + cw_ref[2:3, :] * ext_ref[7:7 + L, :]
            + cw_ref[1:2, :] * ext_ref[6:6 + L, :] + cw_ref[0:1, :] * ext_ref[5:5 + L, :])
    cv_ref[...] = ext_ref[L + 5:L + 8, :]
    ext_ref[0:8, :] = ext_ref[L:L + 8, :]

    @pl.when(c == 0)
    def _():
        s_ref[...] = jnp.zeros_like(s_ref)

    qkv = _silu(conv)
    cols, rows, ccols, crows = gate_forms
    strict, causal = _lower(L, False), _lower(L, True)
    neg = jnp.float32(-jnp.inf)
    qa = qkv[:, 0:GW]
    ka = qkv[:, GW:2 * GW]
    qa = qa * lax.rsqrt(jnp.maximum(_seg_sum(qa * qa, HD), 1e-12)) * (HD ** -0.5)
    ka = ka * lax.rsqrt(jnp.maximum(_seg_sum(ka * ka, HD), 1e-12))

    def head(h):
        sl = slice(h * HD, (h + 1) * HD)
        q, k, v = qa[:, sl], ka[:, sl], qkv[:, 2 * GW + h * HD:2 * GW + (h + 1) * HD]
        beta = cols[:, COL_GDN_B + h:COL_GDN_B + h + 1]
        cg_col = ccols[:, COL_GDN_A + h:COL_GDN_A + h + 1]
        cg_row = crows[COL_GDN_A + h:COL_GDN_A + h + 1, :]
        diff = cg_col - cg_row
        s0 = s_ref[h]
        kb = k * beta
        g_kk, g_qk = _mm_nt(kb, k), _mm_nt(q, k)
        q_s0 = _mm(q * jnp.exp(cg_col), s0)
        yield
        n_mat = g_kk * jnp.exp(jnp.where(strict, diff, neg))
        m = yield from _tri_inv_minus_eye(n_mat, L)
        rhs = jnp.concatenate([v * beta, kb * jnp.exp(cg_col)], axis=1)
        m_rhs = _mm(m, rhs)
        yield
        uw = rhs + m_rhs
        w_s0 = _mm(uw[:, HD:2 * HD], s0)
        yield
        u2 = uw[:, 0:HD] - w_s0
        qk = g_qk * jnp.exp(jnp.where(causal, diff, neg))
        qk_u = _mm(qk, u2)
        cg_last = cg_col[L - 1:L, :]
        s_upd = _mm_tn(k * jnp.exp(cg_last - cg_col), u2)
        yield
        s_ref[h] = jnp.exp(cg_last) * s0 + s_upd
        return q_s0 + qk_u

    def finish(ys):
        g = p_ref[:, w3:w3 + GW]
        y_ref[...] = _head_norm(jnp.concatenate(ys, axis=1), nw_ref[...], _silu(g)).astype(y_ref.dtype)

    return [head(h) for h in range(GDN_H)], finish


def _mixers_kernel(pr_ref, pg_ref, pm_ref, ph_ref, pd_ref,
                   mu_ref, vec_ref, wup_ref, aup_ref, gup_ref, gp_ref, mlw_ref, lb_ref, hgw_ref, cw_ref, gdw_ref,
                   yr_ref, sh_ref, wkv_ref, ym_ref, mc_ref, mn_ref, mm_ref, yh_ref, hg_ref, yg_ref, cv_ref, gd_ref,
                   ext_r, ext_g, *, L, nseq):
    parts = []
    for b in range(nseq):
        gate_forms = _gate_forms(pg_ref.at[b], gp_ref, L)
        parts += [
            _rwkv_parts(pr_ref.at[b], mu_ref, vec_ref, wup_ref, aup_ref, gup_ref, yr_ref.at[b], sh_ref.at[b],
                        wkv_ref.at[b], ext_r.at[b], L),
            _mlstm_parts(pm_ref.at[b], gate_forms, mlw_ref, ym_ref.at[b], mc_ref.at[b], mn_ref.at[b],
                         mm_ref.at[b], L),
            _hgrn_parts(ph_ref.at[b], lb_ref, hgw_ref, yh_ref.at[b], hg_ref.at[b], L),
            _gdn_parts(pd_ref.at[b], gate_forms, cw_ref, gdw_ref, yg_ref.at[b], cv_ref.at[b], gd_ref.at[b],
                       ext_g.at[b], L),
        ]
    results = _round_robin(g for gens, _ in parts for g in gens)
    at = 0
    for gens, finish in parts:
        finish(results[at:at + len(gens)])
        at += len(gens)


def _prompt_mixers(p3, lp):
    b, t, _ = p3.shape
    L = math.gcd(t, CHUNK)
    nseq = math.gcd(b, MIX_NSEQ)
    grid = (b // nseq, t // L)
    cp = _cparams(("parallel", "arbitrary"))
    col = lambda width, idx: pl.BlockSpec((nseq, L, width), lambda i, c: (i, c, idx))
    const2 = lambda shape: pl.BlockSpec(shape, lambda i, c: (0, 0))
    y_spec = pl.BlockSpec((nseq, L, GW), lambda i, c: (i, c, 0))
    y_shape = jax.ShapeDtypeStruct((b, t, GW), BF16)
    state = lambda *s: (pl.BlockSpec((nseq,) + s, lambda i, c: (i,) + (0,) * len(s)),
                        jax.ShapeDtypeStruct((b,) + s, F32))
    gates = col(GATES_W, GATES_OFF // GATES_W)
    y_out = (y_spec, y_shape)
    outs = [y_out, state(1, RWKV_PROJ), state(RWKV_H, RWKV_HD, RWKV_HD),
            y_out, state(ML_H, HD, HD), state(ML_H, HD), state(1, 128),
            y_out, state(HG_H, HD, HD),
            y_out, state(CONV_W - 1, 3 * GW), state(GDN_H, HD, HD)]
    specs, shapes = zip(*outs)
    yr, n_sh, n_wkv, ym, n_c, n_n, n_m, yh, n_hg, yg, n_cv, n_gd = pl.pallas_call(
        functools.partial(_mixers_kernel, L=L, nseq=nseq), grid=grid,
        in_specs=[col(RWKV_PROJ, 0), gates, col(4 * GW, 1), col(4 * GW, 2), col(4 * GW, 3),
                  const2((1, RWKV_PROJ)), const2((8, GW)), const2((64, GW)), const2((64, GW)), const2((128, GW)),
                  const2((8, 128)), const2((1, GW)), const2((1, GW)), const2((1, GW)),
                  const2((CONV_W, 3 * GW)), const2((1, GW))],
        out_specs=list(specs), out_shape=list(shapes),
        scratch_shapes=[pltpu.VMEM((nseq, L + 8, RWKV_PROJ), F32), pltpu.VMEM((nseq, L + 8, 3 * GW), F32)],
        compiler_params=cp, name="mixers",
    )(p3, p3, p3, p3, p3, lp["rwkv_mu"], lp["rwkv_vec"], lp["rwkv_w_up"], lp["rwkv_a_up"], lp["rwkv_g_up"],
      lp["gate_par"], lp["ml_norm"], lp["hg_lb"], lp["hg_norm"], lp["gdn_conv_w"], lp["gdn_norm"])

    ys = [y.reshape(b * t, GW) for y in (yr, ym, yh, yg)]
    states = (n_sh, n_wkv, n_c, n_n, n_m[:, 0, :ML_H], jnp.swapaxes(n_hg, -1, -2), n_cv, n_gd)
    return ys, states


DEC_BS = 8


def _col_to_row(col):
    n = col.shape[0]
    return jnp.sum(jnp.where(_iota((n, n), 0) == _iota((n, n), 1), col, 0.0), axis=0, keepdims=True)


def _sub_sum(x):
    return jnp.sum(x, axis=0, keepdims=True)


def _decode_kernel(p_ref, sh_ref, wkv_ref, mc_ref, mn_ref, mm_ref, hg_ref, cv_ref, gd_ref,
                   mu_ref, vec_ref, wup_ref, aup_ref, gup_ref, gp_ref, mlw_ref, lb_ref, hgw_ref,
                   cw_ref, gdw_ref,
                   yr_ref, ym_ref, yh_ref, yg_ref,
                   nsh_ref, nwkv_ref, nmc_ref, nmn_ref, nmm_ref, nhg_ref, ncv_ref, ngd_ref):
    bs = DEC_BS
    w3 = 3 * GW
    pr = p_ref[:, 0:RWKV_PROJ]
    xs = pr + (sh_ref[...] - pr) * mu_ref[...]
    nsh_ref[...] = pr
    w0, a0, k_k, k_a, r_k, ln_w, ln_b = [vec_ref[i:i + 1, :] for i in range(7)]
    r, k, v = xs[:, 0:GW], xs[:, GW:2 * GW], xs[:, 2 * GW:3 * GW]
    dw, da, dg = xs[:, 1536:1600], xs[:, 1600:1664], xs[:, 1664:1792]
    logw = -_softplus(-(w0 + _mm(jnp.tanh(dw), wup_ref[...]))) - 0.5
    wdec = jnp.exp(-jnp.exp(logw))
    a = _sigmoid(a0 + _mm(da, aup_ref[...]))
    g_r = _mm(_sigmoid(dg), gup_ref[...])
    kk = k * k_k
    kk = kk * lax.rsqrt(jnp.maximum(_seg_sum(kk * kk, RWKV_HD), 1e-12))
    k2 = k * (1.0 + (a - 1.0) * k_a)
    ra, rb = -kk, kk * a

    gates = p_ref[:, GATES_OFF:GATES_OFF + 128] + gp_ref[0:1, :]
    ml_off = GATES_OFF + GATES_W
    mq = p_ref[:, ml_off:ml_off + GW]
    mk = p_ref[:, ml_off + GW:ml_off + 2 * GW] * (HD ** -0.5)
    mv = p_ref[:, ml_off + 2 * GW:ml_off + 3 * GW]
    mo = p_ref[:, ml_off + 3 * GW:ml_off + 4 * GW]
    m_ig = gates
    m_lf = jnp.minimum(gates, 0.0) - jnp.log(1.0 + jnp.exp(-jnp.abs(gates)))

    hg_off = ml_off + 4 * GW
    lb = lb_ref[...]
    hq = _silu(p_ref[:, hg_off:hg_off + GW])
    hfg = lb + (1.0 - lb) * _sigmoid(p_ref[:, hg_off + GW:hg_off + 2 * GW])
    hv = p_ref[:, hg_off + 2 * GW:hg_off + 3 * GW]
    hgate = p_ref[:, hg_off + 3 * GW:hg_off + 4 * GW]

    gd_off = hg_off + 4 * GW
    cur = p_ref[:, gd_off:gd_off + w3]
    conv = (cw_ref[3:4, :] * cur + cw_ref[2:3, :] * cv_ref[:, 2 * w3:3 * w3]
            + cw_ref[1:2, :] * cv_ref[:, w3:2 * w3] + cw_ref[0:1, :] * cv_ref[:, 0:w3])
    ncv_ref[:, 0:2 * w3] = cv_ref[:, w3:3 * w3]
    ncv_ref[:, 2 * w3:3 * w3] = cur
    qkv = _silu(conv)
    gq, gk, gv = qkv[:, 0:GW], qkv[:, GW:2 * GW], qkv[:, 2 * GW:3 * GW]
    gq = gq * lax.rsqrt(jnp.maximum(_seg_sum(gq * gq, HD), 1e-12)) * (HD ** -0.5)
    gk = gk * lax.rsqrt(jnp.maximum(_seg_sum(gk * gk, HD), 1e-12))
    ggate = p_ref[:, gd_off + w3:gd_off + w3 + GW]
    g_beta = _sigmoid(gates)
    g_dec = jnp.exp(-jnp.exp(gp_ref[1:2, :]) * _softplus(gates))

    blocks = ([mv[:, h * HD:(h + 1) * HD] for h in range(ML_H)]
              + [hq[:, h * HD:(h + 1) * HD] for h in range(HG_H)]
              + [hfg[:, h * HD:(h + 1) * HD] for h in range(HG_H)]
              + [gq[:, h * HD:(h + 1) * HD] for h in range(GDN_H)]
              + [gk[:, h * HD:(h + 1) * HD] for h in range(GDN_H)]
              + [v[:, j * 128:(j + 1) * 128] for j in range(4)])
    xt = _mm_nt_hi(_eye(128, 128), jnp.concatenate(blocks, axis=0))
    colf = lambda blk, s: xt[:, blk * bs + s:blk * bs + s + 1]

    yr_rows, ym_rows, yh_rows, yg_rows = [], [], [], []
    for s in range(bs):
        row = lambda arr, lo, width: arr[s:s + 1, lo:lo + width]
        parts = []
        for h in range(RWKV_H):
            lo = h * RWKV_HD
            st = wkv_ref[s, h]
            v_col = colf(20 + h // 2, s)[(h % 2) * 64:(h % 2) * 64 + 64, :]
            sa = _lane_sum(st * row(ra, lo, 64))
            st = st * row(wdec, lo, 64) + sa * row(rb, lo, 64) + v_col * row(k2, lo, 64)
            nwkv_ref[s, h] = st
            parts.append(_col_to_row(_lane_sum(st * row(r, lo, 64))))
        yr_rows.append(jnp.concatenate(parts, axis=1))
        parts = []
        for h in range(ML_H):
            lo = h * HD
            q_r, k_r = row(mq, lo, HD), row(mk, lo, HD)
            ig, lf = m_ig[s:s + 1, COL_ML_I + h:COL_ML_I + h + 1], m_lf[s:s + 1, COL_ML_F + h:COL_ML_F + h + 1]
            cmat, nvec, m_prev = mc_ref[s, h], mn_ref[s, h:h + 1, :], mm_ref[s:s + 1, h:h + 1]
            gsc = lf + m_prev
            mt = jnp.maximum(gsc, ig)
            wi, wg = jnp.exp(ig - mt), jnp.exp(gsc - mt)
            sc = _lane_sum(q_r * k_r) * wi
            v_col = colf(h, s)
            num = sc * v_col + wg * _lane_sum(cmat * q_r)
            den = sc + wg * _lane_sum(nvec * q_r)
            parts.append(_col_to_row(num / jnp.maximum(jnp.abs(den), jnp.exp(-mt))))
            nmc_ref[s, h] = wg * cmat + (wi * v_col) * k_r
            nmn_ref[s, h:h + 1, :] = wg * nvec + wi * k_r
            nmm_ref[s:s + 1, h:h + 1] = mt
        ym_rows.append(jnp.concatenate(parts, axis=1))
        parts = []
        for h in range(HG_H):
            lo = h * HD
            st = hg_ref[s, h]
            q_col, fg_col = colf(4 + h, s), colf(8 + h, s)
            q_r, fg_r, v_r = row(hq, lo, HD), row(hfg, lo, HD), row(hv, lo, HD)
            parts.append(_sub_sum(st * (q_col * fg_col)) + _lane_sum(q_r * (1.0 - fg_r)) * v_r)
            nhg_ref[s, h] = fg_col * st + (1.0 - fg_col) * v_r
        yh_rows.append(jnp.concatenate(parts, axis=1))
        parts = []
        for h in range(GDN_H):
            lo = h * HD
            st = gd_ref[s, h]
            q_col, k_col = colf(12 + h, s), colf(16 + h, s)
            beta = g_beta[s:s + 1, COL_GDN_B + h:COL_GDN_B + h + 1]
            dec = g_dec[s:s + 1, COL_GDN_A + h:COL_GDN_A + h + 1]
            u2 = beta * row(gv, lo, HD) - _sub_sum(st * (k_col * (beta * dec)))
            qk = _lane_sum(row(gq, lo, HD) * row(gk, lo, HD))
            parts.append(dec * _sub_sum(st * q_col) + qk * u2)
            ngd_ref[s, h] = dec * st + k_col * u2
        yg_rows.append(jnp.concatenate(parts, axis=1))

    yr = jnp.concatenate(yr_rows, axis=0)
    mean = _seg_sum(yr, RWKV_HD) * (1.0 / RWKV_HD)
    dev = yr - mean
    var = _seg_sum(dev * dev, RWKV_HD) * (1.0 / RWKV_HD)
    yr = dev * lax.rsqrt(var + RWKV_LN_EPS) * ln_w + ln_b
    bonus = _seg_sum(r * k2 * r_k, RWKV_HD) * v
    yr_ref[...] = ((yr + bonus) * g_r).astype(yr_ref.dtype)
    ym_ref[...] = _head_norm(jnp.concatenate(ym_rows, axis=0), mlw_ref[...], _sigmoid(mo)).astype(ym_ref.dtype)
    yh_ref[...] = _head_norm(jnp.concatenate(yh_rows, axis=0), hgw_ref[...], _silu(hgate)).astype(yh_ref.dtype)
    yg_ref[...] = _head_norm(jnp.concatenate(yg_rows, axis=0), gdw_ref[...], _silu(ggate)).astype(yg_ref.dtype)


def _sample_mixers(p, states, layer, lp):
    n = p.shape[0]
    bs = DEC_BS

    def st_spec(arr):
        tail = arr.shape[2:]
        return pl.BlockSpec((None, bs) + tail, lambda i: (layer, i) + (0,) * len(tail))

    const2 = lambda shape: pl.BlockSpec(shape, lambda i: (0, 0))
    st_specs = [st_spec(a) for a in states]
    y_spec = pl.BlockSpec((bs, GW), lambda i: (i, 0))
    y_shape = jax.ShapeDtypeStruct((n, GW), BF16)
    res = pl.pallas_call(
        _decode_kernel,
        grid=(n // bs,),
        in_specs=[pl.BlockSpec((bs, PROJ_PAD), lambda i: (i, 0))] + st_specs
        + [const2((1, RWKV_PROJ)), const2((8, GW)), const2((64, GW)), const2((64, GW)), const2((128, GW)),
           const2((8, 128)), const2((1, GW)), const2((1, GW)), const2((1, GW)),
           const2((CONV_W, 3 * GW)), const2((1, GW))],
        out_specs=[y_spec] * 4 + st_specs,
        out_shape=[y_shape] * 4 + [jax.ShapeDtypeStruct(a.shape, F32) for a in states],
        input_output_aliases={1 + k: 4 + k for k in range(len(states))},
        compiler_params=_cparams(("arbitrary",)),
        name="decode_mixers",
    )(p, *states, lp["rwkv_mu"], lp["rwkv_vec"], lp["rwkv_w_up"], lp["rwkv_a_up"], lp["rwkv_g_up"],
      lp["gate_par"], lp["ml_norm"], lp["hg_lb"], lp["hg_norm"], lp["gdn_conv_w"], lp["gdn_norm"])
    return list(res[:4]), tuple(res[4:])


def _layer_params(l, a):
    gate_par = jnp.zeros((8, 128), F32)
    gate_par = gate_par.at[0, COL_ML_I:COL_ML_I + 4].set(a["ml_i_bias"][l])
    gate_par = gate_par.at[0, COL_ML_F:COL_ML_F + 4].set(a["ml_f_bias"][l])
    gate_par = gate_par.at[0, COL_GDN_A:COL_GDN_A + 4].set(a["gdn_dt_bias"][l])
    gate_par = gate_par.at[1, COL_GDN_A:COL_GDN_A + 4].set(a["gdn_a_log"][l])
    vec = jnp.stack([a[k][l] for k in ("rwkv_w0", "rwkv_a0", "rwkv_k_k", "rwkv_k_a", "rwkv_r_k",
                                        "rwkv_ln_w", "rwkv_ln_b")] + [jnp.zeros((GW,), F32)])
    return {
        "rwkv_mu": a["rwkv_mu"][l].reshape(1, RWKV_PROJ), "rwkv_vec": vec,
        "rwkv_w_up": a["rwkv_w_up"][l], "rwkv_a_up": a["rwkv_a_up"][l], "rwkv_g_up": a["rwkv_g_up"][l],
        "gate_par": gate_par, "ml_norm": a["ml_norm"][l].reshape(1, GW),
        "hg_lb": a["hg_lbs"][l].reshape(1, GW), "hg_norm": a["hg_norm"][l].reshape(1, GW),
        "gdn_conv_w": a["gdn_conv_w"][l], "gdn_norm": a["gdn_norm"][l].reshape(1, GW),
    }


def _pad_w_in(w_in):
    z = jnp.zeros(w_in.shape[:2] + (GATES_W - 16,), w_in.dtype)
    parts = [w_in[..., 0:1792], w_in[..., 3840:3848], w_in[..., 7944:7952], z,
             w_in[..., 1792:3840], w_in[..., 3848:5896], w_in[..., 5896:7944]]
    return jnp.concatenate(parts, axis=-1).astype(BF16)


def kernel(x_prompt, x_sample, state_rwkv_shift, state_rwkv_wkv, state_mlstm_c, state_mlstm_n,
           state_mlstm_m, state_hgrn, state_gdn_conv, state_gdn, c_prompt, c_sample,
           ada_w, ada_b, norm1, norm2, norm_f, w_in, w_out,
           rwkv_mu, rwkv_w0, rwkv_w_up, rwkv_a0, rwkv_a_up, rwkv_g_up, rwkv_k_k, rwkv_k_a, rwkv_r_k,
           rwkv_ln_w, rwkv_ln_b, ml_i_bias, ml_f_bias, ml_norm, hg_lb, hg_norm,
           gdn_conv_w, gdn_a_log, gdn_dt_bias, gdn_norm,
           moe_w_group, moe_b_group, moe_w_router, moe_b_router, moe_w_gate, moe_w_up, moe_w_down):
    bp, t, _ = x_prompt.shape
    ns = x_sample.shape[0]
    lbs = jax.nn.softmax(hg_lb.astype(F32), axis=0)
    hg_lbs = jnp.cumsum(lbs, axis=0) - lbs[0]
    a = dict(rwkv_mu=rwkv_mu, rwkv_w0=rwkv_w0, rwkv_w_up=rwkv_w_up, rwkv_a0=rwkv_a0, rwkv_a_up=rwkv_a_up,
             rwkv_g_up=rwkv_g_up, rwkv_k_k=rwkv_k_k, rwkv_k_a=rwkv_k_a, rwkv_r_k=rwkv_r_k,
             rwkv_ln_w=rwkv_ln_w, rwkv_ln_b=rwkv_ln_b, ml_i_bias=ml_i_bias, ml_f_bias=ml_f_bias,
             ml_norm=ml_norm, hg_lbs=hg_lbs, hg_norm=hg_norm, gdn_conv_w=gdn_conv_w, gdn_a_log=gdn_a_log,
             gdn_dt_bias=gdn_dt_bias, gdn_norm=gdn_norm)
    w_in_p = _pad_w_in(w_in)
    w_out_b = w_out.astype(BF16)
    w_route = jnp.concatenate([moe_w_group, moe_w_router,
                               jnp.zeros((DEPTH, D_MODEL, 128 - N_GROUPS - N_EXPERTS), F32)], axis=-1)
    w_route_hi = w_route.astype(BF16)
    w_route_lo = (w_route - w_route_hi.astype(F32)).astype(BF16)
    w_route = jnp.concatenate([w_route_hi, w_route_lo], axis=-1)
    b_route = jnp.concatenate([moe_b_group, moe_b_router,
                               jnp.zeros((DEPTH, 128 - N_GROUPS - N_EXPERTS), F32)], axis=-1)

    mod = _ada(jnp.concatenate([c_prompt, c_sample], axis=0), ada_w, ada_b)
    sample_states = (state_rwkv_shift.reshape(DEPTH, ns, RWKV_PROJ), state_rwkv_wkv, state_mlstm_c,
                     state_mlstm_n, state_mlstm_m, state_hgrn,
                     state_gdn_conv.reshape(DEPTH, ns, (CONV_W - 1) * 3 * GW), state_gdn)

    xp = x_prompt.reshape(bp * t, D_MODEL)
    xs = x_sample.reshape(ns, D_MODEL)
    n_p = bp * t
    ff = None
    modp = mods = None
    new_p = []
    for l in range(DEPTH):
        lp = _layer_params(l, a)
        prev_modp, prev_mods = modp, mods
        modp = mod[l, :bp].reshape(bp, 1, 6 * D_MODEL)
        mods = mod[l, bp:].reshape(1, ns, 6 * D_MODEL)

        xp, pp = _in_proj(xp, modp, t, norm1[l], w_in_p[l], ff, 0, prev_modp)
        xs, ps = _in_proj(xs, mods, ns, norm1[l], w_in_p[l], ff, n_p, prev_mods)
        ysp, stp = _prompt_mixers(pp.reshape(bp, t, PROJ_PAD), lp)
        yss, sample_states = _sample_mixers(ps, sample_states, l, lp)
        new_p.append(stp)
        xp, xs, h2, rt = _out_proj(ysp, xp, modp, t, yss, xs, mods, norm2[l], w_out_b[l],
                                   w_route[l], b_route[l].reshape(1, 128))
        ff = _moe(h2, rt, n_p + ns, moe_w_gate, moe_w_up, moe_w_down, l)
    yp = _final(xp, ff, 0, modp, t, norm_f).reshape(bp, t, D_MODEL)
    ys = _final(xs, ff, n_p, mods, ns, norm_f).reshape(ns, 1, D_MODEL)
    prompt_states = tuple(jnp.stack([st[i] for st in new_p]) for i in range(8))
    s_sh, s_wkv, s_mc, s_mn, s_mm, s_hg, s_cv, s_gd = sample_states
    return (yp, ys) + prompt_states + (s_sh.reshape(DEPTH, ns, 1, RWKV_PROJ), s_wkv, s_mc, s_mn, s_mm, s_hg,
                                       s_cv.reshape(DEPTH, ns, CONV_W - 1, 3 * GW), s_gd)
```

```python
import functools
import math

import jax
import jax.numpy as jnp
from jax import lax
from jax.experimental import pallas as pl
from jax.experimental.pallas import tpu as pltpu

F32 = jnp.float32
BF16 = jnp.bfloat16
HI = lax.Precision.HIGHEST

D_MODEL = 2048
DEPTH = 4
GW = 512
RWKV_HD = 64
RWKV_H = 8
RWKV_PROJ = 1792
RWKV_LN_EPS = 64e-5
ML_H = 4
HG_H = 4
GDN_H = 4
HD = 128
CONV_W = 4
N_GROUPS = 4
EPG = 8
N_EXPERTS = 32
D_FF = 256
NORM_EPS = 1e-6

PROJ_PAD = 8192
GATES_OFF = 1792
GATES_W = 256
COL_ML_I, COL_ML_F, COL_GDN_B, COL_GDN_A = 0, 4, 8, 12

CHUNK = 64
MIX_NSEQ = 2
SUB = 16
VMEM_LIMIT = 56 * 1024 * 1024


def _cparams(sem):
    return pltpu.CompilerParams(dimension_semantics=sem, vmem_limit_bytes=VMEM_LIMIT)


def _mm(a, b):
    return jnp.dot(a.astype(BF16), b.astype(BF16), preferred_element_type=F32)


def _mm_nt(a, b):
    return lax.dot_general(a.astype(BF16), b.astype(BF16), (((1,), (1,)), ((), ())),
                           preferred_element_type=F32)


def _mm_tn(a, b):
    return lax.dot_general(a.astype(BF16), b.astype(BF16), (((0,), (0,)), ((), ())),
                           preferred_element_type=F32)


def _mm_hi(a, b):
    return jnp.dot(a, b, precision=HI, preferred_element_type=F32)


def _mm_nt_hi(a, b):
    return lax.dot_general(a, b, (((1,), (1,)), ((), ())), precision=HI, preferred_element_type=F32)


def _sigmoid(x):
    return 1.0 / (1.0 + jnp.exp(-x))


def _silu(x):
    return x * _sigmoid(x)


def _softplus(x):
    return jnp.maximum(x, 0.0) + jnp.log(1.0 + jnp.exp(-jnp.abs(x)))


def _iota(shape, dim):
    return lax.broadcasted_iota(jnp.int32, shape, dim)


def _eye(n, m):
    return (_iota((n, m), 0) == _iota((n, m), 1)).astype(F32)


def _lower(n, inclusive):
    r, c = _iota((n, n), 0), _iota((n, n), 1)
    return (c <= r) if inclusive else (c < r)


def _rms_rows(x):
    return x * lax.rsqrt(jnp.mean(x * x, axis=-1, keepdims=True) + NORM_EPS)


def _tri_inv_minus_eye(n_mat, size):
    m = -n_mat
    steps = int(math.log2(size))
    pw = _mm(n_mat, n_mat)
    yield
    for i in range(1, steps):
        t = _mm(m, pw)
        nxt = _mm(pw, pw) if i < steps - 1 else None
        yield
        m = m + pw + t
        pw = nxt
    return m


def _round_robin(gens):
    gens = list(gens)
    results = [None] * len(gens)
    active = list(range(len(gens)))
    while active:
        for i in list(active):
            try:
                next(gens[i])
            except StopIteration as stop:
                results[i] = stop.value
                active.remove(i)
    return results


def _ada_kernel(c_ref, w_ref, b_ref, o_ref):
    c = c_ref[...]
    o_ref[...] = _mm(_silu(c), w_ref[...]) + b_ref[...]


def _ada(c_all, ada_w, ada_b):
    depth, d, n6 = ada_w.shape
    rows = c_all.shape[0]
    tn = 1024
    return pl.pallas_call(
        _ada_kernel,
        grid=(depth, n6 // tn),
        in_specs=[pl.BlockSpec((rows, d), lambda l, j: (0, 0)),
                  pl.BlockSpec((None, d, tn), lambda l, j: (l, 0, j)),
                  pl.BlockSpec((None, 1, tn), lambda l, j: (l, 0, j))],
        out_specs=pl.BlockSpec((None, rows, tn), lambda l, j: (l, 0, j)),
        out_shape=jax.ShapeDtypeStruct((depth, rows, n6), F32),
        compiler_params=_cparams(("parallel", "parallel")),
        name="ada",
    )(c_all, ada_w, ada_b.reshape(depth, 1, n6))


def _modulated_norm(x, nw, sc, sh):
    return _rms_rows(x) * nw * (1.0 + sc) + sh


def _inproj_kernel(*refs, combine):
    if combine:
        x_ref, ff_ref, gt_ref, sh_ref, sc_ref, nw_ref, w_ref, xo_ref, p_ref, h_ref = refs
    else:
        x_ref, sh_ref, sc_ref, nw_ref, w_ref, p_ref, h_ref = refs

    @pl.when(pl.program_id(1) == 0)
    def _():
        x = x_ref[...]
        if combine:
            rows = x.shape[0]
            x = x + gt_ref[...] * (_load_slabs(ff_ref.at[0], rows) + _load_slabs(ff_ref.at[1], rows))
            xo_ref[...] = x
        h_ref[...] = _modulated_norm(x, nw_ref[...], sc_ref[...], sh_ref[...]).astype(BF16)

    p_ref[...] = jnp.dot(h_ref[...], w_ref[...], preferred_element_type=F32)


def _mod_spec(mod3, which, tm, rows_per_group):
    r = mod3.shape[1]
    return pl.BlockSpec((None, r, D_MODEL),
                        lambda i, *_: ((i * tm) // rows_per_group, 0, which))


def _in_proj(x, mod3, rows_per_group, nw, w, ff=None, ff_row0=0, gate_mod3=None):
    n = x.shape[0]
    tm = min(512, rows_per_group)
    tn = 1024
    combine = ff is not None
    row = lambda i, j: (i, 0)
    in_specs = [pl.BlockSpec((tm, D_MODEL), row)]
    args = [x]
    if combine:
        blk0 = ff_row0 // tm
        in_specs += [pl.BlockSpec((2, tm * SLAB, 128), lambda i, j: (0, blk0 + i, 0)),
                     _mod_spec(gate_mod3, 5, tm, rows_per_group)]
        args += [ff, gate_mod3]
    in_specs += [_mod_spec(mod3, 0, tm, rows_per_group), _mod_spec(mod3, 1, tm, rows_per_group),
                 pl.BlockSpec((1, D_MODEL), lambda i, j: (0, 0)),
                 pl.BlockSpec((D_MODEL, tn), lambda i, j: (0, j))]
    args += [mod3, mod3, nw.reshape(1, D_MODEL), w]
    out_specs = [pl.BlockSpec((tm, tn), lambda i, j: (i, j))]
    out_shape = [jax.ShapeDtypeStruct((n, PROJ_PAD), F32)]
    if combine:
        out_specs = [pl.BlockSpec((tm, D_MODEL), row)] + out_specs
        out_shape = [jax.ShapeDtypeStruct((n, D_MODEL), F32)] + out_shape
    res = pl.pallas_call(
        functools.partial(_inproj_kernel, combine=combine),
        grid=(n // tm, PROJ_PAD // tn),
        in_specs=in_specs, out_specs=out_specs, out_shape=out_shape,
        scratch_shapes=[pltpu.VMEM((tm, D_MODEL), BF16)],
        compiler_params=_cparams(("parallel", "arbitrary")),
        name="in_proj",
    )(*args)
    return (res[0], res[1]) if combine else (x, res[0])


def _lane_max(x):
    return jnp.max(x, axis=-1, keepdims=True)


def _lane_sum(x):
    return jnp.sum(x, axis=-1, keepdims=True)


def _first_lane_of(mask, lane):
    return jnp.min(jnp.where(mask, lane, 4096), axis=-1, keepdims=True)


def _route(logits, biased):
    lane = _iota(logits.shape, 1)
    neg = jnp.float32(-jnp.inf)
    is_g = lane < N_GROUPS
    gmax = _lane_max(jnp.where(is_g, logits, neg))
    gexp = jnp.where(is_g, jnp.exp(logits - gmax), 0.0)
    gb = jnp.where(is_g, biased, neg)
    gsel = _first_lane_of(gb == _lane_max(gb), lane)
    pg = _lane_sum(jnp.where(lane == gsel, gexp, 0.0)) / _lane_sum(gexp)
    lo = N_GROUPS + gsel * EPG
    eb = jnp.where((lane >= lo) & (lane < lo + EPG), biased, neg)
    i1 = _first_lane_of(eb == _lane_max(eb), lane)
    eb2 = jnp.where(lane == i1, neg, eb)
    i2 = _first_lane_of(eb2 == _lane_max(eb2), lane)
    l1 = _lane_sum(jnp.where(lane == i1, logits, 0.0))
    l2 = _lane_sum(jnp.where(lane == i2, logits, 0.0))
    mx = jnp.maximum(l1, l2)
    e1, e2 = jnp.exp(l1 - mx), jnp.exp(l2 - mx)
    cw1, cw2 = pg * e1 / (e1 + e2), pg * e2 / (e1 + e2)
    id1, id2 = (i1 - N_GROUPS).astype(F32), (i2 - N_GROUPS).astype(F32)
    return jnp.where(lane == 0, id1, jnp.where(lane == 1, id2,
                     jnp.where(lane == 2, cw1, jnp.where(lane == 3, cw2, 0.0))))


SLAB = D_MODEL // 128


def _load_slabs(ref, rows):
    return jnp.concatenate([ref[pl.ds(s, rows, stride=SLAB), :] for s in range(SLAB)], axis=1)


def _store_slabs(ref, x, rows):
    for s in range(SLAB):
        ref[pl.ds(s, rows, stride=SLAB), :] = x[:, s * 128:(s + 1) * 128]


def _outproj_kernel(*refs, n_prompt_tiles, tm, rows_s):
    group_p, group_s = refs[0:8], refs[8:16]
    nw_ref, wo_ref, wr_ref, br_ref, xop_ref, xos_ref, h2_ref, rt_ref = refs[16:]
    i = pl.program_id(0)

    def body(group, xo_ref, rows):
        yr_ref, ym_ref, yh_ref, yg_ref, x_ref, gt_ref, sh_ref, sc_ref = group
        mix = jnp.dot(yr_ref[...], wo_ref[0:GW, :], preferred_element_type=F32)
        mix += jnp.dot(ym_ref[...], wo_ref[GW:2 * GW, :], preferred_element_type=F32)
        mix += jnp.dot(yh_ref[...], wo_ref[2 * GW:3 * GW, :], preferred_element_type=F32)
        mix += jnp.dot(yg_ref[...], wo_ref[3 * GW:4 * GW, :], preferred_element_type=F32)
        x = x_ref[...] + gt_ref[...] * mix
        xo_ref[...] = x
        h2 = _modulated_norm(x, nw_ref[...], sc_ref[...], sh_ref[...])
        h_hi = h2.astype(BF16)
        h_lo = (h2 - h_hi.astype(F32)).astype(BF16)
        hw = jnp.dot(h_hi, wr_ref[...], preferred_element_type=F32)
        logits = hw[:, 0:128] + hw[:, 128:256] + jnp.dot(h_lo, wr_ref[:, 0:128], preferred_element_type=F32)
        route = _route(logits, logits + br_ref[...])
        _store_slabs(h2_ref, h2, rows)
        rt_ref[0:rows, :] = route
        if rows < tm:
            h2_ref[rows * SLAB:tm * SLAB, :] = jnp.zeros(((tm - rows) * SLAB, 128), F32)
            rt_ref[rows:tm, :] = jnp.zeros((tm - rows, 128), F32)

    @pl.when(i < n_prompt_tiles)
    def _():
        body(group_p, xop_ref, tm)

    @pl.when(i == n_prompt_tiles)
    def _():
        body(group_s, xos_ref, rows_s)


def _out_proj(ys_p, x_p, mod_p, t, ys_s, x_s, mod_s, nw, wo, w_route, b_route):
    n_p, n_s = x_p.shape[0], x_s.shape[0]
    tm = min(256, t)
    assert n_s <= tm and n_p % tm == 0
    n_pt = n_p // tm
    prow = lambda i: (jnp.minimum(i, n_pt - 1), 0)
    const = lambda i: (0, 0)
    pmod = lambda which: pl.BlockSpec((None, 1, D_MODEL),
                                      lambda i: ((jnp.minimum(i, n_pt - 1) * tm) // t, 0, which))
    smod = lambda which: pl.BlockSpec((None, n_s, D_MODEL), lambda i: (0, 0, which))
    in_specs = ([pl.BlockSpec((tm, GW), prow)] * 4 + [pl.BlockSpec((tm, D_MODEL), prow), pmod(2), pmod(3), pmod(4)]
                + [pl.BlockSpec((n_s, GW), const)] * 4 + [pl.BlockSpec((n_s, D_MODEL), const),
                                                          smod(2), smod(3), smod(4)]
                + [pl.BlockSpec((1, D_MODEL), const), pl.BlockSpec((D_MODEL, D_MODEL), const),
                   pl.BlockSpec((D_MODEL, 256), const), pl.BlockSpec((1, 128), const)])
    n_all = n_p + tm
    return pl.pallas_call(
        functools.partial(_outproj_kernel, n_prompt_tiles=n_pt, tm=tm, rows_s=n_s),
        grid=(n_pt + 1,),
        in_specs=in_specs,
        out_specs=[pl.BlockSpec((tm, D_MODEL), prow), pl.BlockSpec((n_s, D_MODEL), const),
                   pl.BlockSpec((tm * SLAB, 128), lambda i: (i, 0)), pl.BlockSpec((tm, 128), lambda i: (i, 0))],
        out_shape=[jax.ShapeDtypeStruct((n_p, D_MODEL), F32), jax.ShapeDtypeStruct((n_s, D_MODEL), F32),
                   jax.ShapeDtypeStruct((n_all * SLAB, 128), F32), jax.ShapeDtypeStruct((n_all, 128), F32)],
        compiler_params=_cparams(("arbitrary",)),
        name="out_proj",
    )(*ys_p, x_p, mod_p, mod_p, mod_p, *ys_s, x_s, mod_s, mod_s, mod_s,
      nw.reshape(1, D_MODEL), wo, w_route, b_route)


def _moe_kernel(te_ref, nv_ref, tok_ref, dst_ref, h2_hbm, cw_ref, wg_ref, wu_ref, wd_ref, out_hbm,
                hbuf0, hbuf1, obuf0, obuf1, wgb, wub, wdb, gsem, ssem, *, tm, n_tiles):
    t = pl.program_id(0)
    hbufs, obufs = (hbuf0, hbuf1), (obuf0, obuf1)

    def row_copy(src_ref, src, dst_ref_, dst, sem):
        return pltpu.make_async_copy(src_ref.at[pl.ds(pl.multiple_of(src, SLAB), SLAB)],
                                     dst_ref_.at[pl.ds(pl.multiple_of(dst, SLAB), SLAB)], sem)

    def start_gather(tile, slot):
        def body(r, carry):
            row_copy(h2_hbm, tok_ref[tile * tm + r], hbufs[slot], r * SLAB, gsem.at[slot]).start()
            return carry
        lax.fori_loop(0, nv_ref[tile], body, 0)

    def start_scatter(tile, slot):
        def body(r, carry):
            row_copy(obufs[slot], r * SLAB, out_hbm, dst_ref[tile * tm + r], ssem.at[slot]).start()
            return carry
        lax.fori_loop(0, nv_ref[tile], body, 0)

    def wait_rows(src_ref, dst_ref_, sem, count):
        def body(r, carry):
            row_copy(src_ref, 0, dst_ref_, 0, sem).wait()
            return carry
        lax.fori_loop(0, count, body, 0)

    def wait_gather(tile, slot):
        wait_rows(h2_hbm, hbufs[slot], gsem.at[slot], nv_ref[tile])

    def wait_scatter(tile, slot):
        wait_rows(obufs[slot], out_hbm, ssem.at[slot], nv_ref[tile])

    @pl.when(t == 0)
    def _():
        hbuf0[...] = jnp.zeros_like(hbuf0)
        hbuf1[...] = jnp.zeros_like(hbuf1)
        start_gather(0, 0)

    @pl.when((t == 0) | (te_ref[t] != te_ref[jnp.maximum(t - 1, 0)]))
    def _():
        wgb[...] = wg_ref[...].astype(BF16)
        wub[...] = wu_ref[...].astype(BF16)
        wdb[...] = wd_ref[...].astype(BF16)

    def step(slot):
        wait_gather(t, slot)
        start_gather(t + 1, 1 - slot)
        h = _load_slabs(hbufs[slot], tm).astype(BF16)
        g = jnp.dot(h, wgb[...], preferred_element_type=F32)
        u = jnp.dot(h, wub[...], preferred_element_type=F32)
        hid = (_silu(g) * u * cw_ref[...]).astype(BF16)
        _store_slabs(obufs[slot], jnp.dot(hid, wdb[...], preferred_element_type=F32), tm)
        start_scatter(t, slot)

    for slot in (0, 1):
        @pl.when(t % 2 == slot)
        def _(slot=slot):
            step(slot)

        @pl.when((t >= 1) & (t % 2 == slot))
        def _(slot=slot):
            wait_scatter(t - 1, 1 - slot)

    @pl.when(t == n_tiles - 1)
    def _():
        wait_scatter(n_tiles - 1, (n_tiles - 1) % 2)


MOE_TM = 128


def _moe(h2, route, n_tok, w_gate, w_up, w_down, layer):
    tm = MOE_TM
    slots = 2 * n_tok
    n_tiles = -(-(slots + N_EXPERTS * (tm - 1)) // tm)
    n_pad = n_tok
    n_rows = n_tiles * tm
    eid = jnp.concatenate([route[:n_tok, 0], route[:n_tok, 1]]).astype(jnp.int32)
    cw = jnp.concatenate([route[:n_tok, 2], route[:n_tok, 3]])
    onehot = (eid[:, None] == jnp.arange(N_EXPERTS, dtype=jnp.int32)[None, :]).astype(jnp.int32)
    n_blk = -(-slots // tm)
    oh3 = jnp.pad(onehot, ((0, n_blk * tm - slots), (0, 0))).reshape(n_blk, tm, N_EXPERTS).astype(BF16)
    tri = (jnp.arange(tm)[:, None] >= jnp.arange(tm)[None, :]).astype(BF16)
    within = jnp.einsum("ij,bjk->bik", tri, oh3, preferred_element_type=F32).astype(jnp.int32)
    blk_tot = within[:, -1, :]
    blk_off = jnp.cumsum(blk_tot, axis=0) - blk_tot
    running = (within + blk_off[:, None, :]).reshape(n_blk * tm, N_EXPERTS)[:slots]
    counts = jnp.sum(blk_tot, axis=0)
    tiles_per = (counts + tm - 1) // tm
    tile_end = jnp.cumsum(tiles_per)
    tile_start = tile_end - tiles_per
    rank = jnp.sum(running * onehot, axis=1) - 1
    pos = jnp.sum(onehot * tile_start[None, :], axis=1) * tm + rank
    s_idx = jnp.arange(slots, dtype=jnp.int32)
    tok = s_idx % n_tok
    dst = (s_idx // n_tok) * n_pad + tok
    packed = jnp.stack([tok.astype(F32), dst.astype(F32), cw], axis=1)
    rows = jnp.zeros((n_rows, 3), F32).at[pos].set(packed)
    row_tok = rows[:, 0].astype(jnp.int32) * SLAB
    row_dst = rows[:, 1].astype(jnp.int32) * SLAB
    row_cw = rows[:, 2].reshape(n_rows, 1)
    tile_ids = jnp.arange(n_tiles, dtype=jnp.int32)
    tile_e = jnp.sum((tile_end[None, :] <= tile_ids[:, None]).astype(jnp.int32), axis=1)
    tile_e = jnp.minimum(tile_e, N_EXPERTS - 1)
    te_oh = (tile_e[:, None] == jnp.arange(N_EXPERTS, dtype=jnp.int32)[None, :]).astype(jnp.int32)
    tile_cnt = jnp.sum(te_oh * counts[None, :], axis=1)
    tile_first = jnp.sum(te_oh * tile_start[None, :], axis=1)
    tile_nv = jnp.clip(tile_cnt - (tile_ids - tile_first) * tm, 0, tm)
    tile_nv = jnp.where(tile_ids < tile_end[-1], tile_nv, 0)
    tile_nv = jnp.concatenate([tile_nv, jnp.zeros((1,), jnp.int32)]).astype(jnp.int32)

    wspec = lambda shape: pl.BlockSpec((None, None) + shape, lambda t, te, *_: (layer, te[t], 0, 0))
    grid_spec = pltpu.PrefetchScalarGridSpec(
        num_scalar_prefetch=4,
        grid=(n_tiles,),
        in_specs=[pl.BlockSpec(memory_space=pl.ANY),
                  pl.BlockSpec((tm, 1), lambda t, *_: (t, 0)),
                  wspec((D_MODEL, D_FF)), wspec((D_MODEL, D_FF)), wspec((D_FF, D_MODEL))],
        out_specs=pl.BlockSpec(memory_space=pl.ANY),
        scratch_shapes=[pltpu.VMEM((tm * SLAB, 128), F32)] * 4
        + [pltpu.VMEM((D_MODEL, D_FF), BF16), pltpu.VMEM((D_MODEL, D_FF), BF16),
           pltpu.VMEM((D_FF, D_MODEL), BF16),
           pltpu.SemaphoreType.DMA((2,)), pltpu.SemaphoreType.DMA((2,))])
    out = pl.pallas_call(
        functools.partial(_moe_kernel, tm=tm, n_tiles=n_tiles),
        grid_spec=grid_spec,
        out_shape=jax.ShapeDtypeStruct((2 * n_pad * SLAB, 128), F32),
        compiler_params=_cparams(("arbitrary",)),
        name="moe",
    )(tile_e, tile_nv, row_tok, row_dst, h2, row_cw, w_gate, w_up, w_down)
    return out.reshape(2, n_pad * SLAB, 128)


def _final_kernel(x_ref, ff_ref, gt_ref, nw_ref, o_ref):
    rows = x_ref.shape[0]
    x = x_ref[...] + gt_ref[...] * (_load_slabs(ff_ref.at[0], rows) + _load_slabs(ff_ref.at[1], rows))
    o_ref[...] = _rms_rows(x) * nw_ref[...]


def _final(x, ff, ff_row0, mod3, rows_per_group, nw):
    n = x.shape[0]
    tm = min(512, rows_per_group)
    blk0 = ff_row0 // tm
    return pl.pallas_call(
        _final_kernel,
        grid=(n // tm,),
        in_specs=[pl.BlockSpec((tm, D_MODEL), lambda i: (i, 0)),
                  pl.BlockSpec((2, tm * SLAB, 128), lambda i: (0, blk0 + i, 0)),
                  _mod_spec(mod3, 5, tm, rows_per_group),
                  pl.BlockSpec((1, D_MODEL), lambda i: (0, 0))],
        out_specs=pl.BlockSpec((tm, D_MODEL), lambda i: (i, 0)),
        out_shape=jax.ShapeDtypeStruct((n, D_MODEL), F32),
        compiler_params=_cparams(("parallel",)),
        name="final_norm",
    )(x, ff, mod3, nw.reshape(1, D_MODEL))


def _seg_sum(x, width):
    seg = (_iota((128, 128), 0) // width == _iota((128, 128), 1) // width).astype(F32)
    parts = [_mm_hi(x[:, j:j + 128], seg) for j in range(0, x.shape[1], 128)]
    return parts[0] if len(parts) == 1 else jnp.concatenate(parts, axis=1)


def _head_norm(o, nw, gate):
    return o * lax.rsqrt(_seg_sum(o * o, HD) * (1.0 / HD) + NORM_EPS) * nw * gate


def _carry_rows(ext_ref, cur, first, L):
    @pl.when(first)
    def _():
        ext_ref[0:8, :] = jnp.zeros((8, ext_ref.shape[1]), F32)

    ext_ref[8:8 + L, :] = cur


def _rwkv_parts(p_ref, mu_ref, vec_ref, wup_ref, aup_ref, gup_ref, y_ref, sh_ref, s_ref, ext_ref, L):
    c = pl.program_id(1)
    p = p_ref[...]
    _carry_rows(ext_ref, p, c == 0, L)
    prev = ext_ref[7:7 + L, :]
    xs = p + (prev - p) * mu_ref[...]
    ext_ref[0:8, :] = ext_ref[L:L + 8, :]
    sh_ref[...] = p[L - 1:L, :]

    @pl.when(c == 0)
    def _():
        s_ref[...] = jnp.zeros_like(s_ref)

    w0, a0, k_k, k_a, r_k, ln_w, ln_b = [vec_ref[i:i + 1, :] for i in range(7)]
    r, k, v = xs[:, 0:GW], xs[:, GW:2 * GW], xs[:, 2 * GW:3 * GW]
    dw, da, dg = xs[:, 1536:1600], xs[:, 1600:1664], xs[:, 1664:1792]
    logw = -_softplus(-(w0 + _mm(jnp.tanh(dw), wup_ref[...]))) - 0.5
    lw = -jnp.exp(logw)
    a = _sigmoid(a0 + _mm(da, aup_ref[...]))
    g = _mm(_sigmoid(dg), gup_ref[...])
    kk = k * k_k
    kk = kk * lax.rsqrt(jnp.maximum(_seg_sum(kk * kk, RWKV_HD), 1e-12))
    k2 = k * (1.0 + (a - 1.0) * k_a)
    cl = _mm_hi(_lower(L, True).astype(F32), lw)
    cl_last = cl[L - 1:L, :]
    at = -kk * jnp.exp(cl - lw)
    bt = kk * a * jnp.exp(-cl)
    kt = k2 * jnp.exp(-cl)
    rt = r * jnp.exp(cl)
    b_end = kk * a * jnp.exp(cl_last - cl)
    k_end = k2 * jnp.exp(cl_last - cl)
    strict, incl = _lower(L, False), _lower(L, True)
    def head(h):
        sl = slice(h * RWKV_HD, (h + 1) * RWKV_HD)
        s0 = s_ref[h]
        ath, bth, kth, rth, vh = at[:, sl], bt[:, sl], kt[:, sl], rt[:, sl], v[:, sl]
        g_ab, g_ak, g_rb, g_rk = _mm_nt(ath, bth), _mm_nt(ath, kth), _mm_nt(rth, bth), _mm_nt(rth, kth)
        a_s0, r_s0 = _mm_nt(ath, s0), _mm_nt(rth, s0)
        s_v = _mm_tn(vh, k_end[:, sl])
        yield
        n_ab = jnp.where(strict, g_ab, 0.0)
        ak_v = _mm(jnp.where(strict, g_ak, 0.0), vh)
        rk_v = _mm(jnp.where(incl, g_rk, 0.0), vh)
        m = yield from _tri_inv_minus_eye(-n_ab, L)
        rhs = a_s0 + ak_v
        m_rhs = _mm(m, rhs)
        yield
        u = rhs + m_rhs
        rb_u = _mm(jnp.where(incl, g_rb, 0.0), u)
        s_u = _mm_tn(u, b_end[:, sl])
        yield
        s_ref[h] = s0 * jnp.exp(cl_last[:, sl]) + s_u + s_v
        return r_s0 + rb_u + rk_v

    def finish(ys):
        y = jnp.concatenate(ys, axis=1)
        mean = _seg_sum(y, RWKV_HD) * (1.0 / RWKV_HD)
        dev = y - mean
        var = _seg_sum(dev * dev, RWKV_HD) * (1.0 / RWKV_HD)
        y = dev * lax.rsqrt(var + RWKV_LN_EPS) * ln_w + ln_b
        bonus = _seg_sum(r * k2 * r_k, RWKV_HD) * v
        y_ref[...] = ((y + bonus) * g).astype(y_ref.dtype)

    return [head(h) for h in range(RWKV_H)], finish


def _gate_forms(g_ref, gp_ref, L):
    raw = g_ref[:, 0:128] + gp_ref[0:1, :]
    lane = _iota(raw.shape, 1)
    lf = jnp.minimum(raw, 0.0) - jnp.log(1.0 + jnp.exp(-jnp.abs(raw)))
    beta = _sigmoid(raw)
    decay = -jnp.exp(gp_ref[1:2, :]) * _softplus(raw)
    cols = jnp.where(lane < 4, raw, jnp.where(lane < 8, lf, jnp.where(lane < 12, beta, decay)))
    rows = _mm_nt_hi(_eye(16, 128), cols)
    ccols = _mm_hi(_lower(L, True).astype(F32), cols)
    upper = (_iota((L, L), 0) <= _iota((L, L), 1)).astype(F32)
    crows = _mm_hi(rows, upper)
    return cols, rows, ccols, crows


def _mlstm_parts(p_ref, gate_forms, nw_ref, y_ref, c_ref, n_ref, m_ref, L):
    c = pl.program_id(1)

    @pl.when(c == 0)
    def _():
        c_ref[...] = jnp.zeros_like(c_ref)
        n_ref[...] = jnp.zeros_like(n_ref)
        m_ref[...] = jnp.zeros_like(m_ref)

    cols, rows, ccols, crows = gate_forms
    causal = _lower(L, True)
    neg = jnp.float32(-jnp.inf)
    def head(h):
        q = p_ref[:, h * HD:(h + 1) * HD]
        k = p_ref[:, GW + h * HD:GW + (h + 1) * HD] * (HD ** -0.5)
        v = p_ref[:, 2 * GW + h * HD:2 * GW + (h + 1) * HD]
        ig_col, ig_row = cols[:, h:h + 1], rows[h:h + 1, :]
        b_col, b_row = ccols[:, 4 + h:5 + h], crows[4 + h:5 + h, :]
        m_prev = m_ref[:, h:h + 1]
        cmat, nvec = c_ref[h], n_ref[h:h + 1, :]
        qk, qc = _mm_nt(q, k), _mm_nt(q, cmat)
        dmat = jnp.where(causal, b_col - b_row + ig_row, neg)
        gcol = b_col + m_prev
        mt = jnp.maximum(gcol, _lane_max(dmat))
        m_last = mt[L - 1:L, :]
        wk = jnp.exp(b_col[L - 1:L, :] - b_col + ig_col - m_last)
        decay = jnp.exp(gcol[L - 1:L, :] - m_last)
        c_upd = _mm_tn(v * wk, k)
        yield
        smat = qk * jnp.exp(dmat - mt)
        s_v = _mm(smat, v)
        wg = jnp.exp(gcol - mt)
        den = _lane_sum(smat) + wg * _lane_sum(q * nvec)
        c_ref[h] = decay * cmat + c_upd
        n_ref[h:h + 1, :] = decay * nvec + jnp.sum(k * wk, axis=0, keepdims=True)
        m_ref[:, h:h + 1] = m_last
        yield
        return (s_v + wg * qc) / jnp.maximum(jnp.abs(den), jnp.exp(-mt))

    def finish(ys):
        o = p_ref[:, 3 * GW:4 * GW]
        y_ref[...] = _head_norm(jnp.concatenate(ys, axis=1), nw_ref[...], _sigmoid(o)).astype(y_ref.dtype)

    return [head(h) for h in range(ML_H)], finish


def _hgrn_parts(p_ref, lb_ref, nw_ref, y_ref, s_ref, L):
    c = pl.program_id(1)

    @pl.when(c == 0)
    def _():
        s_ref[...] = jnp.zeros_like(s_ref)

    lb = lb_ref[...]
    qa = _silu(p_ref[:, 0:GW])
    fg = lb + (1.0 - lb) * _sigmoid(p_ref[:, GW:2 * GW])
    ka = 1.0 - fg
    va = p_ref[:, 2 * GW:3 * GW]
    cga = _mm_hi(_lower(L, True).astype(F32), jnp.log(fg))
    ones = jnp.ones((HD, HD), BF16)
    t3, s3 = _iota((SUB, SUB, HD), 0), _iota((SUB, SUB, HD), 1)
    neg = jnp.float32(-jnp.inf)
    def head(h):
        sl = slice(h * HD, (h + 1) * HD)
        q, k, v, cg = qa[:, sl], ka[:, sl], va[:, sl], cga[:, sl]
        st = s_ref[h]
        inter = _mm_nt(q * jnp.exp(cg), st)
        cg_last = cg[L - 1:L, :]
        s_upd = _mm_tn(v, k * jnp.exp(cg_last - cg))
        a3s, a_offs = [], []
        for i in range(L // SUB):
            lo = i * SUB
            qi, ki, cgi = q[lo:lo + SUB], k[lo:lo + SUB], cg[lo:lo + SUB]
            e3 = jnp.exp(jnp.where(s3 <= t3, cgi[:, None, :] - cgi[None, :, :], neg))
            x3 = qi[:, None, :] * ki[None, :, :] * e3
            a3s.append(jnp.dot(x3.reshape(SUB * SUB, HD).astype(BF16), ones, preferred_element_type=F32))
            if i > 0:
                ref = cg[lo - 1:lo, :]
                a_offs.append(_mm_nt(qi * jnp.exp(cgi - ref), k[0:lo] * jnp.exp(ref - cg[0:lo])))
        yield
        s_ref[h] = st * jnp.exp(cg_last) + s_upd
        offs = [_mm(a_off, v[0:(i + 1) * SUB]) for i, a_off in enumerate(a_offs)]
        yield
        rows_out = []
        for i in range(L // SUB):
            lo = i * SUB
            oi = inter[lo:lo + SUB] + jnp.sum(a3s[i].reshape(SUB, SUB, HD) * v[lo:lo + SUB][None, :, :], axis=1)
            rows_out.append(oi if i == 0 else oi + offs[i - 1])
        return jnp.concatenate(rows_out, axis=0)

    def finish(ys):
        g = p_ref[:, 3 * GW:4 * GW]
        y_ref[...] = _head_norm(jnp.concatenate(ys, axis=1), nw_ref[...], _silu(g)).astype(y_ref.dtype)

    return [head(h) for h in range(HG_H)], finish


def _gdn_parts(p_ref, gate_forms, cw_ref, nw_ref, y_ref, cv_ref, s_ref, ext_ref, L):
    c = pl.program_id(1)
    w3 = 3 * GW
    _carry_rows(ext_ref, p_ref[:, 0:w3], c == 0, L)
    conv = (cw_ref[3:4, :] * ext_ref[8:8 + L, :] + cw_ref[2:3, :] * ext_ref[7:7 + L, :]
            + cw_ref[1:2, :] * ext_ref[6:6 + L, :] + cw_ref[0:1, :] * ext_ref[5:5 + L, :])
    cv_ref[...] = ext_ref[L + 5:L + 8, :]
    ext_ref[0:8, :] = ext_ref[L:L + 8, :]

    @pl.when(c == 0)
    def _():
        s_ref[...] = jnp.zeros_like(s_ref)

    qkv = _silu(conv)
    cols, rows, ccols, crows = gate_forms
    strict, causal = _lower(L, False), _lower(L, True)
    neg = jnp.float32(-jnp.inf)
    qa = qkv[:, 0:GW]
    ka = qkv[:, GW:2 * GW]
    qa = qa * lax.rsqrt(jnp.maximum(_seg_sum(qa * qa, HD), 1e-12)) * (HD ** -0.5)
    ka = ka * lax.rsqrt(jnp.maximum(_seg_sum(ka * ka, HD), 1e-12))

    def head(h):
        sl = slice(h * HD, (h + 1) * HD)
        q, k, v = qa[:, sl], ka[:, sl], qkv[:, 2 * GW + h * HD:2 * GW + (h + 1) * HD]
        beta = cols[:, COL_GDN_B + h:COL_GDN_B + h + 1]
        cg_col = ccols[:, COL_GDN_A + h:COL_GDN_A + h + 1]
        cg_row = crows[COL_GDN_A + h:COL_GDN_A + h + 1, :]
        diff = cg_col - cg_row
        s0 = s_ref[h]
        kb = k * beta
        g_kk, g_qk = _mm_nt(kb, k), _mm_nt(q, k)
        q_s0 = _mm(q * jnp.exp(cg_col), s0)
        yield
        n_mat = g_kk * jnp.exp(jnp.where(strict, diff, neg))
        m = yield from _tri_inv_minus_eye(n_mat, L)
        rhs = jnp.concatenate([v * beta, kb * jnp.exp(cg_col)], axis=1)
        m_rhs = _mm(m, rhs)
        yield
        uw = rhs + m_rhs
        w_s0 = _mm(uw[:, HD:2 * HD], s0)
        yield
        u2 = uw[:, 0:HD] - w_s0
        qk = g_qk * jnp.exp(jnp.where(causal, diff, neg))
        qk_u = _mm(qk, u2)
        cg_last = cg_col[L - 1:L, :]
        s_upd = _mm_tn(k * jnp.exp(cg_last - cg_col), u2)
        yield
        s_ref[h] = jnp.exp(cg_last) * s0 + s_upd
        return q_s0 + qk_u

    def finish(ys):
        g = p_ref[:, w3:w3 + GW]
        y_ref[...] = _head_norm(jnp.concatenate(ys, axis=1), nw_ref[...], _silu(g)).astype(y_ref.dtype)

    return [head(h) for h in range(GDN_H)], finish


def _mixers_kernel(pr_ref, pg_ref, pm_ref, ph_ref, pd_ref,
                   mu_ref, vec_ref, wup_ref, aup_ref, gup_ref, gp_ref, mlw_ref, lb_ref, hgw_ref, cw_ref, gdw_ref,
                   yr_ref, sh_ref, wkv_ref, ym_ref, mc_ref, mn_ref, mm_ref, yh_ref, hg_ref, yg_ref, cv_ref, gd_ref,
                   ext_r, ext_g, *, L, nseq):
    parts = []
    for b in range(nseq):
        gate_forms = _gate_forms(pg_ref.at[b], gp_ref, L)
        parts += [
            _rwkv_parts(pr_ref.at[b], mu_ref, vec_ref, wup_ref, aup_ref, gup_ref, yr_ref.at[b], sh_ref.at[b],
                        wkv_ref.at[b], ext_r.at[b], L),
            _mlstm_parts(pm_ref.at[b], gate_forms, mlw_ref, ym_ref.at[b], mc_ref.at[b], mn_ref.at[b],
                         mm_ref.at[b], L),
            _hgrn_parts(ph_ref.at[b], lb_ref, hgw_ref, yh_ref.at[b], hg_ref.at[b], L),
            _gdn_parts(pd_ref.at[b], gate_forms, cw_ref, gdw_ref, yg_ref.at[b], cv_ref.at[b], gd_ref.at[b],
                       ext_g.at[b], L),
        ]
    results = _round_robin(g for gens, _ in parts for g in gens)
    at = 0
    for gens, finish in parts:
        finish(results[at:at + len(gens)])
        at += len(gens)


def _prompt_mixers(p3, lp):
    b, t, _ = p3.shape
    L = math.gcd(t, CHUNK)
    nseq = math.gcd(b, MIX_NSEQ)
    grid = (b // nseq, t // L)
    cp = _cparams(("parallel", "arbitrary"))
    col = lambda width, idx: pl.BlockSpec((nseq, L, width), lambda i, c: (i, c, idx))
    const2 = lambda shape: pl.BlockSpec(shape, lambda i, c: (0, 0))
    y_spec = pl.BlockSpec((nseq, L, GW), lambda i, c: (i, c, 0))
    y_shape = jax.ShapeDtypeStruct((b, t, GW), BF16)
    state = lambda *s: (pl.BlockSpec((nseq,) + s, lambda i, c: (i,) + (0,) * len(s)),
                        jax.ShapeDtypeStruct((b,) + s, F32))
    gates = col(GATES_W, GATES_OFF // GATES_W)
    y_out = (y_spec, y_shape)
    outs = [y_out, state(1, RWKV_PROJ), state(RWKV_H, RWKV_HD, RWKV_HD),
            y_out, state(ML_H, HD, HD), state(ML_H, HD), state(1, 128),
            y_out, state(HG_H, HD, HD),
            y_out, state(CONV_W - 1, 3 * GW), state(GDN_H, HD, HD)]
    specs, shapes = zip(*outs)
    yr, n_sh, n_wkv, ym, n_c, n_n, n_m, yh, n_hg, yg, n_cv, n_gd = pl.pallas_call(
        functools.partial(_mixers_kernel, L=L, nseq=nseq), grid=grid,
        in_specs=[col(RWKV_PROJ, 0), gates, col(4 * GW, 1), col(4 * GW, 2), col(4 * GW, 3),
                  const2((1, RWKV_PROJ)), const2((8, GW)), const2((64, GW)), const2((64, GW)), const2((128, GW)),
                  const2((8, 128)), const2((1, GW)), const2((1, GW)), const2((1, GW)),
                  const2((CONV_W, 3 * GW)), const2((1, GW))],
        out_specs=list(specs), out_shape=list(shapes),
        scratch_shapes=[pltpu.VMEM((nseq, L + 8, RWKV_PROJ), F32), pltpu.VMEM((nseq, L + 8, 3 * GW), F32)],
        compiler_params=cp, name="mixers",
    )(p3, p3, p3, p3, p3, lp["rwkv_mu"], lp["rwkv_vec"], lp["rwkv_w_up"], lp["rwkv_a_up"], lp["rwkv_g_up"],
      lp["gate_par"], lp["ml_norm"], lp["hg_lb"], lp["hg_norm"], lp["gdn_conv_w"], lp["gdn_norm"])

    ys = [y.reshape(b * t, GW) for y in (yr, ym, yh, yg)]
    states = (n_sh, n_wkv, n_c, n_n, n_m[:, 0, :ML_H], jnp.swapaxes(n_hg, -1, -2), n_cv, n_gd)
    return ys, states


DEC_BS = 8


def _col_to_row(col):
    n = col.shape[0]
    return jnp.sum(jnp.where(_iota((n, n), 0) == _iota((n, n), 1), col, 0.0), axis=0, keepdims=True)


def _sub_sum(x):
    return jnp.sum(x, axis=0, keepdims=True)


def _decode_kernel(p_ref, sh_ref, wkv_ref, mc_ref, mn_ref, mm_ref, hg_ref, cv_ref, gd_ref,
                   mu_ref, vec_ref, wup_ref, aup_ref, gup_ref, gp_ref, mlw_ref, lb_ref, hgw_ref,
                   cw_ref, gdw_ref,
                   yr_ref, ym_ref, yh_ref, yg_ref,
                   nsh_ref, nwkv_ref, nmc_ref, nmn_ref, nmm_ref, nhg_ref, ncv_ref, ngd_ref):
    bs = DEC_BS
    w3 = 3 * GW
    pr = p_ref[:, 0:RWKV_PROJ]
    xs = pr + (sh_ref[...] - pr) * mu_ref[...]
    nsh_ref[...] = pr
    w0, a0, k_k, k_a, r_k, ln_w, ln_b = [vec_ref[i:i + 1, :] for i in range(7)]
    r, k, v = xs[:, 0:GW], xs[:, GW:2 * GW], xs[:, 2 * GW:3 * GW]
    dw, da, dg = xs[:, 1536:1600], xs[:, 1600:1664], xs[:, 1664:1792]
    logw = -_softplus(-(w0 + _mm(jnp.tanh(dw), wup_ref[...]))) - 0.5
    wdec = jnp.exp(-jnp.exp(logw))
    a = _sigmoid(a0 + _mm(da, aup_ref[...]))
    g_r = _mm(_sigmoid(dg), gup_ref[...])
    kk = k * k_k
    kk = kk * lax.rsqrt(jnp.maximum(_seg_sum(kk * kk, RWKV_HD), 1e-12))
    k2 = k * (1.0 + (a - 1.0) * k_a)
    ra, rb = -kk, kk * a

    gates = p_ref[:, GATES_OFF:GATES_OFF + 128] + gp_ref[0:1, :]
    ml_off = GATES_OFF + GATES_W
    mq = p_ref[:, ml_off:ml_off + GW]
    mk = p_ref[:, ml_off + GW:ml_off + 2 * GW] * (HD ** -0.5)
    mv = p_ref[:, ml_off + 2 * GW:ml_off + 3 * GW]
    mo = p_ref[:, ml_off + 3 * GW:ml_off + 4 * GW]
    m_ig = gates
    m_lf = jnp.minimum(gates, 0.0) - jnp.log(1.0 + jnp.exp(-jnp.abs(gates)))

    hg_off = ml_off + 4 * GW
    lb = lb_ref[...]
    hq = _silu(p_ref[:, hg_off:hg_off + GW])
    hfg = lb + (1.0 - lb) * _sigmoid(p_ref[:, hg_off + GW:hg_off + 2 * GW])
    hv = p_ref[:, hg_off + 2 * GW:hg_off + 3 * GW]
    hgate = p_ref[:, hg_off + 3 * GW:hg_off + 4 * GW]

    gd_off = hg_off + 4 * GW
    cur = p_ref[:, gd_off:gd_off + w3]
    conv = (cw_ref[3:4, :] * cur + cw_ref[2:3, :] * cv_ref[:, 2 * w3:3 * w3]
            + cw_ref[1:2, :] * cv_ref[:, w3:2 * w3] + cw_ref[0:1, :] * cv_ref[:, 0:w3])
    ncv_ref[:, 0:2 * w3] = cv_ref[:, w3:3 * w3]
    ncv_ref[:, 2 * w3:3 * w3] = cur
    qkv = _silu(conv)
    gq, gk, gv = qkv[:, 0:GW], qkv[:, GW:2 * GW], qkv[:, 2 * GW:3 * GW]
    gq = gq * lax.rsqrt(jnp.maximum(_seg_sum(gq * gq, HD), 1e-12)) * (HD ** -0.5)
    gk = gk * lax.rsqrt(jnp.maximum(_seg_sum(gk * gk, HD), 1e-12))
    ggate = p_ref[:, gd_off + w3:gd_off + w3 + GW]
    g_beta = _sigmoid(gates)
    g_dec = jnp.exp(-jnp.exp(gp_ref[1:2, :]) * _softplus(gates))

    blocks = ([mv[:, h * HD:(h + 1) * HD] for h in range(ML_H)]
              + [hq[:, h * HD:(h + 1) * HD] for h in range(HG_H)]
              + [hfg[:, h * HD:(h + 1) * HD] for h in range(HG_H)]
              + [gq[:, h * HD:(h + 1) * HD] for h in range(GDN_H)]
              + [gk[:, h * HD:(h + 1) * HD] for h in range(GDN_H)]
              + [v[:, j * 128:(j + 1) * 128] for j in range(4)])
    xt = _mm_nt_hi(_eye(128, 128), jnp.concatenate(blocks, axis=0))
    colf = lambda blk, s: xt[:, blk * bs + s:blk * bs + s + 1]

    yr_rows, ym_rows, yh_rows, yg_rows = [], [], [], []
    for s in range(bs):
        row = lambda arr, lo, width: arr[s:s + 1, lo:lo + width]
        parts = []
        for h in range(RWKV_H):
            lo = h * RWKV_HD
            st = wkv_ref[s, h]
            v_col = colf(20 + h // 2, s)[(h % 2) * 64:(h % 2) * 64 + 64, :]
            sa = _lane_sum(st * row(ra, lo, 64))
            st = st * row(wdec, lo, 64) + sa * row(rb, lo, 64) + v_col * row(k2, lo, 64)
            nwkv_ref[s, h] = st
            parts.append(_col_to_row(_lane_sum(st * row(r, lo, 64))))
        yr_rows.append(jnp.concatenate(parts, axis=1))
        parts = []
        for h in range(ML_H):
            lo = h * HD
            q_r, k_r = row(mq, lo, HD), row(mk, lo, HD)
            ig, lf = m_ig[s:s + 1, COL_ML_I + h:COL_ML_I + h + 1], m_lf[s:s + 1, COL_ML_F + h:COL_ML_F + h + 1]
            cmat, nvec, m_prev = mc_ref[s, h], mn_ref[s, h:h + 1, :], mm_ref[s:s + 1, h:h + 1]
            gsc = lf + m_prev
            mt = jnp.maximum(gsc, ig)
            wi, wg = jnp.exp(ig - mt), jnp.exp(gsc - mt)
            sc = _lane_sum(q_r * k_r) * wi
            v_col = colf(h, s)
            num = sc * v_col + wg * _lane_sum(cmat * q_r)
            den = sc + wg * _lane_sum(nvec * q_r)
            parts.append(_col_to_row(num / jnp.maximum(jnp.abs(den), jnp.exp(-mt))))
            nmc_ref[s, h] = wg * cmat + (wi * v_col) * k_r
            nmn_ref[s, h:h + 1, :] = wg * nvec + wi * k_r
            nmm_ref[s:s + 1, h:h + 1] = mt
        ym_rows.append(jnp.concatenate(parts, axis=1))
        parts = []
        for h in range(HG_H):
            lo = h * HD
            st = hg_ref[s, h]
            q_col, fg_col = colf(4 + h, s), colf(8 + h, s)
            q_r, fg_r, v_r = row(hq, lo, HD), row(hfg, lo, HD), row(hv, lo, HD)
            parts.append(_sub_sum(st * (q_col * fg_col)) + _lane_sum(q_r * (1.0 - fg_r)) * v_r)
            nhg_ref[s, h] = fg_col * st + (1.0 - fg_col) * v_r
        yh_rows.append(jnp.concatenate(parts, axis=1))
        parts = []
        for h in range(GDN_H):
            lo = h * HD
            st = gd_ref[s, h]
            q_col, k_col = colf(12 + h, s), colf(16 + h, s)
            beta = g_beta[s:s + 1, COL_GDN_B + h:COL_GDN_B + h + 1]
            dec = g_dec[s:s + 1, COL_GDN_A + h:COL_GDN_A + h + 1]
            u2 = beta * row(gv, lo, HD) - _sub_sum(st * (k_col * (beta * dec)))
            qk = _lane_sum(row(gq, lo, HD) * row(gk, lo, HD))
            parts.append(dec * _sub_sum(st * q_col) + qk * u2)
            ngd_ref[s, h] = dec * st + k_col * u2
        yg_rows.append(jnp.concatenate(parts, axis=1))

    yr = jnp.concatenate(yr_rows, axis=0)
    mean = _seg_sum(yr, RWKV_HD) * (1.0 / RWKV_HD)
    dev = yr - mean
    var = _seg_sum(dev * dev, RWKV_HD) * (1.0 / RWKV_HD)
    yr = dev * lax.rsqrt(var + RWKV_LN_EPS) * ln_w + ln_b
    bonus = _seg_sum(r * k2 * r_k, RWKV_HD) * v
    yr_ref[...] = ((yr + bonus) * g_r).astype(yr_ref.dtype)
    ym_ref[...] = _head_norm(jnp.concatenate(ym_rows, axis=0), mlw_ref[...], _sigmoid(mo)).astype(ym_ref.dtype)
    yh_ref[...] = _head_norm(jnp.concatenate(yh_rows, axis=0), hgw_ref[...], _silu(hgate)).astype(yh_ref.dtype)
    yg_ref[...] = _head_norm(jnp.concatenate(yg_rows, axis=0), gdw_ref[...], _silu(ggate)).astype(yg_ref.dtype)


def _sample_mixers(p, states, layer, lp):
    n = p.shape[0]
    bs = DEC_BS

    def st_spec(arr):
        tail = arr.shape[2:]
        return pl.BlockSpec((None, bs) + tail, lambda i: (layer, i) + (0,) * len(tail))

    const2 = lambda shape: pl.BlockSpec(shape, lambda i: (0, 0))
    st_specs = [st_spec(a) for a in states]
    y_spec = pl.BlockSpec((bs, GW), lambda i: (i, 0))
    y_shape = jax.ShapeDtypeStruct((n, GW), BF16)
    res = pl.pallas_call(
        _decode_kernel,
        grid=(n // bs,),
        in_specs=[pl.BlockSpec((bs, PROJ_PAD), lambda i: (i, 0))] + st_specs
        + [const2((1, RWKV_PROJ)), const2((8, GW)), const2((64, GW)), const2((64, GW)), const2((128, GW)),
           const2((8, 128)), const2((1, GW)), const2((1, GW)), const2((1, GW)),
           const2((CONV_W, 3 * GW)), const2((1, GW))],
        out_specs=[y_spec] * 4 + st_specs,
        out_shape=[y_shape] * 4 + [jax.ShapeDtypeStruct(a.shape, F32) for a in states],
        input_output_aliases={1 + k: 4 + k for k in range(len(states))},
        compiler_params=_cparams(("arbitrary",)),
        name="decode_mixers",
    )(p, *states, lp["rwkv_mu"], lp["rwkv_vec"], lp["rwkv_w_up"], lp["rwkv_a_up"], lp["rwkv_g_up"],
      lp["gate_par"], lp["ml_norm"], lp["hg_lb"], lp["hg_norm"], lp["gdn_conv_w"], lp["gdn_norm"])
    return list(res[:4]), tuple(res[4:])


def _layer_params(l, a):
    gate_par = jnp.zeros((8, 128), F32)
    gate_par = gate_par.at[0, COL_ML_I:COL_ML_I + 4].set(a["ml_i_bias"][l])
    gate_par = gate_par.at[0, COL_ML_F:COL_ML_F + 4].set(a["ml_f_bias"][l])
    gate_par = gate_par.at[0, COL_GDN_A:COL_GDN_A + 4].set(a["gdn_dt_bias"][l])
    gate_par = gate_par.at[1, COL_GDN_A:COL_GDN_A + 4].set(a["gdn_a_log"][l])
    vec = jnp.stack([a[k][l] for k in ("rwkv_w0", "rwkv_a0", "rwkv_k_k", "rwkv_k_a", "rwkv_r_k",
                                        "rwkv_ln_w", "rwkv_ln_b")] + [jnp.zeros((GW,), F32)])
    return {
        "rwkv_mu": a["rwkv_mu"][l].reshape(1, RWKV_PROJ), "rwkv_vec": vec,
        "rwkv_w_up": a["rwkv_w_up"][l], "rwkv_a_up": a["rwkv_a_up"][l], "rwkv_g_up": a["rwkv_g_up"][l],
        "gate_par": gate_par, "ml_norm": a["ml_norm"][l].reshape(1, GW),
        "hg_lb": a["hg_lbs"][l].reshape(1, GW), "hg_norm": a["hg_norm"][l].reshape(1, GW),
        "gdn_conv_w": a["gdn_conv_w"][l], "gdn_norm": a["gdn_norm"][l].reshape(1, GW),
    }


def _pad_w_in(w_in):
    z = jnp.zeros(w_in.shape[:2] + (GATES_W - 16,), w_in.dtype)
    parts = [w_in[..., 0:1792], w_in[..., 3840:3848], w_in[..., 7944:7952], z,
             w_in[..., 1792:3840], w_in[..., 3848:5896], w_in[..., 5896:7944]]
    return jnp.concatenate(parts, axis=-1).astype(BF16)


def kernel(x_prompt, x_sample, state_rwkv_shift, state_rwkv_wkv, state_mlstm_c, state_mlstm_n,
           state_mlstm_m, state_hgrn, state_gdn_conv, state_gdn, c_prompt, c_sample,
           ada_w, ada_b, norm1, norm2, norm_f, w_in, w_out,
           rwkv_mu, rwkv_w0, rwkv_w_up, rwkv_a0, rwkv_a_up, rwkv_g_up, rwkv_k_k, rwkv_k_a, rwkv_r_k,
           rwkv_ln_w, rwkv_ln_b, ml_i_bias, ml_f_bias, ml_norm, hg_lb, hg_norm,
           gdn_conv_w, gdn_a_log, gdn_dt_bias, gdn_norm,
           moe_w_group, moe_b_group, moe_w_router, moe_b_router, moe_w_gate, moe_w_up, moe_w_down):
    bp, t, _ = x_prompt.shape
    ns = x_sample.shape[0]
    lbs = jax.nn.softmax(hg_lb.astype(F32), axis=0)
    hg_lbs = jnp.cumsum(lbs, axis=0) - lbs[0]
    a = dict(rwkv_mu=rwkv_mu, rwkv_w0=rwkv_w0, rwkv_w_up=rwkv_w_up, rwkv_a0=rwkv_a0, rwkv_a_up=rwkv_a_up,
             rwkv_g_up=rwkv_g_up, rwkv_k_k=rwkv_k_k, rwkv_k_a=rwkv_k_a, rwkv_r_k=rwkv_r_k,
             rwkv_ln_w=rwkv_ln_w, rwkv_ln_b=rwkv_ln_b, ml_i_bias=ml_i_bias, ml_f_bias=ml_f_bias,
             ml_norm=ml_norm, hg_lbs=hg_lbs, hg_norm=hg_norm, gdn_conv_w=gdn_conv_w, gdn_a_log=gdn_a_log,
             gdn_dt_bias=gdn_dt_bias, gdn_norm=gdn_norm)
    w_in_p = _pad_w_in(w_in)
    w_out_b = w_out.astype(BF16)
    w_route = jnp.concatenate([moe_w_group, moe_w_router,
                               jnp.zeros((DEPTH, D_MODEL, 128 - N_GROUPS - N_EXPERTS), F32)], axis=-1)
    w_route_hi = w_route.astype(BF16)
    w_route_lo = (w_route - w_route_hi.astype(F32)).astype(BF16)
    w_route = jnp.concatenate([w_route_hi, w_route_lo], axis=-1)
    b_route = jnp.concatenate([moe_b_group, moe_b_router,
                               jnp.zeros((DEPTH, 128 - N_GROUPS - N_EXPERTS), F32)], axis=-1)

    mod = _ada(jnp.concatenate([c_prompt, c_sample], axis=0), ada_w, ada_b)
    sample_states = (state_rwkv_shift.reshape(DEPTH, ns, RWKV_PROJ), state_rwkv_wkv, state_mlstm_c,
                     state_mlstm_n, state_mlstm_m, state_hgrn,
                     state_gdn_conv.reshape(DEPTH, ns, (CONV_W - 1) * 3 * GW), state_gdn)

    xp = x_prompt.reshape(bp * t, D_MODEL)
    xs = x_sample.reshape(ns, D_MODEL)
    n_p = bp * t
    ff = None
    modp = mods = None
    new_p = []
    for l in range(DEPTH):
        lp = _layer_params(l, a)
        prev_modp, prev_mods = modp, mods
        modp = mod[l, :bp].reshape(bp, 1, 6 * D_MODEL)
        mods = mod[l, bp:].reshape(1, ns, 6 * D_MODEL)

        xp, pp = _in_proj(xp, modp, t, norm1[l], w_in_p[l], ff, 0, prev_modp)
        xs, ps = _in_proj(xs, mods, ns, norm1[l], w_in_p[l], ff, n_p, prev_mods)
        ysp, stp = _prompt_mixers(pp.reshape(bp, t, PROJ_PAD), lp)
        yss, sample_states = _sample_mixers(ps, sample_states, l, lp)
        new_p.append(stp)
        xp, xs, h2, rt = _out_proj(ysp, xp, modp, t, yss, xs, mods, norm2[l], w_out_b[l],
                                   w_route[l], b_route[l].reshape(1, 128))
        ff = _moe(h2, rt, n_p + ns, moe_w_gate, moe_w_up, moe_w_down, l)
    yp = _final(xp, ff, 0, modp, t, norm_f).reshape(bp, t, D_MODEL)
    ys = _final(xs, ff, n_p, mods, ns, norm_f).reshape(ns, 1, D_MODEL)
    prompt_states = tuple(jnp.stack([st[i] for st in new_p]) for i in range(8))
    s_sh, s_wkv, s_mc, s_mn, s_mm, s_hg, s_cv, s_gd = sample_states
    return (yp, ys) + prompt_states + (s_sh.reshape(DEPTH, ns, 1, RWKV_PROJ), s_wkv, s_mc, s_mn, s_mm, s_hg,
                                       s_cv.reshape(DEPTH, ns, CONV_W - 1, 3 * GW), s_gd)
```

```python
import functools
import math

import jax
import jax.numpy as jnp
from jax import lax
from jax.experimental import pallas as pl
from jax.experimental.pallas import tpu as pltpu

F32 = jnp.float32
BF16 = jnp.bfloat16
HI = lax.Precision.HIGHEST

D_MODEL = 2048
DEPTH = 4
GW = 512
RWKV_HD = 64
RWKV_H = 8
RWKV_PROJ = 1792
RWKV_LN_EPS = 64e-5
ML_H = 4
HG_H = 4
GDN_H = 4
HD = 128
CONV_W = 4
N_GROUPS = 4
EPG = 8
N_EXPERTS = 32
D_FF = 256
NORM_EPS = 1e-6

PROJ_PAD = 8192
GATES_OFF = 1792
GATES_W = 256
COL_ML_I, COL_ML_F, COL_GDN_B, COL_GDN_A = 0, 4, 8, 12

CHUNK = 64
MIX_NSEQ = 2
SUB = 16
VMEM_LIMIT = 56 * 1024 * 1024


def _cparams(sem):
    return pltpu.CompilerParams(dimension_semantics=sem, vmem_limit_bytes=VMEM_LIMIT)


def _mm(a, b):
    return jnp.dot(a.astype(BF16), b.astype(BF16), preferred_element_type=F32)


def _mm_nt(a, b):
    return lax.dot_general(a.astype(BF16), b.astype(BF16), (((1,), (1,)), ((), ())),
                           preferred_element_type=F32)


def _mm_tn(a, b):
    return lax.dot_general(a.astype(BF16), b.astype(BF16), (((0,), (0,)), ((), ())),
                           preferred_element_type=F32)


def _mm_hi(a, b):
    return jnp.dot(a, b, precision=HI, preferred_element_type=F32)


def _mm_nt_hi(a, b):
    return lax.dot_general(a, b, (((1,), (1,)), ((), ())), precision=HI, preferred_element_type=F32)


def _sigmoid(x):
    return 1.0 / (1.0 + jnp.exp(-x))


def _silu(x):
    return x * _sigmoid(x)


def _softplus(x):
    return jnp.maximum(x, 0.0) + jnp.log(1.0 + jnp.exp(-jnp.abs(x)))


def _iota(shape, dim):
    return lax.broadcasted_iota(jnp.int32, shape, dim)


def _eye(n, m):
    return (_iota((n, m), 0) == _iota((n, m), 1)).astype(F32)


def _lower(n, inclusive):
    r, c = _iota((n, n), 0), _iota((n, n), 1)
    return (c <= r) if inclusive else (c < r)


def _rms_rows(x):
    return x * lax.rsqrt(jnp.mean(x * x, axis=-1, keepdims=True) + NORM_EPS)


def _tri_inv_minus_eye(n_mat, size):
    m = -n_mat
    steps = int(math.log2(size))
    pw = _mm(n_mat, n_mat)
    yield
    for i in range(1, steps):
        t = _mm(m, pw)
        nxt = _mm(pw, pw) if i < steps - 1 else None
        yield
        m = m + pw + t
        pw = nxt
    return m


def _round_robin(gens):
    gens = list(gens)
    results = [None] * len(gens)
    active = list(range(len(gens)))
    while active:
        for i in list(active):
            try:
                next(gens[i])
            except StopIteration as stop:
                results[i] = stop.value
                active.remove(i)
    return results


def _ada_kernel(c_ref, w_ref, b_ref, o_ref):
    c = c_ref[...]
    o_ref[...] = _mm(_silu(c), w_ref[...]) + b_ref[...]


def _ada(c_all, ada_w, ada_b):
    depth, d, n6 = ada_w.shape
    rows = c_all.shape[0]
    tn = 1024
    return pl.pallas_call(
        _ada_kernel,
        grid=(depth, n6 // tn),
        in_specs=[pl.BlockSpec((rows, d), lambda l, j: (0, 0)),
                  pl.BlockSpec((None, d, tn), lambda l, j: (l, 0, j)),
                  pl.BlockSpec((None, 1, tn), lambda l, j: (l, 0, j))],
        out_specs=pl.BlockSpec((None, rows, tn), lambda l, j: (l, 0, j)),
        out_shape=jax.ShapeDtypeStruct((depth, rows, n6), F32),
        compiler_params=_cparams(("parallel", "parallel")),
        name="ada",
    )(c_all, ada_w, ada_b.reshape(depth, 1, n6))


def _modulated_norm(x, nw, sc, sh):
    return _rms_rows(x) * nw * (1.0 + sc) + sh


def _inproj_kernel(*refs, combine):
    if combine:
        x_ref, ff_ref, gt_ref, sh_ref, sc_ref, nw_ref, w_ref, xo_ref, p_ref, h_ref = refs
    else:
        x_ref, sh_ref, sc_ref, nw_ref, w_ref, p_ref, h_ref = refs

    @pl.when(pl.program_id(1) == 0)
    def _():
        x = x_ref[...]
        if combine:
            rows = x.shape[0]
            x = x + gt_ref[...] * (_load_slabs(ff_ref.at[0], rows) + _load_slabs(ff_ref.at[1], rows))
            xo_ref[...] = x
        h_ref[...] = _modulated_norm(x, nw_ref[...], sc_ref[...], sh_ref[...]).astype(BF16)

    p_ref[...] = jnp.dot(h_ref[...], w_ref[...], preferred_element_type=F32)


def _mod_spec(mod3, which, tm, rows_per_group):
    r = mod3.shape[1]
    return pl.BlockSpec((None, r, D_MODEL),
                        lambda i, *_: ((i * tm) // rows_per_group, 0, which))


def _in_proj(x, mod3, rows_per_group, nw, w, ff=None, ff_row0=0, gate_mod3=None):
    n = x.shape[0]
    tm = min(512, rows_per_group)
    tn = 1024
    combine = ff is not None
    row = lambda i, j: (i, 0)
    in_specs = [pl.BlockSpec((tm, D_MODEL), row)]
    args = [x]
    if combine:
        blk0 = ff_row0 // tm
        in_specs += [pl.BlockSpec((2, tm * SLAB, 128), lambda i, j: (0, blk0 + i, 0)),
                     _mod_spec(gate_mod3, 5, tm, rows_per_group)]
        args += [ff, gate_mod3]
    in_specs += [_mod_spec(mod3, 0, tm, rows_per_group), _mod_spec(mod3, 1, tm, rows_per_group),
                 pl.BlockSpec((1, D_MODEL), lambda i, j: (0, 0)),
                 pl.BlockSpec((D_MODEL, tn), lambda i, j: (0, j))]
    args += [mod3, mod3, nw.reshape(1, D_MODEL), w]
    out_specs = [pl.BlockSpec((tm, tn), lambda i, j: (i, j))]
    out_shape = [jax.ShapeDtypeStruct((n, PROJ_PAD), F32)]
    if combine:
        out_specs = [pl.BlockSpec((tm, D_MODEL), row)] + out_specs
        out_shape = [jax.ShapeDtypeStruct((n, D_MODEL), F32)] + out_shape
    res = pl.pallas_call(
        functools.partial(_inproj_kernel, combine=combine),
        grid=(n // tm, PROJ_PAD // tn),
        in_specs=in_specs, out_specs=out_specs, out_shape=out_shape,
        scratch_shapes=[pltpu.VMEM((tm, D_MODEL), BF16)],
        compiler_params=_cparams(("parallel", "arbitrary")),
        name="in_proj",
    )(*args)
    return (res[0], res[1]) if combine else (x, res[0])


def _lane_max(x):
    return jnp.max(x, axis=-1, keepdims=True)


def _lane_sum(x):
    return jnp.sum(x, axis=-1, keepdims=True)


def _first_lane_of(mask, lane):
    return jnp.min(jnp.where(mask, lane, 4096), axis=-1, keepdims=True)


def _route(logits, biased):
    lane = _iota(logits.shape, 1)
    neg = jnp.float32(-jnp.inf)
    is_g = lane < N_GROUPS
    gmax = _lane_max(jnp.where(is_g, logits, neg))
    gexp = jnp.where(is_g, jnp.exp(logits - gmax), 0.0)
    gb = jnp.where(is_g, biased, neg)
    gsel = _first_lane_of(gb == _lane_max(gb), lane)
    pg = _lane_sum(jnp.where(lane == gsel, gexp, 0.0)) / _lane_sum(gexp)
    lo = N_GROUPS + gsel * EPG
    eb = jnp.where((lane >= lo) & (lane < lo + EPG), biased, neg)
    i1 = _first_lane_of(eb == _lane_max(eb), lane)
    eb2 = jnp.where(lane == i1, neg, eb)
    i2 = _first_lane_of(eb2 == _lane_max(eb2), lane)
    l1 = _lane_sum(jnp.where(lane == i1, logits, 0.0))
    l2 = _lane_sum(jnp.where(lane == i2, logits, 0.0))
    mx = jnp.maximum(l1, l2)
    e1, e2 = jnp.exp(l1 - mx), jnp.exp(l2 - mx)
    cw1, cw2 = pg * e1 / (e1 + e2), pg * e2 / (e1 + e2)
    id1, id2 = (i1 - N_GROUPS).astype(F32), (i2 - N_GROUPS).astype(F32)
    return jnp.where(lane == 0, id1, jnp.where(lane == 1, id2,
                     jnp.where(lane == 2, cw1, jnp.where(lane == 3, cw2, 0.0))))


SLAB = D_MODEL // 128


def _load_slabs(ref, rows):
    return jnp.concatenate([ref[pl.ds(s, rows, stride=SLAB), :] for s in range(SLAB)], axis=1)


def _store_slabs(ref, x, rows):
    for s in range(SLAB):
        ref[pl.ds(s, rows, stride=SLAB), :] = x[:, s * 128:(s + 1) * 128]


def _outproj_kernel(*refs, n_prompt_tiles, tm, rows_s):
    group_p, group_s = refs[0:8], refs[8:16]
    nw_ref, wo_ref, wr_ref, br_ref, xop_ref, xos_ref, h2_ref, rt_ref = refs[16:]
    i = pl.program_id(0)

    def body(group, xo_ref, rows):
        yr_ref, ym_ref, yh_ref, yg_ref, x_ref, gt_ref, sh_ref, sc_ref = group
        mix = jnp.dot(yr_ref[...], wo_ref[0:GW, :], preferred_element_type=F32)
        mix += jnp.dot(ym_ref[...], wo_ref[GW:2 * GW, :], preferred_element_type=F32)
        mix += jnp.dot(yh_ref[...], wo_ref[2 * GW:3 * GW, :], preferred_element_type=F32)
        mix += jnp.dot(yg_ref[...], wo_ref[3 * GW:4 * GW, :], preferred_element_type=F32)
        x = x_ref[...] + gt_ref[...] * mix
        xo_ref[...] = x
        h2 = _modulated_norm(x, nw_ref[...], sc_ref[...], sh_ref[...])
        h_hi = h2.astype(BF16)
        h_lo = (h2 - h_hi.astype(F32)).astype(BF16)
        hw = jnp.dot(h_hi, wr_ref[...], preferred_element_type=F32)
        logits = hw[:, 0:128] + hw[:, 128:256] + jnp.dot(h_lo, wr_ref[:, 0:128], preferred_element_type=F32)
        route = _route(logits, logits + br_ref[...])
        _store_slabs(h2_ref, h2, rows)
        rt_ref[0:rows, :] = route
        if rows < tm:
            h2_ref[rows * SLAB:tm * SLAB, :] = jnp.zeros(((tm - rows) * SLAB, 128), F32)
            rt_ref[rows:tm, :] = jnp.zeros((tm - rows, 128), F32)

    @pl.when(i < n_prompt_tiles)
    def _():
        body(group_p, xop_ref, tm)

    @pl.when(i == n_prompt_tiles)
    def _():
        body(group_s, xos_ref, rows_s)


def _out_proj(ys_p, x_p, mod_p, t, ys_s, x_s, mod_s, nw, wo, w_route, b_route):
    n_p, n_s = x_p.shape[0], x_s.shape[0]
    tm = min(256, t)
    assert n_s <= tm and n_p % tm == 0
    n_pt = n_p // tm
    prow = lambda i: (jnp.minimum(i, n_pt - 1), 0)
    const = lambda i: (0, 0)
    pmod = lambda which: pl.BlockSpec((None, 1, D_MODEL),
                                      lambda i: ((jnp.minimum(i, n_pt - 1) * tm) // t, 0, which))
    smod = lambda which: pl.BlockSpec((None, n_s, D_MODEL), lambda i: (0, 0, which))
    in_specs = ([pl.BlockSpec((tm, GW), prow)] * 4 + [pl.BlockSpec((tm, D_MODEL), prow), pmod(2), pmod(3), pmod(4)]
                + [pl.BlockSpec((n_s, GW), const)] * 4 + [pl.BlockSpec((n_s, D_MODEL), const),
                                                          smod(2), smod(3), smod(4)]
                + [pl.BlockSpec((1, D_MODEL), const), pl.BlockSpec((D_MODEL, D_MODEL), const),
                   pl.BlockSpec((D_MODEL, 256), const), pl.BlockSpec((1, 128), const)])
    n_all = n_p + tm
    return pl.pallas_call(
        functools.partial(_outproj_kernel, n_prompt_tiles=n_pt, tm=tm, rows_s=n_s),
        grid=(n_pt + 1,),
        in_specs=in_specs,
        out_specs=[pl.BlockSpec((tm, D_MODEL), prow), pl.BlockSpec((n_s, D_MODEL), const),
                   pl.BlockSpec((tm * SLAB, 128), lambda i: (i, 0)), pl.BlockSpec((tm, 128), lambda i: (i, 0))],
        out_shape=[jax.ShapeDtypeStruct((n_p, D_MODEL), F32), jax.ShapeDtypeStruct((n_s, D_MODEL), F32),
                   jax.ShapeDtypeStruct((n_all * SLAB, 128), F32), jax.ShapeDtypeStruct((n_all, 128), F32)],
        compiler_params=_cparams(("arbitrary",)),
        name="out_proj",
    )(*ys_p, x_p, mod_p, mod_p, mod_p, *ys_s, x_s, mod_s, mod_s, mod_s,
      nw.reshape(1, D_MODEL), wo, w_route, b_route)


def _moe_kernel(te_ref, nv_ref, tok_ref, dst_ref, h2_hbm, cw_ref, wg_ref, wu_ref, wd_ref, out_hbm,
                hbuf0, hbuf1, obuf0, obuf1, wgb, wub, wdb, gsem, ssem, *, tm, n_tiles):
    t = pl.program_id(0)
    hbufs, obufs = (hbuf0, hbuf1), (obuf0, obuf1)

    def row_copy(src_ref, src, dst_ref_, dst, sem):
        return pltpu.make_async_copy(src_ref.at[pl.ds(pl.multiple_of(src, SLAB), SLAB)],
                                     dst_ref_.at[pl.ds(pl.multiple_of(dst, SLAB), SLAB)], sem)

    def start_gather(tile, slot):
        def body(r, carry):
            row_copy(h2_hbm, tok_ref[tile * tm + r], hbufs[slot], r * SLAB, gsem.at[slot]).start()
            return carry
        lax.fori_loop(0, nv_ref[tile], body, 0)

    def start_scatter(tile, slot):
        def body(r, carry):
            row_copy(obufs[slot], r * SLAB, out_hbm, dst_ref[tile * tm + r], ssem.at[slot]).start()
            return carry
        lax.fori_loop(0, nv_ref[tile], body, 0)

    def wait_rows(src_ref, dst_ref_, sem, count):
        def body(r, carry):
            row_copy(src_ref, 0, dst_ref_, 0, sem).wait()
            return carry
        lax.fori_loop(0, count, body, 0)

    def wait_gather(tile, slot):
        wait_rows(h2_hbm, hbufs[slot], gsem.at[slot], nv_ref[tile])

    def wait_scatter(tile, slot):
        wait_rows(obufs[slot], out_hbm, ssem.at[slot], nv_ref[tile])

    @pl.when(t == 0)
    def _():
        hbuf0[...] = jnp.zeros_like(hbuf0)
        hbuf1[...] = jnp.zeros_like(hbuf1)
        start_gather(0, 0)

    @pl.when((t == 0) | (te_ref[t] != te_ref[jnp.maximum(t - 1, 0)]))
    def _():
        wgb[...] = wg_ref[...].astype(BF16)
        wub[...] = wu_ref[...].astype(BF16)
        wdb[...] = wd_ref[...].astype(BF16)

    def step(slot):
        wait_gather(t, slot)
        start_gather(t + 1, 1 - slot)
        h = _load_slabs(hbufs[slot], tm).astype(BF16)
        g = jnp.dot(h, wgb[...], preferred_element_type=F32)
        u = jnp.dot(h, wub[...], preferred_element_type=F32)
        hid = (_silu(g) * u * cw_ref[...]).astype(BF16)
        _store_slabs(obufs[slot], jnp.dot(hid, wdb[...], preferred_element_type=F32), tm)
        start_scatter(t, slot)

    for slot in (0, 1):
        @pl.when(t % 2 == slot)
        def _(slot=slot):
            step(slot)

        @pl.when((t >= 1) & (t % 2 == slot))
        def _(slot=slot):
            wait_scatter(t - 1, 1 - slot)

    @pl.when(t == n_tiles - 1)
    def _():
        wait_scatter(n_tiles - 1, (n_tiles - 1) % 2)


MOE_TM = 256


def _moe(h2, route, n_tok, w_gate, w_up, w_down, layer):
    tm = MOE_TM
    slots = 2 * n_tok
    n_tiles = -(-(slots + N_EXPERTS * (tm - 1)) // tm)
    n_pad = n_tok
    n_rows = n_tiles * tm
    eid = jnp.concatenate([route[:n_tok, 0], route[:n_tok, 1]]).astype(jnp.int32)
    cw = jnp.concatenate([route[:n_tok, 2], route[:n_tok, 3]])
    onehot = (eid[:, None] == jnp.arange(N_EXPERTS, dtype=jnp.int32)[None, :]).astype(jnp.int32)
    n_blk = -(-slots // tm)
    oh3 = jnp.pad(onehot, ((0, n_blk * tm - slots), (0, 0))).reshape(n_blk, tm, N_EXPERTS).astype(BF16)
    tri = (jnp.arange(tm)[:, None] >= jnp.arange(tm)[None, :]).astype(BF16)
    within = jnp.einsum("ij,bjk->bik", tri, oh3, preferred_element_type=F32).astype(jnp.int32)
    blk_tot = within[:, -1, :]
    blk_off = jnp.cumsum(blk_tot, axis=0) - blk_tot
    running = (within + blk_off[:, None, :]).reshape(n_blk * tm, N_EXPERTS)[:slots]
    counts = jnp.sum(blk_tot, axis=0)
    tiles_per = (counts + tm - 1) // tm
    tile_end = jnp.cumsum(tiles_per)
    tile_start = tile_end - tiles_per
    rank = jnp.sum(running * onehot, axis=1) - 1
    pos = jnp.sum(onehot * tile_start[None, :], axis=1) * tm + rank
    s_idx = jnp.arange(slots, dtype=jnp.int32)
    tok = s_idx % n_tok
    dst = (s_idx // n_tok) * n_pad + tok
    packed = jnp.stack([tok.astype(F32), dst.astype(F32), cw], axis=1)
    rows = jnp.zeros((n_rows, 3), F32).at[pos].set(packed)
    row_tok = rows[:, 0].astype(jnp.int32) * SLAB
    row_dst = rows[:, 1].astype(jnp.int32) * SLAB
    row_cw = rows[:, 2].reshape(n_rows, 1)
    tile_ids = jnp.arange(n_tiles, dtype=jnp.int32)
    tile_e = jnp.sum((tile_end[None, :] <= tile_ids[:, None]).astype(jnp.int32), axis=1)
    tile_e = jnp.minimum(tile_e, N_EXPERTS - 1)
    te_oh = (tile_e[:, None] == jnp.arange(N_EXPERTS, dtype=jnp.int32)[None, :]).astype(jnp.int32)
    tile_cnt = jnp.sum(te_oh * counts[None, :], axis=1)
    tile_first = jnp.sum(te_oh * tile_start[None, :], axis=1)
    tile_nv = jnp.clip(tile_cnt - (tile_ids - tile_first) * tm, 0, tm)
    tile_nv = jnp.where(tile_ids < tile_end[-1], tile_nv, 0)
    tile_nv = jnp.concatenate([tile_nv, jnp.zeros((1,), jnp.int32)]).astype(jnp.int32)

    wspec = lambda shape: pl.BlockSpec((None, None) + shape, lambda t, te, *_: (layer, te[t], 0, 0))
    grid_spec = pltpu.PrefetchScalarGridSpec(
        num_scalar_prefetch=4,
        grid=(n_tiles,),
        in_specs=[pl.BlockSpec(memory_space=pl.ANY),
                  pl.BlockSpec((tm, 1), lambda t, *_: (t, 0)),
                  wspec((D_MODEL, D_FF)), wspec((D_MODEL, D_FF)), wspec((D_FF, D_MODEL))],
        out_specs=pl.BlockSpec(memory_space=pl.ANY),
        scratch_shapes=[pltpu.VMEM((tm * SLAB, 128), F32)] * 4
        + [pltpu.VMEM((D_MODEL, D_FF), BF16), pltpu.VMEM((D_MODEL, D_FF), BF16),
           pltpu.VMEM((D_FF, D_MODEL), BF16),
           pltpu.SemaphoreType.DMA((2,)), pltpu.SemaphoreType.DMA((2,))])
    out = pl.pallas_call(
        functools.partial(_moe_kernel, tm=tm, n_tiles=n_tiles),
        grid_spec=grid_spec,
        out_shape=jax.ShapeDtypeStruct((2 * n_pad * SLAB, 128), F32),
        compiler_params=_cparams(("arbitrary",)),
        name="moe",
    )(tile_e, tile_nv, row_tok, row_dst, h2, row_cw, w_gate, w_up, w_down)
    return out.reshape(2, n_pad * SLAB, 128)


def _final_kernel(x_ref, ff_ref, gt_ref, nw_ref, o_ref):
    rows = x_ref.shape[0]
    x = x_ref[...] + gt_ref[...] * (_load_slabs(ff_ref.at[0], rows) + _load_slabs(ff_ref.at[1], rows))
    o_ref[...] = _rms_rows(x) * nw_ref[...]


def _final(x, ff, ff_row0, mod3, rows_per_group, nw):
    n = x.shape[0]
    tm = min(512, rows_per_group)
    blk0 = ff_row0 // tm
    return pl.pallas_call(
        _final_kernel,
        grid=(n // tm,),
        in_specs=[pl.BlockSpec((tm, D_MODEL), lambda i: (i, 0)),
                  pl.BlockSpec((2, tm * SLAB, 128), lambda i: (0, blk0 + i, 0)),
                  _mod_spec(mod3, 5, tm, rows_per_group),
                  pl.BlockSpec((1, D_MODEL), lambda i: (0, 0))],
        out_specs=pl.BlockSpec((tm, D_MODEL), lambda i: (i, 0)),
        out_shape=jax.ShapeDtypeStruct((n, D_MODEL), F32),
        compiler_params=_cparams(("parallel",)),
        name="final_norm",
    )(x, ff, mod3, nw.reshape(1, D_MODEL))


def _seg_sum(x, width):
    seg = (_iota((128, 128), 0) // width == _iota((128, 128), 1) // width).astype(F32)
    parts = [_mm_hi(x[:, j:j + 128], seg) for j in range(0, x.shape[1], 128)]
    return parts[0] if len(parts) == 1 else jnp.concatenate(parts, axis=1)


def _head_norm(o, nw, gate):
    return o * lax.rsqrt(_seg_sum(o * o, HD) * (1.0 / HD) + NORM_EPS) * nw * gate


def _carry_rows(ext_ref, cur, first, L):
    @pl.when(first)
    def _():
        ext_ref[0:8, :] = jnp.zeros((8, ext_ref.shape[1]), F32)

    ext_ref[8:8 + L, :] = cur


def _rwkv_parts(p_ref, mu_ref, vec_ref, wup_ref, aup_ref, gup_ref, y_ref, sh_ref, s_ref, ext_ref, L):
    c = pl.program_id(1)
    p = p_ref[...]
    _carry_rows(ext_ref, p, c == 0, L)
    prev = ext_ref[7:7 + L, :]
    xs = p + (prev - p) * mu_ref[...]
    ext_ref[0:8, :] = ext_ref[L:L + 8, :]
    sh_ref[...] = p[L - 1:L, :]

    @pl.when(c == 0)
    def _():
        s_ref[...] = jnp.zeros_like(s_ref)

    w0, a0, k_k, k_a, r_k, ln_w, ln_b = [vec_ref[i:i + 1, :] for i in range(7)]
    r, k, v = xs[:, 0:GW], xs[:, GW:2 * GW], xs[:, 2 * GW:3 * GW]
    dw, da, dg = xs[:, 1536:1600], xs[:, 1600:1664], xs[:, 1664:1792]
    logw = -_softplus(-(w0 + _mm(jnp.tanh(dw), wup_ref[...]))) - 0.5
    lw = -jnp.exp(logw)
    a = _sigmoid(a0 + _mm(da, aup_ref[...]))
    g = _mm(_sigmoid(dg), gup_ref[...])
    kk = k * k_k
    kk = kk * lax.rsqrt(jnp.maximum(_seg_sum(kk * kk, RWKV_HD), 1e-12))
    k2 = k * (1.0 + (a - 1.0) * k_a)
    cl = _mm_hi(_lower(L, True).astype(F32), lw)
    cl_last = cl[L - 1:L, :]
    at = -kk * jnp.exp(cl - lw)
    bt = kk * a * jnp.exp(-cl)
    kt = k2 * jnp.exp(-cl)
    rt = r * jnp.exp(cl)
    b_end = kk * a * jnp.exp(cl_last - cl)
    k_end = k2 * jnp.exp(cl_last - cl)
    strict, incl = _lower(L, False), _lower(L, True)
    def head(h):
        sl = slice(h * RWKV_HD, (h + 1) * RWKV_HD)
        s0 = s_ref[h]
        ath, bth, kth, rth, vh = at[:, sl], bt[:, sl], kt[:, sl], rt[:, sl], v[:, sl]
        g_ab, g_ak, g_rb, g_rk = _mm_nt(ath, bth), _mm_nt(ath, kth), _mm_nt(rth, bth), _mm_nt(rth, kth)
        a_s0, r_s0 = _mm_nt(ath, s0), _mm_nt(rth, s0)
        s_v = _mm_tn(vh, k_end[:, sl])
        yield
        n_ab = jnp.where(strict, g_ab, 0.0)
        ak_v = _mm(jnp.where(strict, g_ak, 0.0), vh)
        rk_v = _mm(jnp.where(incl, g_rk, 0.0), vh)
        m = yield from _tri_inv_minus_eye(-n_ab, L)
        rhs = a_s0 + ak_v
        m_rhs = _mm(m, rhs)
        yield
        u = rhs + m_rhs
        rb_u = _mm(jnp.where(incl, g_rb, 0.0), u)
        s_u = _mm_tn(u, b_end[:, sl])
        yield
        s_ref[h] = s0 * jnp.exp(cl_last[:, sl]) + s_u + s_v
        return r_s0 + rb_u + rk_v

    def finish(ys):
        y = jnp.concatenate(ys, axis=1)
        mean = _seg_sum(y, RWKV_HD) * (1.0 / RWKV_HD)
        dev = y - mean
        var = _seg_sum(dev * dev, RWKV_HD) * (1.0 / RWKV_HD)
        y = dev * lax.rsqrt(var + RWKV_LN_EPS) * ln_w + ln_b
        bonus = _seg_sum(r * k2 * r_k, RWKV_HD) * v
        y_ref[...] = ((y + bonus) * g).astype(y_ref.dtype)

    return [head(h) for h in range(RWKV_H)], finish


def _gate_forms(g_ref, gp_ref, L):
    raw = g_ref[:, 0:128] + gp_ref[0:1, :]
    lane = _iota(raw.shape, 1)
    lf = jnp.minimum(raw, 0.0) - jnp.log(1.0 + jnp.exp(-jnp.abs(raw)))
    beta = _sigmoid(raw)
    decay = -jnp.exp(gp_ref[1:2, :]) * _softplus(raw)
    cols = jnp.where(lane < 4, raw, jnp.where(lane < 8, lf, jnp.where(lane < 12, beta, decay)))
    rows = _mm_nt_hi(_eye(16, 128), cols)
    ccols = _mm_hi(_lower(L, True).astype(F32), cols)
    upper = (_iota((L, L), 0) <= _iota((L, L), 1)).astype(F32)
    crows = _mm_hi(rows, upper)
    return cols, rows, ccols, crows


def _mlstm_parts(p_ref, gate_forms, nw_ref, y_ref, c_ref, n_ref, m_ref, L):
    c = pl.program_id(1)

    @pl.when(c == 0)
    def _():
        c_ref[...] = jnp.zeros_like(c_ref)
        n_ref[...] = jnp.zeros_like(n_ref)
        m_ref[...] = jnp.zeros_like(m_ref)

    cols, rows, ccols, crows = gate_forms
    causal = _lower(L, True)
    neg = jnp.float32(-jnp.inf)
    def head(h):
        q = p_ref[:, h * HD:(h + 1) * HD]
        k = p_ref[:, GW + h * HD:GW + (h + 1) * HD] * (HD ** -0.5)
        v = p_ref[:, 2 * GW + h * HD:2 * GW + (h + 1) * HD]
        ig_col, ig_row = cols[:, h:h + 1], rows[h:h + 1, :]
        b_col, b_row = ccols[:, 4 + h:5 + h], crows[4 + h:5 + h, :]
        m_prev = m_ref[:, h:h + 1]
        cmat, nvec = c_ref[h], n_ref[h:h + 1, :]
        qk, qc = _mm_nt(q, k), _mm_nt(q, cmat)
        dmat = jnp.where(causal, b_col - b_row + ig_row, neg)
        gcol = b_col + m_prev
        mt = jnp.maximum(gcol, _lane_max(dmat))
        m_last = mt[L - 1:L, :]
        wk = jnp.exp(b_col[L - 1:L, :] - b_col + ig_col - m_last)
        decay = jnp.exp(gcol[L - 1:L, :] - m_last)
        c_upd = _mm_tn(v * wk, k)
        yield
        smat = qk * jnp.exp(dmat - mt)
        s_v = _mm(smat, v)
        wg = jnp.exp(gcol - mt)
        den = _lane_sum(smat) + wg * _lane_sum(q * nvec)
        c_ref[h] = decay * cmat + c_upd
        n_ref[h:h + 1, :] = decay * nvec + jnp.sum(k * wk, axis=0, keepdims=True)
        m_ref[:, h:h + 1] = m_last
        yield
        return (s_v + wg * qc) / jnp.maximum(jnp.abs(den), jnp.exp(-mt))

    def finish(ys):
        o = p_ref[:, 3 * GW:4 * GW]
        y_ref[...] = _head_norm(jnp.concatenate(ys, axis=1), nw_ref[...], _sigmoid(o)).astype(y_ref.dtype)

    return [head(h) for h in range(ML_H)], finish


def _hgrn_parts(p_ref, lb_ref, nw_ref, y_ref, s_ref, L):
    c = pl.program_id(1)

    @pl.when(c == 0)
    def _():
        s_ref[...] = jnp.zeros_like(s_ref)

    lb = lb_ref[...]
    qa = _silu(p_ref[:, 0:GW])
    fg = lb + (1.0 - lb) * _sigmoid(p_ref[:, GW:2 * GW])
    ka = 1.0 - fg
    va = p_ref[:, 2 * GW:3 * GW]
    cga = _mm_hi(_lower(L, True).astype(F32), jnp.log(fg))
    ones = jnp.ones((HD, HD), BF16)
    t3, s3 = _iota((SUB, SUB, HD), 0), _iota((SUB, SUB, HD), 1)
    neg = jnp.float32(-jnp.inf)
    def head(h):
        sl = slice(h * HD, (h + 1) * HD)
        q, k, v, cg = qa[:, sl], ka[:, sl], va[:, sl], cga[:, sl]
        st = s_ref[h]
        inter = _mm_nt(q * jnp.exp(cg), st)
        cg_last = cg[L - 1:L, :]
        s_upd = _mm_tn(v, k * jnp.exp(cg_last - cg))
        a3s, a_offs = [], []
        for i in range(L // SUB):
            lo = i * SUB
            qi, ki, cgi = q[lo:lo + SUB], k[lo:lo + SUB], cg[lo:lo + SUB]
            e3 = jnp.exp(jnp.where(s3 <= t3, cgi[:, None, :] - cgi[None, :, :], neg))
            x3 = qi[:, None, :] * ki[None, :, :] * e3
            a3s.append(jnp.dot(x3.reshape(SUB * SUB, HD).astype(BF16), ones, preferred_element_type=F32))
            if i > 0:
                ref = cg[lo - 1:lo, :]
                a_offs.append(_mm_nt(qi * jnp.exp(cgi - ref), k[0:lo] * jnp.exp(ref - cg[0:lo])))
        yield
        s_ref[h] = st * jnp.exp(cg_last) + s_upd
        offs = [_mm(a_off, v[0:(i + 1) * SUB]) for i, a_off in enumerate(a_offs)]
        yield
        rows_out = []
        for i in range(L // SUB):
            lo = i * SUB
            oi = inter[lo:lo + SUB] + jnp.sum(a3s[i].reshape(SUB, SUB, HD) * v[lo:lo + SUB][None, :, :], axis=1)
            rows_out.append(oi if i == 0 else oi + offs[i - 1])
        return jnp.concatenate(rows_out, axis=0)

    def finish(ys):
        g = p_ref[:, 3 * GW:4 * GW]
        y_ref[...] = _head_norm(jnp.concatenate(ys, axis=1), nw_ref[...], _silu(g)).astype(y_ref.dtype)

    return [head(h) for h in range(HG_H)], finish


def _gdn_parts(p_ref, gate_forms, cw_ref, nw_ref, y_ref, cv_ref, s_ref, ext_ref, L):
    c = pl.program_id(1)
    w3 = 3 * GW
    _carry_rows(ext_ref, p_ref[:, 0:w3], c == 0, L)
    conv = (cw_ref[3:4, :] * ext_ref[8:8 + L, :] + cw_ref[2:3, :] * ext_ref[7:7 + L, :]
            + cw_ref[1:2, :] * ext_ref[6:6 + L, :] + cw_ref[0:1, :] * ext_ref[5:5 + L, :])
    cv_ref[...] = ext_ref[L + 5:L + 8, :]
    ext_ref[0:8, :] = ext_ref[L:L + 8, :]

    @pl.when(c == 0)
    def _():
        s_ref[...] = jnp.zeros_like(s_ref)

    qkv = _silu(conv)
    cols, rows, ccols, crows = gate_forms
    strict, causal = _lower(L, False), _lower(L, True)
    neg = jnp.float32(-jnp.inf)
    qa = qkv[:, 0:GW]
    ka = qkv[:, GW:2 * GW]
    qa = qa * lax.rsqrt(jnp.maximum(_seg_sum(qa * qa, HD), 1e-12)) * (HD ** -0.5)
    ka = ka * lax.rsqrt(jnp.maximum(_seg_sum(ka * ka, HD), 1e-12))

    def head(h):
        sl = slice(h * HD, (h + 1) * HD)
        q, k, v = qa[:, sl], ka[:, sl], qkv[:, 2 * GW + h * HD:2 * GW + (h + 1) * HD]
        beta = cols[:, COL_GDN_B + h:COL_GDN_B + h + 1]
        cg_col = ccols[:, COL_GDN_A + h:COL_GDN_A + h + 1]
        cg_row = crows[COL_GDN_A + h:COL_GDN_A + h + 1, :]
        diff = cg_col - cg_row
        s0 = s_ref[h]
        kb = k * beta
        g_kk, g_qk = _mm_nt(kb, k), _mm_nt(q, k)
        q_s0 = _mm(q * jnp.exp(cg_col), s0)
        yield
        n_mat = g_kk * jnp.exp(jnp.where(strict, diff, neg))
        m = yield from _tri_inv_minus_eye(n_mat, L)
        rhs = jnp.concatenate([v * beta, kb * jnp.exp(cg_col)], axis=1)
        m_rhs = _mm(m, rhs)
        yield
        uw = rhs + m_rhs
        w_s0 = _mm(uw[:, HD:2 * HD], s0)
        yield
        u2 = uw[:, 0:HD] - w_s0
        qk = g_qk * jnp.exp(jnp.where(causal, diff, neg))
        qk_u = _mm(qk, u2)
        cg_last = cg_col[L - 1:L, :]
        s_upd = _mm_tn(k * jnp.exp(cg_last - cg_col), u2)
        yield
        s_ref[h] = jnp.exp(cg_last) * s0 + s_upd
        return q_s0 + qk_u

    def finish(ys):
        g = p_ref[:, w3:w3 + GW]
        y_ref[...] = _head_norm(jnp.concatenate(ys, axis=1), nw_ref[...], _silu(g)).astype(y_ref.dtype)

    return [head(h) for h in range(GDN_H)], finish


def _mixers_kernel(pr_ref, pg_ref, pm_ref, ph_ref, pd_ref,
                   mu_ref, vec_ref, wup_ref, aup_ref, gup_ref, gp_ref, mlw_ref, lb_ref, hgw_ref, cw_ref, gdw_ref,
                   yr_ref, sh_ref, wkv_ref, ym_ref, mc_ref, mn_ref, mm_ref, yh_ref, hg_ref, yg_ref, cv_ref, gd_ref,
                   ext_r, ext_g, *, L, nseq):
    parts = []
    for b in range(nseq):
        gate_forms = _gate_forms(pg_ref.at[b], gp_ref, L)
        parts += [
            _rwkv_parts(pr_ref.at[b], mu_ref, vec_ref, wup_ref, aup_ref, gup_ref, yr_ref.at[b], sh_ref.at[b],
                        wkv_ref.at[b], ext_r.at[b], L),
            _mlstm_parts(pm_ref.at[b], gate_forms, mlw_ref, ym_ref.at[b], mc_ref.at[b], mn_ref.at[b],
                         mm_ref.at[b], L),
            _hgrn_parts(ph_ref.at[b], lb_ref, hgw_ref, yh_ref.at[b], hg_ref.at[b], L),
            _gdn_parts(pd_ref.at[b], gate_forms, cw_ref, gdw_ref, yg_ref.at[b], cv_ref.at[b], gd_ref.at[b],
                       ext_g.at[b], L),
        ]
    results = _round_robin(g for gens, _ in parts for g in gens)
    at = 0
    for gens, finish in parts:
        finish(results[at:at + len(gens)])
        at += len(gens)


def _prompt_mixers(p3, lp):
    b, t, _ = p3.shape
    L = math.gcd(t, CHUNK)
    nseq = math.gcd(b, MIX_NSEQ)
    grid = (b // nseq, t // L)
    cp = _cparams(("parallel", "arbitrary"))
    col = lambda width, idx: pl.BlockSpec((nseq, L, width), lambda i, c: (i, c, idx))
    const2 = lambda shape: pl.BlockSpec(shape, lambda i, c: (0, 0))
    y_spec = pl.BlockSpec((nseq, L, GW), lambda i, c: (i, c, 0))
    y_shape = jax.ShapeDtypeStruct((b, t, GW), BF16)
    state = lambda *s: (pl.BlockSpec((nseq,) + s, lambda i, c: (i,) + (0,) * len(s)),
                        jax.ShapeDtypeStruct((b,) + s, F32))
    gates = col(GATES_W, GATES_OFF // GATES_W)
    y_out = (y_spec, y_shape)
    outs = [y_out, state(1, RWKV_PROJ), state(RWKV_H, RWKV_HD, RWKV_HD),
            y_out, state(ML_H, HD, HD), state(ML_H, HD), state(1, 128),
            y_out, state(HG_H, HD, HD),
            y_out, state(CONV_W - 1, 3 * GW), state(GDN_H, HD, HD)]
    specs, shapes = zip(*outs)
    yr, n_sh, n_wkv, ym, n_c, n_n, n_m, yh, n_hg, yg, n_cv, n_gd = pl.pallas_call(
        functools.partial(_mixers_kernel, L=L, nseq=nseq), grid=grid,
        in_specs=[col(RWKV_PROJ, 0), gates, col(4 * GW, 1), col(4 * GW, 2), col(4 * GW, 3),
                  const2((1, RWKV_PROJ)), const2((8, GW)), const2((64, GW)), const2((64, GW)), const2((128, GW)),
                  const2((8, 128)), const2((1, GW)), const2((1, GW)), const2((1, GW)),
                  const2((CONV_W, 3 * GW)), const2((1, GW))],
        out_specs=list(specs), out_shape=list(shapes),
        scratch_shapes=[pltpu.VMEM((nseq, L + 8, RWKV_PROJ), F32), pltpu.VMEM((nseq, L + 8, 3 * GW), F32)],
        compiler_params=cp, name="mixers",
    )(p3, p3, p3, p3, p3, lp["rwkv_mu"], lp["rwkv_vec"], lp["rwkv_w_up"], lp["rwkv_a_up"], lp["rwkv_g_up"],
      lp["gate_par"], lp["ml_norm"], lp["hg_lb"], lp["hg_norm"], lp["gdn_conv_w"], lp["gdn_norm"])

    ys = [y.reshape(b * t, GW) for y in (yr, ym, yh, yg)]
    states = (n_sh, n_wkv, n_c, n_n, n_m[:, 0, :ML_H], jnp.swapaxes(n_hg, -1, -2), n_cv, n_gd)
    return ys, states


DEC_BS = 8


def _col_to_row(col):
    n = col.shape[0]
    return jnp.sum(jnp.where(_iota((n, n), 0) == _iota((n, n), 1), col, 0.0), axis=0, keepdims=True)


def _sub_sum(x):
    return jnp.sum(x, axis=0, keepdims=True)


def _decode_kernel(p_ref, sh_ref, wkv_ref, mc_ref, mn_ref, mm_ref, hg_ref, cv_ref, gd_ref,
                   mu_ref, vec_ref, wup_ref, aup_ref, gup_ref, gp_ref, mlw_ref, lb_ref, hgw_ref,
                   cw_ref, gdw_ref,
                   yr_ref, ym_ref, yh_ref, yg_ref,
                   nsh_ref, nwkv_ref, nmc_ref, nmn_ref, nmm_ref, nhg_ref, ncv_ref, ngd_ref):
    bs = DEC_BS
    w3 = 3 * GW
    pr = p_ref[:, 0:RWKV_PROJ]
    xs = pr + (sh_ref[...] - pr) * mu_ref[...]
    nsh_ref[...] = pr
    w0, a0, k_k, k_a, r_k, ln_w, ln_b = [vec_ref[i:i + 1, :] for i in range(7)]
    r, k, v = xs[:, 0:GW], xs[:, GW:2 * GW], xs[:, 2 * GW:3 * GW]
    dw, da, dg = xs[:, 1536:1600], xs[:, 1600:1664], xs[:, 1664:1792]
    logw = -_softplus(-(w0 + _mm(jnp.tanh(dw), wup_ref[...]))) - 0.5
    wdec = jnp.exp(-jnp.exp(logw))
    a = _sigmoid(a0 + _mm(da, aup_ref[...]))
    g_r = _mm(_sigmoid(dg), gup_ref[...])
    kk = k * k_k
    kk = kk * lax.rsqrt(jnp.maximum(_seg_sum(kk * kk, RWKV_HD), 1e-12))
    k2 = k * (1.0 + (a - 1.0) * k_a)
    ra, rb = -kk, kk * a

    gates = p_ref[:, GATES_OFF:GATES_OFF + 128] + gp_ref[0:1, :]
    ml_off = GATES_OFF + GATES_W
    mq = p_ref[:, ml_off:ml_off + GW]
    mk = p_ref[:, ml_off + GW:ml_off + 2 * GW] * (HD ** -0.5)
    mv = p_ref[:, ml_off + 2 * GW:ml_off + 3 * GW]
    mo = p_ref[:, ml_off + 3 * GW:ml_off + 4 * GW]
    m_ig = gates
    m_lf = jnp.minimum(gates, 0.0) - jnp.log(1.0 + jnp.exp(-jnp.abs(gates)))

    hg_off = ml_off + 4 * GW
    lb = lb_ref[...]
    hq = _silu(p_ref[:, hg_off:hg_off + GW])
    hfg = lb + (1.0 - lb) * _sigmoid(p_ref[:, hg_off + GW:hg_off + 2 * GW])
    hv = p_ref[:, hg_off + 2 * GW:hg_off + 3 * GW]
    hgate = p_ref[:, hg_off + 3 * GW:hg_off + 4 * GW]

    gd_off = hg_off + 4 * GW
    cur = p_ref[:, gd_off:gd_off + w3]
    conv = (cw_ref[3:4, :] * cur + cw_ref[2:3, :] * cv_ref[:, 2 * w3:3 * w3]
            + cw_ref[1:2, :] * cv_ref[:, w3:2 * w3] + cw_ref[0:1, :] * cv_ref[:, 0:w3])
    ncv_ref[:, 0:2 * w3] = cv_ref[:, w3:3 * w3]
    ncv_ref[:, 2 * w3:3 * w3] = cur
    qkv = _silu(conv)
    gq, gk, gv = qkv[:, 0:GW], qkv[:, GW:2 * GW], qkv[:, 2 * GW:3 * GW]
    gq = gq * lax.rsqrt(jnp.maximum(_seg_sum(gq * gq, HD), 1e-12)) * (HD ** -0.5)
    gk = gk * lax.rsqrt(jnp.maximum(_seg_sum(gk * gk, HD), 1e-12))
    ggate = p_ref[:, gd_off + w3:gd_off + w3 + GW]
    g_beta = _sigmoid(gates)
    g_dec = jnp.exp(-jnp.exp(gp_ref[1:2, :]) * _softplus(gates))

    blocks = ([mv[:, h * HD:(h + 1) * HD] for h in range(ML_H)]
              + [hq[:, h * HD:(h + 1) * HD] for h in range(HG_H)]
              + [hfg[:, h * HD:(h + 1) * HD] for h in range(HG_H)]
              + [gq[:, h * HD:(h + 1) * HD] for h in range(GDN_H)]
              + [gk[:, h * HD:(h + 1) * HD] for h in range(GDN_H)]
              + [v[:, j * 128:(j + 1) * 128] for j in range(4)])
    xt = _mm_nt_hi(_eye(128, 128), jnp.concatenate(blocks, axis=0))
    colf = lambda blk, s: xt[:, blk * bs + s:blk * bs + s + 1]

    yr_rows, ym_rows, yh_rows, yg_rows = [], [], [], []
    for s in range(bs):
        row = lambda arr, lo, width: arr[s:s + 1, lo:lo + width]
        parts = []
        for h in range(RWKV_H):
            lo = h * RWKV_HD
            st = wkv_ref[s, h]
            v_col = colf(20 + h // 2, s)[(h % 2) * 64:(h % 2) * 64 + 64, :]
            sa = _lane_sum(st * row(ra, lo, 64))
            st = st * row(wdec, lo, 64) + sa * row(rb, lo, 64) + v_col * row(k2, lo, 64)
            nwkv_ref[s, h] = st
            parts.append(_col_to_row(_lane_sum(st * row(r, lo, 64))))
        yr_rows.append(jnp.concatenate(parts, axis=1))
        parts = []
        for h in range(ML_H):
            lo = h * HD
            q_r, k_r = row(mq, lo, HD), row(mk, lo, HD)
            ig, lf = m_ig[s:s + 1, COL_ML_I + h:COL_ML_I + h + 1], m_lf[s:s + 1, COL_ML_F + h:COL_ML_F + h + 1]
            cmat, nvec, m_prev = mc_ref[s, h], mn_ref[s, h:h + 1, :], mm_ref[s:s + 1, h:h + 1]
            gsc = lf + m_prev
            mt = jnp.maximum(gsc, ig)
            wi, wg = jnp.exp(ig - mt), jnp.exp(gsc - mt)
            sc = _lane_sum(q_r * k_r) * wi
            v_col = colf(h, s)
            num = sc * v_col + wg * _lane_sum(cmat * q_r)
            den = sc + wg * _lane_sum(nvec * q_r)
            parts.append(_col_to_row(num / jnp.maximum(jnp.abs(den), jnp.exp(-mt))))
            nmc_ref[s, h] = wg * cmat + (wi * v_col) * k_r
            nmn_ref[s, h:h + 1, :] = wg * nvec + wi * k_r
            nmm_ref[s:s + 1, h:h + 1] = mt
        ym_rows.append(jnp.concatenate(parts, axis=1))
        parts = []
        for h in range(HG_H):
            lo = h * HD
            st = hg_ref[s, h]
            q_col, fg_col = colf(4 + h, s), colf(8 + h, s)
            q_r, fg_r, v_r = row(hq, lo, HD), row(hfg, lo, HD), row(hv, lo, HD)
            parts.append(_sub_sum(st * (q_col * fg_col)) + _lane_sum(q_r * (1.0 - fg_r)) * v_r)
            nhg_ref[s, h] = fg_col * st + (1.0 - fg_col) * v_r
        yh_rows.append(jnp.concatenate(parts, axis=1))
        parts = []
        for h in range(GDN_H):
            lo = h * HD
            st = gd_ref[s, h]
            q_col, k_col = colf(12 + h, s), colf(16 + h, s)
            beta = g_beta[s:s + 1, COL_GDN_B + h:COL_GDN_B + h + 1]
            dec = g_dec[s:s + 1, COL_GDN_A + h:COL_GDN_A + h + 1]
            u2 = beta * row(gv, lo, HD) - _sub_sum(st * (k_col * (beta * dec)))
            qk = _lane_sum(row(gq, lo, HD) * row(gk, lo, HD))
            parts.append(dec * _sub_sum(st * q_col) + qk * u2)
            ngd_ref[s, h] = dec * st + k_col * u2
        yg_rows.append(jnp.concatenate(parts, axis=1))

    yr = jnp.concatenate(yr_rows, axis=0)
    mean = _seg_sum(yr, RWKV_HD) * (1.0 / RWKV_HD)
    dev = yr - mean
    var = _seg_sum(dev * dev, RWKV_HD) * (1.0 / RWKV_HD)
    yr = dev * lax.rsqrt(var + RWKV_LN_EPS) * ln_w + ln_b
    bonus = _seg_sum(r * k2 * r_k, RWKV_HD) * v
    yr_ref[...] = ((yr + bonus) * g_r).astype(yr_ref.dtype)
    ym_ref[...] = _head_norm(jnp.concatenate(ym_rows, axis=0), mlw_ref[...], _sigmoid(mo)).astype(ym_ref.dtype)
    yh_ref[...] = _head_norm(jnp.concatenate(yh_rows, axis=0), hgw_ref[...], _silu(hgate)).astype(yh_ref.dtype)
    yg_ref[...] = _head_norm(jnp.concatenate(yg_rows, axis=0), gdw_ref[...], _silu(ggate)).astype(yg_ref.dtype)


def _sample_mixers(p, states, layer, lp):
    n = p.shape[0]
    bs = DEC_BS

    def st_spec(arr):
        tail = arr.shape[2:]
        return pl.BlockSpec((None, bs) + tail, lambda i: (layer, i) + (0,) * len(tail))

    const2 = lambda shape: pl.BlockSpec(shape, lambda i: (0, 0))
    st_specs = [st_spec(a) for a in states]
    y_spec = pl.BlockSpec((bs, GW), lambda i: (i, 0))
    y_shape = jax.ShapeDtypeStruct((n, GW), BF16)
    res = pl.pallas_call(
        _decode_kernel,
        grid=(n // bs,),
        in_specs=[pl.BlockSpec((bs, PROJ_PAD), lambda i: (i, 0))] + st_specs
        + [const2((1, RWKV_PROJ)), const2((8, GW)), const2((64, GW)), const2((64, GW)), const2((128, GW)),
           const2((8, 128)), const2((1, GW)), const2((1, GW)), const2((1, GW)),
           const2((CONV_W, 3 * GW)), const2((1, GW))],
        out_specs=[y_spec] * 4 + st_specs,
        out_shape=[y_shape] * 4 + [jax.ShapeDtypeStruct(a.shape, F32) for a in states],
        input_output_aliases={1 + k: 4 + k for k in range(len(states))},
        compiler_params=_cparams(("arbitrary",)),
        name="decode_mixers",
    )(p, *states, lp["rwkv_mu"], lp["rwkv_vec"], lp["rwkv_w_up"], lp["rwkv_a_up"], lp["rwkv_g_up"],
      lp["gate_par"], lp["ml_norm"], lp["hg_lb"], lp["hg_norm"], lp["gdn_conv_w"], lp["gdn_norm"])
    return list(res[:4]), tuple(res[4:])


def _layer_params(l, a):
    gate_par = jnp.zeros((8, 128), F32)
    gate_par = gate_par.at[0, COL_ML_I:COL_ML_I + 4].set(a["ml_i_bias"][l])
    gate_par = gate_par.at[0, COL_ML_F:COL_ML_F + 4].set(a["ml_f_bias"][l])
    gate_par = gate_par.at[0, COL_GDN_A:COL_GDN_A + 4].set(a["gdn_dt_bias"][l])
    gate_par = gate_par.at[1, COL_GDN_A:COL_GDN_A + 4].set(a["gdn_a_log"][l])
    vec = jnp.stack([a[k][l] for k in ("rwkv_w0", "rwkv_a0", "rwkv_k_k", "rwkv_k_a", "rwkv_r_k",
                                        "rwkv_ln_w", "rwkv_ln_b")] + [jnp.zeros((GW,), F32)])
    return {
        "rwkv_mu": a["rwkv_mu"][l].reshape(1, RWKV_PROJ), "rwkv_vec": vec,
        "rwkv_w_up": a["rwkv_w_up"][l], "rwkv_a_up": a["rwkv_a_up"][l], "rwkv_g_up": a["rwkv_g_up"][l],
        "gate_par": gate_par, "ml_norm": a["ml_norm"][l].reshape(1, GW),
        "hg_lb": a["hg_lbs"][l].reshape(1, GW), "hg_norm": a["hg_norm"][l].reshape(1, GW),
        "gdn_conv_w": a["gdn_conv_w"][l], "gdn_norm": a["gdn_norm"][l].reshape(1, GW),
    }


def _pad_w_in(w_in):
    z = jnp.zeros(w_in.shape[:2] + (GATES_W - 16,), w_in.dtype)
    parts = [w_in[..., 0:1792], w_in[..., 3840:3848], w_in[..., 7944:7952], z,
             w_in[..., 1792:3840], w_in[..., 3848:5896], w_in[..., 5896:7944]]
    return jnp.concatenate(parts, axis=-1).astype(BF16)


def kernel(x_prompt, x_sample, state_rwkv_shift, state_rwkv_wkv, state_mlstm_c, state_mlstm_n,
           state_mlstm_m, state_hgrn, state_gdn_conv, state_gdn, c_prompt, c_sample,
           ada_w, ada_b, norm1, norm2, norm_f, w_in, w_out,
           rwkv_mu, rwkv_w0, rwkv_w_up, rwkv_a0, rwkv_a_up, rwkv_g_up, rwkv_k_k, rwkv_k_a, rwkv_r_k,
           rwkv_ln_w, rwkv_ln_b, ml_i_bias, ml_f_bias, ml_norm, hg_lb, hg_norm,
           gdn_conv_w, gdn_a_log, gdn_dt_bias, gdn_norm,
           moe_w_group, moe_b_group, moe_w_router, moe_b_router, moe_w_gate, moe_w_up, moe_w_down):
    bp, t, _ = x_prompt.shape
    ns = x_sample.shape[0]
    lbs = jax.nn.softmax(hg_lb.astype(F32), axis=0)
    hg_lbs = jnp.cumsum(lbs, axis=0) - lbs[0]
    a = dict(rwkv_mu=rwkv_mu, rwkv_w0=rwkv_w0, rwkv_w_up=rwkv_w_up, rwkv_a0=rwkv_a0, rwkv_a_up=rwkv_a_up,
             rwkv_g_up=rwkv_g_up, rwkv_k_k=rwkv_k_k, rwkv_k_a=rwkv_k_a, rwkv_r_k=rwkv_r_k,
             rwkv_ln_w=rwkv_ln_w, rwkv_ln_b=rwkv_ln_b, ml_i_bias=ml_i_bias, ml_f_bias=ml_f_bias,
             ml_norm=ml_norm, hg_lbs=hg_lbs, hg_norm=hg_norm, gdn_conv_w=gdn_conv_w, gdn_a_log=gdn_a_log,
             gdn_dt_bias=gdn_dt_bias, gdn_norm=gdn_norm)
    w_in_p = _pad_w_in(w_in)
    w_out_b = w_out.astype(BF16)
    w_route = jnp.concatenate([moe_w_group, moe_w_router,
                               jnp.zeros((DEPTH, D_MODEL, 128 - N_GROUPS - N_EXPERTS), F32)], axis=-1)
    w_route_hi = w_route.astype(BF16)
    w_route_lo = (w_route - w_route_hi.astype(F32)).astype(BF16)
    w_route = jnp.concatenate([w_route_hi, w_route_lo], axis=-1)
    b_route = jnp.concatenate([moe_b_group, moe_b_router,
                               jnp.zeros((DEPTH, 128 - N_GROUPS - N_EXPERTS), F32)], axis=-1)

    mod = _ada(jnp.concatenate([c_prompt, c_sample], axis=0), ada_w, ada_b)
    sample_states = (state_rwkv_shift.reshape(DEPTH, ns, RWKV_PROJ), state_rwkv_wkv, state_mlstm_c,
                     state_mlstm_n, state_mlstm_m, state_hgrn,
                     state_gdn_conv.reshape(DEPTH, ns, (CONV_W - 1) * 3 * GW), state_gdn)

    xp = x_prompt.reshape(bp * t, D_MODEL)
    xs = x_sample.reshape(ns, D_MODEL)
    n_p = bp * t
    ff = None
    modp = mods = None
    new_p = []
    for l in range(DEPTH):
        lp = _layer_params(l, a)
        prev_modp, prev_mods = modp, mods
        modp = mod[l, :bp].reshape(bp, 1, 6 * D_MODEL)
        mods = mod[l, bp:].reshape(1, ns, 6 * D_MODEL)

        xp, pp = _in_proj(xp, modp, t, norm1[l], w_in_p[l], ff, 0, prev_modp)
        xs, ps = _in_proj(xs, mods, ns, norm1[l], w_in_p[l], ff, n_p, prev_mods)
        ysp, stp = _prompt_mixers(pp.reshape(bp, t, PROJ_PAD), lp)
        yss, sample_states = _sample_mixers(ps, sample_states, l, lp)
        new_p.append(stp)
        xp, xs, h2, rt = _out_proj(ysp, xp, modp, t, yss, xs, mods, norm2[l], w_out_b[l],
                                   w_route[l], b_route[l].reshape(1, 128))
        ff = _moe(h2, rt, n_p + ns, moe_w_gate, moe_w_up, moe_w_down, l)
    yp = _final(xp, ff, 0, modp, t, norm_f).reshape(bp, t, D_MODEL)
    ys = _final(xs, ff, n_p, mods, ns, norm_f).reshape(ns, 1, D_MODEL)
    prompt_states = tuple(jnp.stack([st[i] for st in new_p]) for i in range(8))
    s_sh, s_wkv, s_mc, s_mn, s_mm, s_hg, s_cv, s_gd = sample_states
    return (yp, ys) + prompt_states + (s_sh.reshape(DEPTH, ns, 1, RWKV_PROJ), s_wkv, s_mc, s_mn, s_mm, s_hg,
                                       s_cv.reshape(DEPTH, ns, CONV_W - 1, 3 * GW), s_gd)
```
